```python
import math
import jax, jax.numpy as jnp
from jax import lax
import numpy as np

D_MODEL = 1024
BATCH = 4
SEQ = 8192
DEPTH = 4

N_A = DEPTH // 2
N_B = DEPTH - N_A
HEAD_DIM = 64
N_HEADS = D_MODEL // HEAD_DIM
N_META = 16
BLOCK = 128
PAD = BLOCK - N_META
D_DECAY_LORA = max(32, int(round(1.8 * D_MODEL ** 0.5 / 32)) * 32)
D_AAA_LORA = max(32, int(round(1.8 * D_MODEL ** 0.5 / 32)) * 32)
D_MV_LORA = max(32, int(round(1.3 * D_MODEL ** 0.5 / 32)) * 32)
D_GATE_LORA = max(32, int(round(0.6 * D_MODEL ** 0.8 / 32)) * 32)
D_FF = int(math.ceil(8 * D_MODEL / 3 / 256)) * 256
CONV_WIDTH = 3
GN_EPS = 64e-5
RMS_EPS = 1e-6

kernel_name = "yoco_rwkv7_stickbreaking_hybrid"


def rms_norm(x, g):
    xf = x.astype(jnp.float32)
    y = xf * lax.rsqrt(jnp.mean(xf * xf, axis=-1, keepdims=True) + RMS_EPS) * g.astype(jnp.float32)
    return y.astype(x.dtype)


def token_shift(x):
    return jnp.pad(x, ((0, 0), (1, 0), (0, 0)))[:, :-1]


def conv_glu_ffn(x, w_up, conv_w, conv_b, w_down):
    u = x @ w_up
    c = u.shape[-1]
    u = lax.conv_general_dilated(
        u, conv_w.reshape(CONV_WIDTH, 1, c).astype(u.dtype),
        window_strides=(1,), padding=[(CONV_WIDTH - 1, 0)],
        dimension_numbers=('NWC', 'WIO', 'NWC'), feature_group_count=c) + conv_b
    gate, val = jnp.split(u, 2, axis=-1)
    return (jax.nn.silu(gate) * val) @ w_down


def rwkv7_time_mix(x, v_first, v_res, mix, w_r, w_k, w_v, w_o, w0, w1, w2,
                   a0, a1, a2, g1, g2, k_k, k_a, r_k, lnx_g, lnx_b):
    bsz, L, C = x.shape
    f32 = jnp.float32
    xx = token_shift(x) - x
    xr, xw, xk, xv, xa, xg = [x + xx * mix[i] for i in range(6)]
    r = xr @ w_r
    k = xk @ w_k
    v = xv @ w_v
    w_log = -jax.nn.softplus(-(w0 + jnp.tanh(xw @ w1) @ w2)) - 0.5
    a = jax.nn.sigmoid(a0 + (xa @ a1) @ a2)
    g = jax.nn.sigmoid(xg @ g1) @ g2

    def heads(t):
        return t.astype(f32).reshape(bsz, L, N_HEADS, HEAD_DIM)

    kk = heads(k * k_k)
    kk = kk / jnp.maximum(jnp.linalg.norm(kk, axis=-1, keepdims=True), 1e-12)
    k = k * (1 + (a - 1) * k_a)
    if v_res is None:
        v_first = v
    else:
        v0, v1, v2 = v_res
        v = v + (v_first - v) * jax.nn.sigmoid(v0 + (xv @ v1) @ v2)
    rh, kh, vh, ah = heads(r), heads(k), heads(v), heads(a)
    decay = jnp.exp(-jnp.exp(heads(w_log)))
    seq = tuple(jnp.moveaxis(t, 1, 0) for t in (rh, decay, kh, vh, -kk, kk * ah))

    def step(S, inp):
        r_t, w_t, k_t, v_t, a_t, b_t = inp
        sa = jnp.einsum('bhij,bhj->bhi', S, a_t)
        S = (S * w_t[:, :, None, :] + sa[..., None] * b_t[:, :, None, :]
             + v_t[..., None] * k_t[:, :, None, :])
        return S, jnp.einsum('bhij,bhj->bhi', S, r_t)

    S0 = jnp.zeros((bsz, N_HEADS, HEAD_DIM, HEAD_DIM), f32)
    _, y = lax.scan(step, S0, seq)
    y = jnp.moveaxis(y, 0, 1)
    mu = jnp.mean(y, axis=-1, keepdims=True)
    var = jnp.mean(jnp.square(y - mu), axis=-1, keepdims=True)
    y = ((y - mu) * lax.rsqrt(var + GN_EPS)).reshape(bsz, L, C) * lnx_g.astype(f32) + lnx_b.astype(f32)
    bonus = jnp.sum(rh * kh * r_k.astype(f32), axis=-1, keepdims=True) * vh
    y = (y + bonus.reshape(bsz, L, C)).astype(x.dtype)
    return (y * g) @ w_o, v_first


def to_padded_heads(t):
    bsz, L, _ = t.shape
    t = t.reshape(bsz, L, N_HEADS, HEAD_DIM)
    t = jnp.pad(t, ((0, 0), (PAD, 0), (0, 0), (0, 0)))
    return jnp.transpose(t, (0, 2, 1, 3))


def stick_breaking_attention(q, k, v):
    bsz, nh, lp, dh = q.shape
    nb = lp // BLOCK
    scale = dh ** -0.5
    key_pos = jnp.arange(lp)
    qb = jnp.moveaxis(q.reshape(bsz, nh, nb, BLOCK, dh), 2, 0)

    def one_block(args):
        q_blk, blk = args
        z = jnp.einsum('bhqd,bhkd->bhqk', q_blk, k).astype(jnp.float32) * scale
        q_pos = blk * BLOCK + jnp.arange(BLOCK)
        valid = (key_pos[None, :] < q_pos[:, None]) & (key_pos[None, :] >= PAD)
        log_1m = jnp.where(valid, jax.nn.log_sigmoid(-z), 0.0)
        log_after = lax.cumsum(log_1m, axis=3, reverse=True) - log_1m
        w = jnp.where(valid, jnp.exp(jax.nn.log_sigmoid(z) + log_after), 0.0)
        return jnp.einsum('bhqk,bhkd->bhqd', w.astype(v.dtype), v)

    out = lax.map(one_block, (qb, jnp.arange(nb)))
    out = jnp.moveaxis(out, 0, 2).reshape(bsz, nh, lp, dh)
    return jnp.transpose(out, (0, 2, 1, 3))[:, PAD:].reshape(bsz, lp - PAD, nh * dh)


def setup_inputs(seed: int = 0) -> dict:
    key = jax.random.key(seed)
    ks = jax.random.split(key, 40)
    counter = iter(range(40))

    def nk():
        return ks[next(counter)]

    def nrm(shape, s):
        return jax.random.normal(nk(), shape, jnp.float32) * s

    def uni(shape, lo, hi):
        return jax.random.uniform(nk(), shape, jnp.float32, lo, hi)

    D, F2 = D_MODEL, 2 * D_FF
    nv = max(N_A - 1, 0)
    return {
        "x": nrm((BATCH, SEQ, D), 1.0),
        "meta_tokens": nrm((N_META, D), 1.0),
        "norm_mix_g": 1.0 + nrm((DEPTH, D), 0.02),
        "norm_ffn_g": 1.0 + nrm((DEPTH, D), 0.02),
        "ffn_up": nrm((DEPTH, D, F2), D ** -0.5),
        "ffn_conv_w": nrm((DEPTH, CONV_WIDTH, F2), CONV_WIDTH ** -0.5),
        "ffn_conv_b": nrm((DEPTH, F2), 0.01),
        "ffn_down": nrm((DEPTH, D_FF, D), D_FF ** -0.5),
        "rw_mix": uni((N_A, 6, D), 0.0, 1.0),
        "rw_wr": nrm((N_A, D, D), D ** -0.5),
        "rw_wk": nrm((N_A, D, D), D ** -0.5),
        "rw_wv": nrm((N_A, D, D), D ** -0.5),
        "rw_wo": nrm((N_A, D, D), D ** -0.5),
        "rw_w0": uni((N_A, D), -6.5, -1.0),
        "rw_w1": nrm((N_A, D, D_DECAY_LORA), D ** -0.5),
        "rw_w2": nrm((N_A, D_DECAY_LORA, D), 0.1 * D_DECAY_LORA ** -0.5),
        "rw_a0": nrm((N_A, D), 0.1),
        "rw_a1": nrm((N_A, D, D_AAA_LORA), D ** -0.5),
        "rw_a2": nrm((N_A, D_AAA_LORA, D), 0.1 * D_AAA_LORA ** -0.5),
        "rw_g1": nrm((N_A, D, D_GATE_LORA), D ** -0.5),
        "rw_g2": nrm((N_A, D_GATE_LORA, D), D_GATE_LORA ** -0.5),
        "rw_kk": 0.85 + nrm((N_A, D), 0.05),
        "rw_ka": 1.0 + nrm((N_A, D), 0.05),
        "rw_rk": nrm((N_A, N_HEADS, HEAD_DIM), 0.1),
        "rw_lnx_g": 1.0 + nrm((N_A, D), 0.02),
        "rw_lnx_b": nrm((N_A, D), 0.01),
        "rw_v0": 1.0 + nrm((nv, D), 0.1),
        "rw_v1": nrm((nv, D, D_MV_LORA), D ** -0.5),
        "rw_v2": nrm((nv, D_MV_LORA, D), 0.1 * D_MV_LORA ** -0.5),
        "kv_norm_g": 1.0 + nrm((D,), 0.02),
        "sb_wk": nrm((D, D), D ** -0.5),
        "sb_wv": nrm((D, D), D ** -0.5),
        "sb_wq": nrm((N_B, D, D), D ** -0.5),
        "sb_wo": nrm((N_B, D, D), D ** -0.5),
        "final_norm_g": 1.0 + nrm((D,), 0.02),
    }


def reference(x, meta_tokens, norm_mix_g, norm_ffn_g, ffn_up, ffn_conv_w, ffn_conv_b, ffn_down,
              rw_mix, rw_wr, rw_wk, rw_wv, rw_wo, rw_w0, rw_w1, rw_w2, rw_a0, rw_a1, rw_a2,
              rw_g1, rw_g2, rw_kk, rw_ka, rw_rk, rw_lnx_g, rw_lnx_b, rw_v0, rw_v1, rw_v2,
              kv_norm_g, sb_wk, sb_wv, sb_wq, sb_wo, final_norm_g):
    bsz = x.shape[0]
    meta = jnp.broadcast_to(meta_tokens.astype(x.dtype)[None], (bsz, N_META, D_MODEL))
    h = jnp.concatenate([meta, x], axis=1)
    v_first = None
    k_sh = v_sh = None
    for layer in range(DEPTH):
        hn = rms_norm(h, norm_mix_g[layer])
        if layer < N_A:
            i = layer
            v_res = None if i == 0 else (rw_v0[i - 1], rw_v1[i - 1], rw_v2[i - 1])
            mix_out, v_first = rwkv7_time_mix(
                hn, v_first, v_res, rw_mix[i], rw_wr[i], rw_wk[i], rw_wv[i], rw_wo[i],
                rw_w0[i], rw_w1[i], rw_w2[i], rw_a0[i], rw_a1[i], rw_a2[i], rw_g1[i], rw_g2[i],
                rw_kk[i], rw_ka[i], rw_rk[i], rw_lnx_g[i], rw_lnx_b[i])
        else:
            j = layer - N_A
            if layer == N_A:
                kvn = rms_norm(h, kv_norm_g)
                k_sh = to_padded_heads(kvn @ sb_wk)
                v_sh = to_padded_heads(kvn @ sb_wv)
            q = to_padded_heads(hn @ sb_wq[j])
            mix_out = stick_breaking_attention(q, k_sh, v_sh) @ sb_wo[j]
        h = h + mix_out
        h = h + conv_glu_ffn(rms_norm(h, norm_ffn_g[layer]), ffn_up[layer], ffn_conv_w[layer],
                             ffn_conv_b[layer], ffn_down[layer])
    return rms_norm(h, final_norm_g)[:, N_META:, :]
```

```python
import functools

import jax
import jax.numpy as jnp
from jax import lax
from jax.experimental import pallas as pl
from jax.experimental.pallas import tpu as pltpu

HEAD_DIM = 64
N_META = 16
BLOCK = 128
PAD = BLOCK - N_META
CONV_WIDTH = 3
GN_EPS = 64e-5
RMS_EPS = 1e-6

LANES = 128
CHUNK = 64
CONV_HALO = 16
SHIFT_HALO = 8
LOG_W_FLOOR = -100.0
VMEM_LIMIT = 56 * 1024 * 1024

F32 = jnp.float32
BF16 = jnp.bfloat16
HI = lax.Precision.HIGHEST


def _dot(a, b, precision=None):
    return jnp.dot(a, b, preferred_element_type=F32, precision=precision)


def _dot_nt(a, b, precision=None):
    return lax.dot_general(a, b, (((1,), (1,)), ((), ())), preferred_element_type=F32, precision=precision)


def _dot_tn(a, b, precision=None):
    return lax.dot_general(a, b, (((0,), (0,)), ((), ())), preferred_element_type=F32, precision=precision)


def _rms_hat(x):
    return x * lax.rsqrt(jnp.mean(x * x, axis=-1, keepdims=True) + RMS_EPS)


def _params(sem):
    return pltpu.CompilerParams(dimension_semantics=sem, vmem_limit_bytes=VMEM_LIMIT)


def _row_in_seq(tile_rows, tile_index, seq_rows):
    rows = tile_index * tile_rows + lax.broadcasted_iota(jnp.int32, (tile_rows, 1), 0)
    return rows % seq_rows


def _norm_proj_kernel(n_groups, x_ref, *refs):
    g_refs = refs[:n_groups]
    w_refs = refs[n_groups:2 * n_groups]
    o_refs = refs[2 * n_groups:]
    xhat = _rms_hat(x_ref[...])
    for g_ref, w_ref, o_ref in zip(g_refs, w_refs, o_refs):
        xn = (xhat * g_ref[...]).astype(BF16)
        o_ref[...] = _dot(xn, w_ref[...]).astype(o_ref.dtype)


def norm_proj(h, gains, weights, out_dtype, tm):
    m, d = h.shape
    n = len(gains)
    in_specs = [pl.BlockSpec((tm, d), lambda i: (i, 0))]
    in_specs += [pl.BlockSpec((1, d), lambda i: (0, 0)) for _ in gains]
    in_specs += [pl.BlockSpec(w.shape, lambda i: (0, 0)) for w in weights]
    out_specs = [pl.BlockSpec((tm, w.shape[1]), lambda i: (i, 0)) for w in weights]
    out_shape = [jax.ShapeDtypeStruct((m, w.shape[1]), out_dtype) for w in weights]
    return pl.pallas_call(
        functools.partial(_norm_proj_kernel, n),
        grid=(m // tm,),
        in_specs=in_specs, out_specs=out_specs, out_shape=out_shape,
        compiler_params=_params(("parallel",)),
        name="norm_proj",
    )(h, *[g.reshape(1, d) for g in gains], *weights)


def _final_norm_kernel(x_ref, g_ref, o_ref):
    o_ref[...] = _rms_hat(x_ref[...]) * g_ref[...]


def final_norm(h, gain, tm):
    m, d = h.shape
    return pl.pallas_call(
        _final_norm_kernel,
        grid=(m // tm,),
        in_specs=[pl.BlockSpec((tm, d), lambda i: (i, 0)), pl.BlockSpec((1, d), lambda i: (0, 0))],
        out_specs=pl.BlockSpec((tm, d), lambda i: (i, 0)),
        out_shape=jax.ShapeDtypeStruct((m, d), F32),
        compiler_params=_params(("parallel",)),
        name="final_norm",
    )(h, gain.reshape(1, d))


def _rwkv_proj_kernel(has_vres, seq_rows, tm, *refs):
    if has_vres:
        (h_ref, halo_ref, gain_ref, mix_ref, vec_ref, wr_ref, wk_ref, wv_ref, w1_ref, w2_ref,
         a1_ref, a2_ref, g1_ref, g2_ref, v1_ref, v2_ref, vfirst_ref,
         r_ref, k_ref, v_ref, lw_ref, a_ref, g_ref) = refs
    else:
        (h_ref, halo_ref, gain_ref, mix_ref, vec_ref, wr_ref, wk_ref, wv_ref, w1_ref, w2_ref,
         a1_ref, a2_ref, g1_ref, g2_ref,
         r_ref, k_ref, v_ref, lw_ref, a_ref, g_ref) = refs
    i = pl.program_id(0)
    gain = gain_ref[...]
    hn = _rms_hat(h_ref[...]) * gain
    prev_last = (_rms_hat(halo_ref[...]) * gain)[SHIFT_HALO - 1:SHIFT_HALO, :]
    row = lax.broadcasted_iota(jnp.int32, (tm, 1), 0)
    shifted = jnp.where(row == 0, prev_last, pltpu.roll(hn, 1, 0))
    shifted = jnp.where(_row_in_seq(tm, i, seq_rows) == 0, 0.0, shifted)
    xx = shifted - hn

    def mixed(j):
        return (hn + xx * mix_ref[j:j + 1, :]).astype(BF16)

    w0, a0, v0 = vec_ref[0:1, :], vec_ref[1:2, :], vec_ref[2:3, :]
    r_ref[...] = _dot(mixed(0), wr_ref[...])
    t = jnp.tanh(_dot(mixed(1), w1_ref[...])).astype(BF16)
    w_in = w0 + _dot(t, w2_ref[...])
    lw_ref[...] = -jnp.exp(-jax.nn.softplus(-w_in) - 0.5)
    k_ref[...] = _dot(mixed(2), wk_ref[...])
    xv = mixed(3)
    v = _dot(xv, wv_ref[...])
    if has_vres:
        gate = jax.nn.sigmoid(v0 + _dot(_dot(xv, v1_ref[...]).astype(BF16), v2_ref[...]))
        v = v + (vfirst_ref[...] - v) * gate
    v_ref[...] = v
    a_ref[...] = jax.nn.sigmoid(a0 + _dot(_dot(mixed(4), a1_ref[...]).astype(BF16), a2_ref[...]))
    g_ref[...] = _dot(jax.nn.sigmoid(_dot(mixed(5), g1_ref[...])).astype(BF16), g2_ref[...])


def _pad_lora(w_in, w_out):
    hid = w_in.shape[1]
    hid_p = -(-hid // LANES) * LANES
    return (jnp.pad(w_in, ((0, 0), (0, hid_p - hid))).astype(BF16),
            jnp.pad(w_out, ((0, hid_p - hid), (0, 0))).astype(BF16))


def rwkv_proj(h, gain, mix, w0, a0, v0, wr, wk, wv, w1, w2, a1, a2, g1, g2, v1, v2, v_first,
              seq_rows, tm):
    m, d = h.shape
    has_vres = v_first is not None
    mix_p = jnp.pad(mix, ((0, 8 - mix.shape[0]), (0, 0)))
    vecs = jnp.stack([w0, a0, v0 if has_vres else jnp.zeros_like(w0)])
    vecs = jnp.pad(vecs, ((0, 8 - vecs.shape[0]), (0, 0)))
    w1p, w2p = _pad_lora(w1, w2)
    a1p, a2p = _pad_lora(a1, a2)
    g1p, g2p = _pad_lora(g1, g2)
    full = lambda a: pl.BlockSpec(a.shape, lambda i: (0, 0))
    tile = pl.BlockSpec((tm, d), lambda i: (i, 0))
    halo = pl.BlockSpec((SHIFT_HALO, d), lambda i: (jnp.maximum(i * (tm // SHIFT_HALO) - 1, 0), 0))
    args = [h, h, gain.reshape(1, d), mix_p, vecs, wr.astype(BF16), wk.astype(BF16), wv.astype(BF16),
            w1p, w2p, a1p, a2p, g1p, g2p]
    in_specs = [tile, halo] + [full(a) for a in args[2:]]
    if has_vres:
        v1p, v2p = _pad_lora(v1, v2)
        args += [v1p, v2p, v_first]
        in_specs += [full(v1p), full(v2p), tile]
    return pl.pallas_call(
        functools.partial(_rwkv_proj_kernel, has_vres, seq_rows, tm),
        grid=(m // tm,),
        in_specs=in_specs,
        out_specs=[tile] * 6,
        out_shape=[jax.ShapeDtypeStruct((m, d), F32)] * 6,
        compiler_params=_params(("parallel",)),
        name="rwkv_proj",
    )(*args)


def _rwkv_rec_kernel(r_ref, k_ref, v_ref, lw_ref, a_ref, kk_ref, ka_ref, rk_ref, lg_ref, lb_ref,
                     y_ref, s_ref):
    c = CHUNK

    @pl.when(pl.program_id(2) == 0)
    def _():
        s_ref[...] = jnp.zeros_like(s_ref)

    r, k_raw, v, lw, a_sig = r_ref[...], k_ref[...], v_ref[...], lw_ref[...], a_ref[...]
    lane = lax.broadcasted_iota(jnp.int32, (c, LANES), 1)
    head0 = lane < HEAD_DIM
    row_l = lax.broadcasted_iota(jnp.int32, (LANES, LANES), 0)
    col_l = lax.broadcasted_iota(jnp.int32, (LANES, LANES), 1)
    same_head = (row_l // HEAD_DIM) == (col_l // HEAD_DIM)
    ones_bd = same_head.astype(F32)
    eye_l = (row_l == col_l).astype(F32)
    row_c = lax.broadcasted_iota(jnp.int32, (c, c), 0)
    col_c = lax.broadcasted_iota(jnp.int32, (c, c), 1)
    strict = row_c > col_c
    incl = row_c >= col_c
    eye_c = (row_c == col_c).astype(F32)

    kk = k_raw * kk_ref[...]
    kk = kk * lax.rsqrt(jnp.maximum(_dot(kk * kk, ones_bd, HI), 1e-24))
    k = k_raw * (1.0 + (a_sig - 1.0) * ka_ref[...])
    a_vec = -kk
    b_vec = kk * a_sig

    cum = _dot(incl.astype(F32), lw, HI)
    cum_last = cum[c - 1:c, :]
    at = a_vec * jnp.exp(cum - lw)
    inv_p = jnp.exp(-cum)
    bt = b_vec * inv_p
    kt = k * inv_p
    rt = r * jnp.exp(cum)
    to_end = jnp.exp(cum_last - cum)
    bh = b_vec * to_end
    kh = k * to_end
    p_end = jnp.exp(cum_last)

    s0 = s_ref[...]
    at_s = _dot(at, s0, HI)
    rt_s = _dot(rt, s0, HI)

    def per_head(mask):
        at_m = jnp.where(mask, at, 0.0)
        rt_m = jnp.where(mask, rt, 0.0)
        a_ab = jnp.where(strict, _dot_nt(at_m, bt, HI), 0.0)
        a_ak = jnp.where(strict, _dot_nt(at_m, kt, HI), 0.0)
        a_rb = jnp.where(incl, _dot_nt(rt_m, bt, HI), 0.0)
        a_rk = jnp.where(incl, _dot_nt(rt_m, kt, HI), 0.0)
        t_inv = eye_c + a_ab
        pw = a_ab
        for _ in range(c.bit_length() - 2):
            pw = _dot(pw, pw, HI)
            t_inv = t_inv + _dot(t_inv, pw, HI)
        u = _dot(t_inv, at_s + _dot(a_ak, v, HI), HI)
        y = rt_s + _dot(jnp.concatenate([a_rb, a_rk], axis=1), jnp.concatenate([u, v], axis=0), HI)
        return u, y

    u0, y0 = per_head(head0)
    u1, y1 = per_head(jnp.logical_not(head0))
    u = jnp.where(head0, u0, u1)
    y = jnp.where(head0, y0, y1)

    upd = _dot_tn(jnp.concatenate([bh, kh], axis=0), jnp.concatenate([u, v], axis=0), HI)
    s_ref[...] = _dot(eye_l * p_end, s0, HI) + jnp.where(same_head, upd, 0.0)

    inv_n = 1.0 / HEAD_DIM
    mu = _dot(y, ones_bd, HI) * inv_n
    dev = y - mu
    var = _dot(dev * dev, ones_bd, HI) * inv_n
    bonus = _dot(r * k * rk_ref[...], ones_bd, HI) * v
    y_ref[...] = dev * lax.rsqrt(var + GN_EPS) * lg_ref[...] + lb_ref[...] + bonus


def rwkv_recurrence(r, k, v, lw, a, k_k, k_a, r_k, lnx_g, lnx_b, batch):
    m, d = r.shape
    n_chunks = m // batch // CHUNK
    n_pairs = d // LANES
    tile = pl.BlockSpec((CHUNK, LANES), lambda b, p, c: (b * n_chunks + c, p))
    vec = pl.BlockSpec((1, LANES), lambda b, p, c: (0, p))
    vecs = [t.reshape(1, d) for t in (k_k, k_a, r_k, lnx_g, lnx_b)]
    return pl.pallas_call(
        _rwkv_rec_kernel,
        grid=(batch, n_pairs, n_chunks),
        in_specs=[tile] * 5 + [vec] * 5,
        out_specs=tile,
        out_shape=jax.ShapeDtypeStruct((m, d), F32),
        scratch_shapes=[pltpu.VMEM((LANES, LANES), F32)],
        compiler_params=_params(("parallel", "parallel", "arbitrary")),
        name="rwkv_recurrence",
    )(r, k, v, lw, a, *vecs)


def _out_proj_kernel(has_gate, seq_rows, tm, *refs):
    if has_gate:
        y_ref, g_ref, h_ref, w_ref, o_ref = refs
        y = (y_ref[...] * g_ref[...]).astype(BF16)
    else:
        y_ref, h_ref, w_ref, o_ref = refs
        y = y_ref[...].astype(BF16)
    new_h = h_ref[...] + _dot(y, w_ref[...])
    keep = _row_in_seq(tm, pl.program_id(0), seq_rows) >= PAD
    o_ref[...] = jnp.where(keep, new_h, 0.0)


def out_proj(y, gate, h, w, seq_rows, tm):
    m, d = h.shape
    tile = pl.BlockSpec((tm, d), lambda i: (i, 0))
    args = [y] + ([gate] if gate is not None else []) + [h, w.astype(BF16)]
    in_specs = [tile] * (len(args) - 1) + [pl.BlockSpec(w.shape, lambda i: (0, 0))]
    return pl.pallas_call(
        functools.partial(_out_proj_kernel, gate is not None, seq_rows, tm),
        grid=(m // tm,),
        in_specs=in_specs, out_specs=tile,
        out_shape=jax.ShapeDtypeStruct((m, d), F32),
        compiler_params=_params(("parallel",)),
        name="out_proj",
    )(*args)


def _ffn_kernel(seq_rows, tm, h_ref, halo_ref, gain_ref, wg_ref, wv_ref, cwg_ref, cwv_ref,
                cbg_ref, cbv_ref, wd_ref, o_ref, xn_ref, acc_ref):
    i, j = pl.program_id(0), pl.program_id(1)

    @pl.when(j == 0)
    def _():
        gain = gain_ref[...]
        xn_ref[0:CONV_HALO, :] = (_rms_hat(halo_ref[...]) * gain).astype(BF16)
        xn_ref[CONV_HALO:, :] = (_rms_hat(h_ref[...]) * gain).astype(BF16)
        acc_ref[...] = jnp.zeros_like(acc_ref)

    xn = xn_ref[...]

    def conv(u, cw_ref, cb_ref):
        out = cb_ref[...] + u[CONV_HALO:, :] * cw_ref[CONV_WIDTH - 1:CONV_WIDTH, :]
        for tap in range(CONV_WIDTH - 1):
            back = CONV_WIDTH - 1 - tap
            out = out + u[CONV_HALO - back:CONV_HALO - back + tm, :] * cw_ref[tap:tap + 1, :]
        return out

    gate = conv(_dot(xn, wg_ref[...]), cwg_ref, cbg_ref)
    val = conv(_dot(xn, wv_ref[...]), cwv_ref, cbv_ref)
    act = (jax.nn.silu(gate) * val).astype(BF16)
    acc_ref[...] += _dot(act, wd_ref[...])

    @pl.when(j == pl.num_programs(1) - 1)
    def _():
        keep = _row_in_seq(tm, i, seq_rows) >= PAD
        o_ref[...] = jnp.where(keep, h_ref[...] + acc_ref[...], 0.0)


def conv_ffn(h, gain, w_up, conv_w, conv_b, w_down, seq_rows, tm, tf):
    m, d = h.shape
    d_ff = w_down.shape[0]
    nf = d_ff // tf
    cw = jnp.pad(conv_w, ((0, 8 - CONV_WIDTH), (0, 0)))
    cb = conv_b.reshape(1, 2 * d_ff)
    w_up = w_up.astype(BF16)
    tile = pl.BlockSpec((tm, d), lambda i, j: (i, 0))
    halo = pl.BlockSpec((CONV_HALO, d), lambda i, j: (jnp.maximum(i * (tm // CONV_HALO) - 1, 0), 0))
    return pl.pallas_call(
        functools.partial(_ffn_kernel, seq_rows, tm),
        grid=(m // tm, nf),
        in_specs=[
            tile, halo, pl.BlockSpec((1, d), lambda i, j: (0, 0)),
            pl.BlockSpec((d, tf), lambda i, j: (0, j)),
            pl.BlockSpec((d, tf), lambda i, j: (0, j + nf)),
            pl.BlockSpec((8, tf), lambda i, j: (0, j)),
            pl.BlockSpec((8, tf), lambda i, j: (0, j + nf)),
            pl.BlockSpec((1, tf), lambda i, j: (0, j)),
            pl.BlockSpec((1, tf), lambda i, j: (0, j + nf)),
            pl.BlockSpec((tf, d), lambda i, j: (j, 0)),
        ],
        out_specs=tile,
        out_shape=jax.ShapeDtypeStruct((m, d), F32),
        scratch_shapes=[pltpu.VMEM((tm + CONV_HALO, d), BF16), pltpu.VMEM((tm, d), F32)],
        compiler_params=_params(("parallel", "arbitrary")),
        name="conv_ffn",
    )(h, h, gain.reshape(1, d), w_up, w_up, cw, cw, cb, cb, w_down.astype(BF16))


def _split_bf16(x):
    hi = x.astype(BF16)
    return hi, (x - hi.astype(F32)).astype(BF16)


def _sb_attn_kernel(q_ref, k_ref, v_ref, o_ref):
    qi = pl.program_id(2)
    scale = HEAD_DIM ** -0.5
    q = q_ref[...]
    lane = lax.broadcasted_iota(jnp.int32, (BLOCK, LANES), 1)
    head0 = lane < HEAD_DIM
    q_heads = (jnp.where(head0, q, jnp.zeros_like(q)), jnp.where(head0, jnp.zeros_like(q), q))
    q_pos = qi * BLOCK + lax.broadcasted_iota(jnp.int32, (BLOCK, BLOCK), 0)
    k_off = lax.broadcasted_iota(jnp.int32, (BLOCK, BLOCK), 1)
    row_b = lax.broadcasted_iota(jnp.int32, (2 * BLOCK, 2 * BLOCK), 0) % BLOCK
    col_b = lax.broadcasted_iota(jnp.int32, (2 * BLOCK, 2 * BLOCK), 1)
    sum_mat = jnp.where((row_b > col_b) | (col_b >= BLOCK), 1.0, 0.0).astype(BF16)

    def cond(state):
        j, live, _, _, _ = state
        return (j >= 0) & (live > 0)

    def body(state):
        j, _, c0, c1, acc = state
        start = pl.multiple_of(j * BLOCK, BLOCK)
        kj = k_ref[pl.ds(start, BLOCK), :]
        vj = v_ref[pl.ds(start, BLOCK), :]
        k_pos = j * BLOCK + k_off
        valid = (k_pos < q_pos) & (k_pos >= PAD)
        outs, carries = [], []
        for qh, carry in zip(q_heads, (c0, c1)):
            z = _dot_nt(qh, kj) * scale
            sp = jnp.maximum(z, 0.0) + jnp.log1p(jnp.exp(-jnp.abs(z)))
            log_1m = jnp.where(valid, -sp, 0.0)
            sums = _dot(jnp.concatenate(_split_bf16(log_1m), axis=1), sum_mat)
            log_w = (z - sp) + sums[:, :BLOCK] + carry
            w = jnp.where(valid, jnp.exp(log_w), 0.0)
            outs.append(_dot(w.astype(BF16), vj))
            carries.append(carry + sums[:, BLOCK:])
        acc = acc + jnp.where(head0, outs[0], outs[1])
        live = (jnp.max(jnp.maximum(carries[0], carries[1])) > LOG_W_FLOOR).astype(jnp.int32)
        return j - 1, live, carries[0], carries[1], acc

    zeros = jnp.zeros((BLOCK, LANES), F32)
    state = lax.while_loop(cond, body, (qi, jnp.int32(1), zeros, zeros, zeros))
    o_ref[...] = state[-1]


def sb_attention(q, k, v, batch):
    m, d = q.shape
    seq_rows = m // batch
    nq = seq_rows // BLOCK
    n_pairs = d // LANES
    return pl.pallas_call(
        _sb_attn_kernel,
        grid=(batch, n_pairs, nq),
        in_specs=[
            pl.BlockSpec((BLOCK, LANES), lambda b, p, i: (b * nq + i, p)),
            pl.BlockSpec((seq_rows, LANES), lambda b, p, i: (b, p)),
            pl.BlockSpec((seq_rows, LANES), lambda b, p, i: (b, p)),
        ],
        out_specs=pl.BlockSpec((BLOCK, LANES), lambda b, p, i: (b * nq + i, p)),
        out_shape=jax.ShapeDtypeStruct((m, d), F32),
        compiler_params=_params(("parallel", "parallel", "arbitrary")),
        name="sb_attention",
    )(q, k, v)


def _row_tile(m, want):
    tm = want
    while m % tm:
        tm //= 2
    return tm


def kernel(x, meta_tokens, norm_mix_g, norm_ffn_g, ffn_up, ffn_conv_w, ffn_conv_b, ffn_down, rw_mix, rw_wr, rw_wk, rw_wv, rw_wo, rw_w0, rw_w1, rw_w2, rw_a0, rw_a1, rw_a2, rw_g1, rw_g2, rw_kk, rw_ka, rw_rk, rw_lnx_g, rw_lnx_b, rw_v0, rw_v1, rw_v2, kv_norm_g, sb_wk, sb_wv, sb_wq, sb_wo, final_norm_g):
    bsz, seq, d = x.shape
    depth = norm_mix_g.shape[0]
    n_a = rw_wr.shape[0]
    d_ff = ffn_down.shape[1]
    seq_rows = PAD + N_META + seq
    assert seq % BLOCK == 0 and d % LANES == 0 and seq_rows % CHUNK == 0
    m = bsz * seq_rows
    tm_big = _row_tile(m, 512)
    tm_proj = _row_tile(m, 256)
    tf = 256 if d_ff % 256 == 0 else LANES

    meta = jnp.broadcast_to(meta_tokens.astype(x.dtype)[None], (bsz, N_META, d))
    h = jnp.concatenate([jnp.zeros((bsz, PAD, d), x.dtype), meta, x], axis=1).reshape(m, d)

    v_first = None
    k_sh = v_sh = None
    for layer in range(depth):
        if layer < n_a:
            i = layer
            vres = (None, None, None) if i == 0 else (rw_v0[i - 1], rw_v1[i - 1], rw_v2[i - 1])
            r, k, v, lw, a, g = rwkv_proj(
                h, norm_mix_g[layer], rw_mix[i], rw_w0[i], rw_a0[i], vres[0], rw_wr[i], rw_wk[i], rw_wv[i],
                rw_w1[i], rw_w2[i], rw_a1[i], rw_a2[i], rw_g1[i], rw_g2[i], vres[1], vres[2],
                None if i == 0 else v_first, seq_rows, tm_proj)
            if i == 0:
                v_first = v
            y = rwkv_recurrence(r, k, v, lw, a, rw_kk[i], rw_ka[i], rw_rk[i].reshape(-1),
                                rw_lnx_g[i], rw_lnx_b[i], bsz)
            h = out_proj(y, g, h, rw_wo[i], seq_rows, tm_big)
        else:
            j = layer - n_a
            if layer == n_a:
                q, k_sh, v_sh = norm_proj(
                    h, [norm_mix_g[layer], kv_norm_g, kv_norm_g],
                    [sb_wq[j].astype(BF16), sb_wk.astype(BF16), sb_wv.astype(BF16)], BF16, tm_big)
            else:
                (q,) = norm_proj(h, [norm_mix_g[layer]], [sb_wq[j].astype(BF16)], BF16, tm_big)
            o = sb_attention(q, k_sh, v_sh, bsz)
            h = out_proj(o, None, h, sb_wo[j], seq_rows, tm_big)
        h = conv_ffn(h, norm_ffn_g[layer], ffn_up[layer], ffn_conv_w[layer], ffn_conv_b[layer],
                     ffn_down[layer], seq_rows, tm_big, tf)
    out = final_norm(h, final_norm_g, tm_big)
    return out.reshape(bsz, seq_rows, d)[:, PAD + N_META:, :]
```

```python
import functools

import jax
import jax.numpy as jnp
from jax import lax
from jax.experimental import pallas as pl
from jax.experimental.pallas import tpu as pltpu

HEAD_DIM = 64
N_META = 16
BLOCK = 128
PAD = BLOCK - N_META
CONV_WIDTH = 3
GN_EPS = 64e-5
RMS_EPS = 1e-6

LANES = 128
CHUNK = 64
CONV_HALO = 16
SHIFT_HALO = 8
ATTN_GROUPS = 4
LOG_W_FLOOR = -100.0
VMEM_LIMIT = 56 * 1024 * 1024

F32 = jnp.float32
BF16 = jnp.bfloat16
HI = lax.Precision.HIGHEST


def _dot(a, b, precision=None):
    return jnp.dot(a, b, preferred_element_type=F32, precision=precision)


def _dot_nt(a, b, precision=None):
    return lax.dot_general(a, b, (((1,), (1,)), ((), ())), preferred_element_type=F32, precision=precision)


def _dot_tn(a, b, precision=None):
    return lax.dot_general(a, b, (((0,), (0,)), ((), ())), preferred_element_type=F32, precision=precision)


def _rms_hat(x):
    return x * lax.rsqrt(jnp.mean(x * x, axis=-1, keepdims=True) + RMS_EPS)


def _params(sem):
    return pltpu.CompilerParams(dimension_semantics=sem, vmem_limit_bytes=VMEM_LIMIT)


def _row_in_seq(tile_rows, tile_index, seq_rows):
    rows = tile_index * tile_rows + lax.broadcasted_iota(jnp.int32, (tile_rows, 1), 0)
    return rows % seq_rows


def _norm_proj_kernel(n_groups, x_ref, *refs):
    g_refs = refs[:n_groups]
    w_refs = refs[n_groups:2 * n_groups]
    o_refs = refs[2 * n_groups:]
    xhat = _rms_hat(x_ref[...])
    for g_ref, w_ref, o_ref in zip(g_refs, w_refs, o_refs):
        xn = (xhat * g_ref[...]).astype(BF16)
        o_ref[...] = _dot(xn, w_ref[...]).astype(o_ref.dtype)


def norm_proj(h, gains, weights, out_dtype, tm):
    m, d = h.shape
    n = len(gains)
    in_specs = [pl.BlockSpec((tm, d), lambda i: (i, 0))]
    in_specs += [pl.BlockSpec((1, d), lambda i: (0, 0)) for _ in gains]
    in_specs += [pl.BlockSpec(w.shape, lambda i: (0, 0)) for w in weights]
    out_specs = [pl.BlockSpec((tm, w.shape[1]), lambda i: (i, 0)) for w in weights]
    out_shape = [jax.ShapeDtypeStruct((m, w.shape[1]), out_dtype) for w in weights]
    return pl.pallas_call(
        functools.partial(_norm_proj_kernel, n),
        grid=(m // tm,),
        in_specs=in_specs, out_specs=out_specs, out_shape=out_shape,
        compiler_params=_params(("parallel",)),
        name="norm_proj",
    )(h, *[g.reshape(1, d) for g in gains], *weights)


def _final_norm_kernel(x_ref, g_ref, o_ref):
    o_ref[...] = _rms_hat(x_ref[...]) * g_ref[...]


def final_norm(h, gain, tm):
    m, d = h.shape
    return pl.pallas_call(
        _final_norm_kernel,
        grid=(m // tm,),
        in_specs=[pl.BlockSpec((tm, d), lambda i: (i, 0)), pl.BlockSpec((1, d), lambda i: (0, 0))],
        out_specs=pl.BlockSpec((tm, d), lambda i: (i, 0)),
        out_shape=jax.ShapeDtypeStruct((m, d), F32),
        compiler_params=_params(("parallel",)),
        name="final_norm",
    )(h, gain.reshape(1, d))


def _rwkv_proj_kernel(has_vres, seq_rows, tm, *refs):
    if has_vres:
        (h_ref, halo_ref, gain_ref, mix_ref, vec_ref, wr_ref, wk_ref, wv_ref, w1_ref, w2_ref,
         a1_ref, a2_ref, g1_ref, g2_ref, v1_ref, v2_ref, vfirst_ref,
         r_ref, k_ref, v_ref, lw_ref, a_ref, g_ref) = refs
    else:
        (h_ref, halo_ref, gain_ref, mix_ref, vec_ref, wr_ref, wk_ref, wv_ref, w1_ref, w2_ref,
         a1_ref, a2_ref, g1_ref, g2_ref,
         r_ref, k_ref, v_ref, lw_ref, a_ref, g_ref) = refs
    i = pl.program_id(0)
    gain = gain_ref[...]
    hn = _rms_hat(h_ref[...]) * gain
    prev_last = (_rms_hat(halo_ref[...]) * gain)[SHIFT_HALO - 1:SHIFT_HALO, :]
    row = lax.broadcasted_iota(jnp.int32, (tm, 1), 0)
    shifted = jnp.where(row == 0, prev_last, pltpu.roll(hn, 1, 0))
    shifted = jnp.where(_row_in_seq(tm, i, seq_rows) == 0, 0.0, shifted)
    xx = shifted - hn

    def mixed(j):
        return (hn + xx * mix_ref[j:j + 1, :]).astype(BF16)

    w0, a0, v0 = vec_ref[0:1, :], vec_ref[1:2, :], vec_ref[2:3, :]
    r_ref[...] = _dot(mixed(0), wr_ref[...])
    t = jnp.tanh(_dot(mixed(1), w1_ref[...])).astype(BF16)
    w_in = w0 + _dot(t, w2_ref[...])
    lw_ref[...] = -jnp.exp(-jax.nn.softplus(-w_in) - 0.5)
    k_ref[...] = _dot(mixed(2), wk_ref[...])
    xv = mixed(3)
    v = _dot(xv, wv_ref[...])
    if has_vres:
        gate = jax.nn.sigmoid(v0 + _dot(_dot(xv, v1_ref[...]).astype(BF16), v2_ref[...]))
        v = v + (vfirst_ref[...] - v) * gate
    v_ref[...] = v
    a_ref[...] = jax.nn.sigmoid(a0 + _dot(_dot(mixed(4), a1_ref[...]).astype(BF16), a2_ref[...]))
    g_ref[...] = _dot(jax.nn.sigmoid(_dot(mixed(5), g1_ref[...])).astype(BF16), g2_ref[...])


def _pad_lora(w_in, w_out):
    hid = w_in.shape[1]
    hid_p = -(-hid // LANES) * LANES
    return (jnp.pad(w_in, ((0, 0), (0, hid_p - hid))).astype(BF16),
            jnp.pad(w_out, ((0, hid_p - hid), (0, 0))).astype(BF16))


def rwkv_proj(h, gain, mix, w0, a0, v0, wr, wk, wv, w1, w2, a1, a2, g1, g2, v1, v2, v_first,
              seq_rows, tm):
    m, d = h.shape
    has_vres = v_first is not None
    mix_p = jnp.pad(mix, ((0, 8 - mix.shape[0]), (0, 0)))
    vecs = jnp.stack([w0, a0, v0 if has_vres else jnp.zeros_like(w0)])
    vecs = jnp.pad(vecs, ((0, 8 - vecs.shape[0]), (0, 0)))
    w1p, w2p = _pad_lora(w1, w2)
    a1p, a2p = _pad_lora(a1, a2)
    g1p, g2p = _pad_lora(g1, g2)
    full = lambda a: pl.BlockSpec(a.shape, lambda i: (0, 0))
    tile = pl.BlockSpec((tm, d), lambda i: (i, 0))
    halo = pl.BlockSpec((SHIFT_HALO, d), lambda i: (jnp.maximum(i * (tm // SHIFT_HALO) - 1, 0), 0))
    args = [h, h, gain.reshape(1, d), mix_p, vecs, wr.astype(BF16), wk.astype(BF16), wv.astype(BF16),
            w1p, w2p, a1p, a2p, g1p, g2p]
    in_specs = [tile, halo] + [full(a) for a in args[2:]]
    if has_vres:
        v1p, v2p = _pad_lora(v1, v2)
        args += [v1p, v2p, v_first]
        in_specs += [full(v1p), full(v2p), tile]
    return pl.pallas_call(
        functools.partial(_rwkv_proj_kernel, has_vres, seq_rows, tm),
        grid=(m // tm,),
        in_specs=in_specs,
        out_specs=[tile] * 6,
        out_shape=[jax.ShapeDtypeStruct((m, d), F32)] * 6,
        compiler_params=_params(("parallel",)),
        name="rwkv_proj",
    )(*args)


def _split_bf16(x):
    hi = x.astype(BF16)
    return hi, (x - hi.astype(F32)).astype(BF16)


def _mm(kind, a, b, passes):
    fn = {"nn": _dot, "nt": _dot_nt, "tn": _dot_tn}[kind]
    if passes == 1:
        return fn(a.astype(BF16), b.astype(BF16))
    a_hi, a_lo = _split_bf16(a)
    b_hi, b_lo = _split_bf16(b)
    a_axis = 0 if kind == "tn" else 1
    b_axis = 1 if kind == "nt" else 0
    return fn(jnp.concatenate([a_hi, a_lo, a_hi], axis=a_axis), jnp.concatenate([b_hi, b_hi, b_lo], axis=b_axis))


def _rwkv_rec_kernel(r_ref, k_ref, v_ref, lw_ref, a_ref, kk_ref, ka_ref, rk_ref, lg_ref, lb_ref,
                     y_ref, s_ref):
    c = CHUNK
    n_pairs = r_ref.shape[1] // LANES

    @pl.when(pl.program_id(1) == 0)
    def _():
        s_ref[...] = jnp.zeros_like(s_ref)

    head0 = lax.broadcasted_iota(jnp.int32, (c, LANES), 1) < HEAD_DIM
    row2 = lax.broadcasted_iota(jnp.int32, (2 * c, 2 * c), 0)
    col2 = lax.broadcasted_iota(jnp.int32, (2 * c, 2 * c), 1)
    same_rows = (row2 // c) == (col2 // c)
    strict_bd = same_rows & (row2 % c > col2 % c)
    incl_bd = same_rows & (row2 % c >= col2 % c)
    eye2 = (row2 == col2).astype(F32)
    row_l = lax.broadcasted_iota(jnp.int32, (LANES, LANES), 0)
    col_l = lax.broadcasted_iota(jnp.int32, (LANES, LANES), 1)
    same_head = (row_l // HEAD_DIM) == (col_l // HEAD_DIM)
    ones_bd2 = jnp.concatenate([same_head, same_head], axis=0).astype(BF16)
    row_t = lax.broadcasted_iota(jnp.int32, (c, 2 * c), 0)
    col_t = lax.broadcasted_iota(jnp.int32, (c, 2 * c), 1)
    tri2 = (row_t >= col_t % c).astype(BF16)

    def head_sum(x):
        return _dot(jnp.concatenate(_split_bf16(x), axis=1), ones_bd2)

    def stack(x):
        return jnp.concatenate([jnp.where(head0, x, 0.0), jnp.where(head0, 0.0, x)], axis=0)

    def unstack(x):
        return jnp.where(head0, x[:c], x[c:])

    cum_all = _dot(tri2, jnp.concatenate(_split_bf16(lw_ref[...]), axis=0))

    pairs = range(n_pairs)
    sls = [slice(p * LANES, (p + 1) * LANES) for p in pairs]
    r_, v_, k_, at_, rt_, bh_, kh_, pe_, aa_ = [], [], [], [], [], [], [], [], []
    for sl in sls:
        r, k_raw, v, lw, a_sig = r_ref[:, sl], k_ref[:, sl], v_ref[:, sl], lw_ref[:, sl], a_ref[:, sl]
        cum = cum_all[:, sl]
        kk = k_raw * kk_ref[:, sl]
        kk = kk * lax.rsqrt(jnp.maximum(head_sum(kk * kk), 1e-24))
        k = k_raw * (1.0 + (a_sig - 1.0) * ka_ref[:, sl])
        b_vec = kk * a_sig
        cum_last = cum[c - 1:c, :]
        inv_p = jnp.exp(-cum)
        to_end = jnp.exp(cum_last - cum)
        at = -kk * jnp.exp(cum - lw)
        rt = r * jnp.exp(cum)
        aa_.append(_mm("nt", jnp.concatenate([stack(at), stack(rt)], axis=0),
                       jnp.concatenate([stack(b_vec * inv_p), stack(k * inv_p)], axis=0), 1))
        r_.append(r); v_.append(v); k_.append(k); at_.append(at); rt_.append(rt)
        bh_.append(b_vec * to_end); kh_.append(k * to_end); pe_.append(jnp.exp(cum_last))

    a_ab_ = [jnp.where(strict_bd, aa[:2 * c, :2 * c], 0.0) for aa in aa_]
    a_ak_ = [jnp.where(strict_bd, aa[:2 * c, 2 * c:], 0.0) for aa in aa_]
    a_r_ = [jnp.concatenate([jnp.where(incl_bd, aa[2 * c:, :2 * c], 0.0),
                             jnp.where(incl_bd, aa[2 * c:, 2 * c:], 0.0)], axis=1) for aa in aa_]

    t_inv_ = [eye2 + a for a in a_ab_]
    pw_ = a_ab_
    for _ in range(c.bit_length() - 2):
        pw_ = [_mm("nn", pw, pw, 1) for pw in pw_]
        t_inv_ = [t + _mm("nn", t, pw, 1) for t, pw in zip(t_inv_, pw_)]

    v2_ = [jnp.concatenate([v, v], axis=0) for v in v_]
    akv_ = [_mm("nn", a_ak, v2, 1) for a_ak, v2 in zip(a_ak_, v2_)]
    s0_ = [s_ref[p] for p in pairs]
    ars_ = [_mm("nt", jnp.concatenate([at, rt], axis=0), s0, 1) for at, rt, s0 in zip(at_, rt_, s0_)]
    u_st_ = [_mm("nn", t, jnp.concatenate([ars[:c], ars[:c]], axis=0) + akv, 1)
             for t, ars, akv in zip(t_inv_, ars_, akv_)]
    y_st_ = [jnp.concatenate([ars[c:], ars[c:]], axis=0) + _mm("nn", a_r, jnp.concatenate([u_st, v2], axis=0), 1)
             for ars, a_r, u_st, v2 in zip(ars_, a_r_, u_st_, v2_)]
    upd_ = [_mm("tn", jnp.concatenate([unstack(u_st), v], axis=0), jnp.concatenate([bh, kh], axis=0), 1)
            for u_st, v, bh, kh in zip(u_st_, v_, bh_, kh_)]
    for p in pairs:
        s_ref[p] = s0_[p] * pe_[p] + jnp.where(same_head, upd_[p], 0.0)

    inv_n = 1.0 / HEAD_DIM
    for p, sl in enumerate(sls):
        y = unstack(y_st_[p])
        dev = y - head_sum(y) * inv_n
        var = head_sum(dev * dev) * inv_n
        bonus = head_sum(r_[p] * k_[p] * rk_ref[:, sl]) * v_[p]
        y_ref[:, sl] = dev * lax.rsqrt(var + GN_EPS) * lg_ref[:, sl] + lb_ref[:, sl] + bonus


def rwkv_recurrence(r, k, v, lw, a, k_k, k_a, r_k, lnx_g, lnx_b, batch):
    m, d = r.shape
    n_chunks = m // batch // CHUNK
    tile = pl.BlockSpec((CHUNK, d), lambda b, c: (b * n_chunks + c, 0))
    vec = pl.BlockSpec((1, d), lambda b, c: (0, 0))
    vecs = [t.reshape(1, d) for t in (k_k, k_a, r_k, lnx_g, lnx_b)]
    return pl.pallas_call(
        _rwkv_rec_kernel,
        grid=(batch, n_chunks),
        in_specs=[tile] * 5 + [vec] * 5,
        out_specs=tile,
        out_shape=jax.ShapeDtypeStruct((m, d), F32),
        scratch_shapes=[pltpu.VMEM((d // LANES, LANES, LANES), F32)],
        compiler_params=_params(("parallel", "arbitrary")),
        name="rwkv_recurrence",
    )(r, k, v, lw, a, *vecs)


def _out_proj_kernel(has_gate, seq_rows, tm, *refs):
    if has_gate:
        y_ref, g_ref, h_ref, w_ref, o_ref = refs
        y = (y_ref[...] * g_ref[...]).astype(BF16)
    else:
        y_ref, h_ref, w_ref, o_ref = refs
        y = y_ref[...].astype(BF16)
    new_h = h_ref[...] + _dot(y, w_ref[...])
    keep = _row_in_seq(tm, pl.program_id(0), seq_rows) >= PAD
    o_ref[...] = jnp.where(keep, new_h, 0.0)


def out_proj(y, gate, h, w, seq_rows, tm):
    m, d = h.shape
    tile = pl.BlockSpec((tm, d), lambda i: (i, 0))
    args = [y] + ([gate] if gate is not None else []) + [h, w.astype(BF16)]
    in_specs = [tile] * (len(args) - 1) + [pl.BlockSpec(w.shape, lambda i: (0, 0))]
    return pl.pallas_call(
        functools.partial(_out_proj_kernel, gate is not None, seq_rows, tm),
        grid=(m // tm,),
        in_specs=in_specs, out_specs=tile,
        out_shape=jax.ShapeDtypeStruct((m, d), F32),
        compiler_params=_params(("parallel",)),
        name="out_proj",
    )(*args)


def _ffn_kernel(seq_rows, tm, h_ref, halo_ref, gain_ref, wg_ref, wv_ref, cwg_ref, cwv_ref,
                cbg_ref, cbv_ref, wd_ref, o_ref, xn_ref, acc_ref):
    i, j = pl.program_id(0), pl.program_id(1)

    @pl.when(j == 0)
    def _():
        gain = gain_ref[...]
        xn_ref[0:CONV_HALO, :] = (_rms_hat(halo_ref[...]) * gain).astype(BF16)
        xn_ref[CONV_HALO:, :] = (_rms_hat(h_ref[...]) * gain).astype(BF16)
        acc_ref[...] = jnp.zeros_like(acc_ref)

    xn = xn_ref[...]

    def conv(u, cw_ref, cb_ref):
        out = cb_ref[...] + u[CONV_HALO:, :] * cw_ref[CONV_WIDTH - 1:CONV_WIDTH, :]
        for tap in range(CONV_WIDTH - 1):
            back = CONV_WIDTH - 1 - tap
            out = out + u[CONV_HALO - back:CONV_HALO - back + tm, :] * cw_ref[tap:tap + 1, :]
        return out

    gate = conv(_dot(xn, wg_ref[...]), cwg_ref, cbg_ref)
    val = conv(_dot(xn, wv_ref[...]), cwv_ref, cbv_ref)
    act = (jax.nn.silu(gate) * val).astype(BF16)
    acc_ref[...] += _dot(act, wd_ref[...])

    @pl.when(j == pl.num_programs(1) - 1)
    def _():
        keep = _row_in_seq(tm, i, seq_rows) >= PAD
        o_ref[...] = jnp.where(keep, h_ref[...] + acc_ref[...], 0.0)


def conv_ffn(h, gain, w_up, conv_w, conv_b, w_down, seq_rows, tm, tf):
    m, d = h.shape
    d_ff = w_down.shape[0]
    nf = d_ff // tf
    cw = jnp.pad(conv_w, ((0, 8 - CONV_WIDTH), (0, 0)))
    cb = conv_b.reshape(1, 2 * d_ff)
    w_up = w_up.astype(BF16)
    tile = pl.BlockSpec((tm, d), lambda i, j: (i, 0))
    halo = pl.BlockSpec((CONV_HALO, d), lambda i, j: (jnp.maximum(i * (tm // CONV_HALO) - 1, 0), 0))
    return pl.pallas_call(
        functools.partial(_ffn_kernel, seq_rows, tm),
        grid=(m // tm, nf),
        in_specs=[
            tile, halo, pl.BlockSpec((1, d), lambda i, j: (0, 0)),
            pl.BlockSpec((d, tf), lambda i, j: (0, j)),
            pl.BlockSpec((d, tf), lambda i, j: (0, j + nf)),
            pl.BlockSpec((8, tf), lambda i, j: (0, j)),
            pl.BlockSpec((8, tf), lambda i, j: (0, j + nf)),
            pl.BlockSpec((1, tf), lambda i, j: (0, j)),
            pl.BlockSpec((1, tf), lambda i, j: (0, j + nf)),
            pl.BlockSpec((tf, d), lambda i, j: (j, 0)),
        ],
        out_specs=tile,
        out_shape=jax.ShapeDtypeStruct((m, d), F32),
        scratch_shapes=[pltpu.VMEM((tm + CONV_HALO, d), BF16), pltpu.VMEM((tm, d), F32)],
        compiler_params=_params(("parallel", "arbitrary")),
        name="conv_ffn",
    )(h, h, gain.reshape(1, d), w_up, w_up, cw, cw, cb, cb, w_down.astype(BF16))


def _sb_attn_kernel(q_ref, k_ref, v_ref, o_ref):
    qi = pl.program_id(2)
    n_groups = q_ref.shape[1] // LANES
    scale = HEAD_DIM ** -0.5
    head0 = lax.broadcasted_iota(jnp.int32, (BLOCK, LANES), 1) < HEAD_DIM
    q_heads = []
    for g in range(n_groups):
        q = q_ref[:, g * LANES:(g + 1) * LANES]
        q_heads += [jnp.where(head0, q, jnp.zeros_like(q)), jnp.where(head0, jnp.zeros_like(q), q)]
    q_pos = qi * BLOCK + lax.broadcasted_iota(jnp.int32, (BLOCK, BLOCK), 0)
    k_off = lax.broadcasted_iota(jnp.int32, (BLOCK, BLOCK), 1)
    row_b = lax.broadcasted_iota(jnp.int32, (2 * BLOCK, 2 * BLOCK), 0) % BLOCK
    col_b = lax.broadcasted_iota(jnp.int32, (2 * BLOCK, 2 * BLOCK), 1)
    sum_mat = jnp.where((row_b > col_b) | (col_b >= BLOCK), 1.0, 0.0).astype(BF16)

    def cond(state):
        j, live, _, _ = state
        return (j >= 0) & (live > 0)

    def body(state):
        j, _, carries, accs = state
        start = pl.multiple_of(j * BLOCK, BLOCK)
        k_pos = j * BLOCK + k_off
        valid = (k_pos < q_pos) & (k_pos >= PAD)
        kjs = [k_ref[pl.ds(start, BLOCK), g * LANES:(g + 1) * LANES] for g in range(n_groups)]
        vjs = [v_ref[pl.ds(start, BLOCK), g * LANES:(g + 1) * LANES] for g in range(n_groups)]
        zs = [_dot_nt(qh, kjs[h // 2]) * scale for h, qh in enumerate(q_heads)]
        sps = [jnp.maximum(z, 0.0) + jnp.log1p(jnp.exp(-jnp.abs(z))) for z in zs]
        log_1ms = [jnp.where(valid, -sp, 0.0) for sp in sps]
        sums = [_dot(jnp.concatenate(_split_bf16(l), axis=1), sum_mat) for l in log_1ms]
        ws = [jnp.where(valid, jnp.exp((z - sp) + sm[:, :BLOCK] + c), 0.0)
              for z, sp, sm, c in zip(zs, sps, sums, carries)]
        pvs = [_dot(w.astype(BF16), vjs[h // 2]) for h, w in enumerate(ws)]
        carries = tuple(c + sm[:, BLOCK:] for c, sm in zip(carries, sums))
        accs = tuple(acc + jnp.where(head0, pvs[2 * g], pvs[2 * g + 1]) for g, acc in enumerate(accs))
        top = carries[0]
        for c in carries[1:]:
            top = jnp.maximum(top, c)
        live = (jnp.max(top) > LOG_W_FLOOR).astype(jnp.int32)
        return j - 1, live, carries, accs

    zeros = jnp.zeros((BLOCK, LANES), F32)
    state = lax.while_loop(cond, body, (qi, jnp.int32(1), (zeros,) * (2 * n_groups), (zeros,) * n_groups))
    for g, acc in enumerate(state[-1]):
        o_ref[:, g * LANES:(g + 1) * LANES] = acc


def sb_attention(q, k, v, batch, groups):
    m, d = q.shape
    seq_rows = m // batch
    nq = seq_rows // BLOCK
    width = groups * LANES
    return pl.pallas_call(
        _sb_attn_kernel,
        grid=(batch, d // width, nq),
        in_specs=[
            pl.BlockSpec((BLOCK, width), lambda b, p, i: (b * nq + i, p)),
            pl.BlockSpec((seq_rows, width), lambda b, p, i: (b, p)),
            pl.BlockSpec((seq_rows, width), lambda b, p, i: (b, p)),
        ],
        out_specs=pl.BlockSpec((BLOCK, width), lambda b, p, i: (b * nq + i, p)),
        out_shape=jax.ShapeDtypeStruct((m, d), F32),
        compiler_params=_params(("parallel", "parallel", "arbitrary")),
        name="sb_attention",
    )(q, k, v)


def _row_tile(m, want):
    tm = want
    while m % tm:
        tm //= 2
    return tm


def kernel(x, meta_tokens, norm_mix_g, norm_ffn_g, ffn_up, ffn_conv_w, ffn_conv_b, ffn_down, rw_mix, rw_wr, rw_wk, rw_wv, rw_wo, rw_w0, rw_w1, rw_w2, rw_a0, rw_a1, rw_a2, rw_g1, rw_g2, rw_kk, rw_ka, rw_rk, rw_lnx_g, rw_lnx_b, rw_v0, rw_v1, rw_v2, kv_norm_g, sb_wk, sb_wv, sb_wq, sb_wo, final_norm_g):
    bsz, seq, d = x.shape
    depth = norm_mix_g.shape[0]
    n_a = rw_wr.shape[0]
    d_ff = ffn_down.shape[1]
    seq_rows = PAD + N_META + seq
    assert seq % BLOCK == 0 and d % LANES == 0 and seq_rows % CHUNK == 0
    m = bsz * seq_rows
    tm_big = _row_tile(m, 512)
    tm_proj = _row_tile(m, 256)
    tf = 256 if d_ff % 256 == 0 else LANES

    meta = jnp.broadcast_to(meta_tokens.astype(x.dtype)[None], (bsz, N_META, d))
    h = jnp.concatenate([jnp.zeros((bsz, PAD, d), x.dtype), meta, x], axis=1).reshape(m, d)

    v_first = None
    k_sh = v_sh = None
    for layer in range(depth):
        if layer < n_a:
            i = layer
            vres = (None, None, None) if i == 0 else (rw_v0[i - 1], rw_v1[i - 1], rw_v2[i - 1])
            r, k, v, lw, a, g = rwkv_proj(
                h, norm_mix_g[layer], rw_mix[i], rw_w0[i], rw_a0[i], vres[0], rw_wr[i], rw_wk[i], rw_wv[i],
                rw_w1[i], rw_w2[i], rw_a1[i], rw_a2[i], rw_g1[i], rw_g2[i], vres[1], vres[2],
                None if i == 0 else v_first, seq_rows, tm_proj)
            if i == 0:
                v_first = v
            y = rwkv_recurrence(r, k, v, lw, a, rw_kk[i], rw_ka[i], rw_rk[i].reshape(-1),
                                rw_lnx_g[i], rw_lnx_b[i], bsz)
            h = out_proj(y, g, h, rw_wo[i], seq_rows, tm_big)
        else:
            j = layer - n_a
            if layer == n_a:
                q, k_sh, v_sh = norm_proj(
                    h, [norm_mix_g[layer], kv_norm_g, kv_norm_g],
                    [sb_wq[j].astype(BF16), sb_wk.astype(BF16), sb_wv.astype(BF16)], BF16, tm_big)
            else:
                (q,) = norm_proj(h, [norm_mix_g[layer]], [sb_wq[j].astype(BF16)], BF16, tm_big)
            o = sb_attention(q, k_sh, v_sh, bsz, ATTN_GROUPS if (d // LANES) % ATTN_GROUPS == 0 else 1)
            h = out_proj(o, None, h, sb_wo[j], seq_rows, tm_big)
        h = conv_ffn(h, norm_ffn_g[layer], ffn_up[layer], ffn_conv_w[layer], ffn_conv_b[layer],
                     ffn_down[layer], seq_rows, tm_big, tf)
    out = final_norm(h, final_norm_g, tm_big)
    return out.reshape(bsz, seq_rows, d)[:, PAD + N_META:, :]
```

```python
import functools
import math

import jax
import jax.numpy as jnp
from jax import lax
from jax.experimental import pallas as pl
from jax.experimental.pallas import tpu as pltpu

HEAD_DIM = 64
N_META = 16
BLOCK = 128
PAD = BLOCK - N_META
CONV_WIDTH = 3
GN_EPS = 64e-5
RMS_EPS = 1e-6

LANES = 128
CHUNK = 64
CONV_HALO = 16
SHIFT_HALO = 8
ATTN_GROUPS = 4
LOG_W_FLOOR = -100.0
VMEM_LIMIT = 56 * 1024 * 1024

F32 = jnp.float32
BF16 = jnp.bfloat16
HI = lax.Precision.HIGHEST


def _dot(a, b, precision=None):
    return jnp.dot(a, b, preferred_element_type=F32, precision=precision)


def _dot_nt(a, b, precision=None):
    return lax.dot_general(a, b, (((1,), (1,)), ((), ())), preferred_element_type=F32, precision=precision)


def _dot_tn(a, b, precision=None):
    return lax.dot_general(a, b, (((0,), (0,)), ((), ())), preferred_element_type=F32, precision=precision)


def _rms_hat(x):
    return x * lax.rsqrt(jnp.mean(x * x, axis=-1, keepdims=True) + RMS_EPS)


def _params(sem):
    return pltpu.CompilerParams(dimension_semantics=sem, vmem_limit_bytes=VMEM_LIMIT)


def _row_in_seq(tile_rows, tile_index, seq_rows):
    rows = tile_index * tile_rows + lax.broadcasted_iota(jnp.int32, (tile_rows, 1), 0)
    return rows % seq_rows


def _norm_proj_kernel(n_groups, x_ref, *refs):
    g_refs = refs[:n_groups]
    w_refs = refs[n_groups:2 * n_groups]
    o_refs = refs[2 * n_groups:]
    xhat = _rms_hat(x_ref[...])
    for g_ref, w_ref, o_ref in zip(g_refs, w_refs, o_refs):
        xn = (xhat * g_ref[...]).astype(BF16)
        o_ref[...] = _dot(xn, w_ref[...]).astype(o_ref.dtype)


def norm_proj(h, gains, weights, out_dtype, tm):
    m, d = h.shape
    n = len(gains)
    in_specs = [pl.BlockSpec((tm, d), lambda i: (i, 0))]
    in_specs += [pl.BlockSpec((1, d), lambda i: (0, 0)) for _ in gains]
    in_specs += [pl.BlockSpec(w.shape, lambda i: (0, 0)) for w in weights]
    out_specs = [pl.BlockSpec((tm, w.shape[1]), lambda i: (i, 0)) for w in weights]
    out_shape = [jax.ShapeDtypeStruct((m, w.shape[1]), out_dtype) for w in weights]
    return pl.pallas_call(
        functools.partial(_norm_proj_kernel, n),
        grid=(m // tm,),
        in_specs=in_specs, out_specs=out_specs, out_shape=out_shape,
        compiler_params=_params(("parallel",)),
        name="norm_proj",
    )(h, *[g.reshape(1, d) for g in gains], *weights)


def _final_norm_kernel(x_ref, g_ref, o_ref):
    o_ref[...] = _rms_hat(x_ref[...]) * g_ref[...]


def final_norm(h, gain, tm):
    m, d = h.shape
    return pl.pallas_call(
        _final_norm_kernel,
        grid=(m // tm,),
        in_specs=[pl.BlockSpec((tm, d), lambda i: (i, 0)), pl.BlockSpec((1, d), lambda i: (0, 0))],
        out_specs=pl.BlockSpec((tm, d), lambda i: (i, 0)),
        out_shape=jax.ShapeDtypeStruct((m, d), F32),
        compiler_params=_params(("parallel",)),
        name="final_norm",
    )(h, gain.reshape(1, d))


def _rwkv_proj_kernel(has_vres, seq_rows, tm, *refs):
    if has_vres:
        (h_ref, halo_ref, gain_ref, mix_ref, vec_ref, wr_ref, wk_ref, wv_ref, w1_ref, w2_ref,
         a1_ref, a2_ref, g1_ref, g2_ref, v1_ref, v2_ref, vfirst_ref,
         r_ref, k_ref, v_ref, lw_ref, a_ref, g_ref) = refs
    else:
        (h_ref, halo_ref, gain_ref, mix_ref, vec_ref, wr_ref, wk_ref, wv_ref, w1_ref, w2_ref,
         a1_ref, a2_ref, g1_ref, g2_ref,
         r_ref, k_ref, v_ref, lw_ref, a_ref, g_ref) = refs
    i = pl.program_id(0)
    gain = gain_ref[...]
    hn = _rms_hat(h_ref[...]) * gain
    prev_last = (_rms_hat(halo_ref[...]) * gain)[SHIFT_HALO - 1:SHIFT_HALO, :]
    row = lax.broadcasted_iota(jnp.int32, (tm, 1), 0)
    shifted = jnp.where(row == 0, prev_last, pltpu.roll(hn, 1, 0))
    shifted = jnp.where(_row_in_seq(tm, i, seq_rows) == 0, 0.0, shifted)
    xx = shifted - hn

    def mixed(j):
        return (hn + xx * mix_ref[j:j + 1, :]).astype(BF16)

    w0, a0, v0 = vec_ref[0:1, :], vec_ref[1:2, :], vec_ref[2:3, :]
    r_ref[...] = _dot(mixed(0), wr_ref[...])
    t = jnp.tanh(_dot(mixed(1), w1_ref[...])).astype(BF16)
    w_in = w0 + _dot(t, w2_ref[...])
    lw_ref[...] = -jnp.exp(-jax.nn.softplus(-w_in) - 0.5)
    k_ref[...] = _dot(mixed(2), wk_ref[...])
    xv = mixed(3)
    v = _dot(xv, wv_ref[...])
    if has_vres:
        gate = jax.nn.sigmoid(v0 + _dot(_dot(xv, v1_ref[...]).astype(BF16), v2_ref[...]))
        v = v + (vfirst_ref[...] - v) * gate
    v_ref[...] = v
    a_ref[...] = jax.nn.sigmoid(a0 + _dot(_dot(mixed(4), a1_ref[...]).astype(BF16), a2_ref[...]))
    g_ref[...] = _dot(jax.nn.sigmoid(_dot(mixed(5), g1_ref[...])).astype(BF16), g2_ref[...])


def _pad_lora(w_in, w_out):
    hid = w_in.shape[1]
    hid_p = -(-hid // LANES) * LANES
    return (jnp.pad(w_in, ((0, 0), (0, hid_p - hid))).astype(BF16),
            jnp.pad(w_out, ((0, hid_p - hid), (0, 0))).astype(BF16))


def rwkv_proj(h, gain, mix, w0, a0, v0, wr, wk, wv, w1, w2, a1, a2, g1, g2, v1, v2, v_first,
              seq_rows, tm):
    m, d = h.shape
    has_vres = v_first is not None
    mix_p = jnp.pad(mix, ((0, 8 - mix.shape[0]), (0, 0)))
    vecs = jnp.stack([w0, a0, v0 if has_vres else jnp.zeros_like(w0)])
    vecs = jnp.pad(vecs, ((0, 8 - vecs.shape[0]), (0, 0)))
    w1p, w2p = _pad_lora(w1, w2)
    a1p, a2p = _pad_lora(a1, a2)
    g1p, g2p = _pad_lora(g1, g2)
    full = lambda a: pl.BlockSpec(a.shape, lambda i: (0, 0))
    tile = pl.BlockSpec((tm, d), lambda i: (i, 0))
    halo = pl.BlockSpec((SHIFT_HALO, d), lambda i: (jnp.maximum(i * (tm // SHIFT_HALO) - 1, 0), 0))
    args = [h, h, gain.reshape(1, d), mix_p, vecs, wr.astype(BF16), wk.astype(BF16), wv.astype(BF16),
            w1p, w2p, a1p, a2p, g1p, g2p]
    in_specs = [tile, halo] + [full(a) for a in args[2:]]
    if has_vres:
        v1p, v2p = _pad_lora(v1, v2)
        args += [v1p, v2p, v_first]
        in_specs += [full(v1p), full(v2p), tile]
    return pl.pallas_call(
        functools.partial(_rwkv_proj_kernel, has_vres, seq_rows, tm),
        grid=(m // tm,),
        in_specs=in_specs,
        out_specs=[tile] * 6,
        out_shape=[jax.ShapeDtypeStruct((m, d), F32)] * 6,
        compiler_params=_params(("parallel",)),
        name="rwkv_proj",
    )(*args)


def _split_bf16(x):
    hi = x.astype(BF16)
    return hi, (x - hi.astype(F32)).astype(BF16)


def _mm(kind, a, b, passes):
    fn = {"nn": _dot, "nt": _dot_nt, "tn": _dot_tn}[kind]
    if passes == 1:
        return fn(a.astype(BF16), b.astype(BF16))
    a_hi, a_lo = _split_bf16(a)
    b_hi, b_lo = _split_bf16(b)
    a_axis = 0 if kind == "tn" else 1
    b_axis = 1 if kind == "nt" else 0
    return fn(jnp.concatenate([a_hi, a_lo, a_hi], axis=a_axis), jnp.concatenate([b_hi, b_hi, b_lo], axis=b_axis))


def _rwkv_rec_kernel(r_ref, k_ref, v_ref, lw_ref, a_ref, kk_ref, ka_ref, rk_ref, lg_ref, lb_ref,
                     y_ref, s_ref):
    c = CHUNK
    n_pairs = r_ref.shape[1] // LANES

    @pl.when(pl.program_id(1) == 0)
    def _():
        s_ref[...] = jnp.zeros_like(s_ref)

    head0 = lax.broadcasted_iota(jnp.int32, (c, LANES), 1) < HEAD_DIM
    row2 = lax.broadcasted_iota(jnp.int32, (2 * c, 2 * c), 0)
    col2 = lax.broadcasted_iota(jnp.int32, (2 * c, 2 * c), 1)
    same_rows = (row2 // c) == (col2 // c)
    strict_bd = same_rows & (row2 % c > col2 % c)
    incl_bd = same_rows & (row2 % c >= col2 % c)
    eye2 = (row2 == col2).astype(F32)
    row_l = lax.broadcasted_iota(jnp.int32, (LANES, LANES), 0)
    col_l = lax.broadcasted_iota(jnp.int32, (LANES, LANES), 1)
    same_head = (row_l // HEAD_DIM) == (col_l // HEAD_DIM)
    ones_bd2 = jnp.concatenate([same_head, same_head], axis=0).astype(BF16)
    row_t = lax.broadcasted_iota(jnp.int32, (c, 2 * c), 0)
    col_t = lax.broadcasted_iota(jnp.int32, (c, 2 * c), 1)
    tri2 = (row_t >= col_t % c).astype(BF16)

    def head_sum(x):
        return _dot(jnp.concatenate(_split_bf16(x), axis=1), ones_bd2)

    def stack(x):
        return jnp.concatenate([jnp.where(head0, x, 0.0), jnp.where(head0, 0.0, x)], axis=0)

    def unstack(x):
        return jnp.where(head0, x[:c], x[c:])

    cum_all = _dot(tri2, jnp.concatenate(_split_bf16(lw_ref[...]), axis=0))

    pairs = range(n_pairs)
    sls = [slice(p * LANES, (p + 1) * LANES) for p in pairs]
    r_, v_, k_, at_, rt_, bh_, kh_, pe_, aa_ = [], [], [], [], [], [], [], [], []
    for sl in sls:
        r, k_raw, v, lw, a_sig = r_ref[:, sl], k_ref[:, sl], v_ref[:, sl], lw_ref[:, sl], a_ref[:, sl]
        cum = cum_all[:, sl]
        kk = k_raw * kk_ref[:, sl]
        kk = kk * lax.rsqrt(jnp.maximum(head_sum(kk * kk), 1e-24))
        k = k_raw * (1.0 + (a_sig - 1.0) * ka_ref[:, sl])
        b_vec = kk * a_sig
        cum_last = cum[c - 1:c, :]
        inv_p = jnp.exp(-cum)
        to_end = jnp.exp(cum_last - cum)
        at = -kk * jnp.exp(cum - lw)
        rt = r * jnp.exp(cum)
        aa_.append(_mm("nt", jnp.concatenate([stack(at), stack(rt)], axis=0),
                       jnp.concatenate([stack(b_vec * inv_p), stack(k * inv_p)], axis=0), 1))
        r_.append(r); v_.append(v); k_.append(k); at_.append(at); rt_.append(rt)
        bh_.append(b_vec * to_end); kh_.append(k * to_end); pe_.append(jnp.exp(cum_last))

    a_ab_ = [jnp.where(strict_bd, aa[:2 * c, :2 * c], 0.0) for aa in aa_]
    a_ak_ = [jnp.where(strict_bd, aa[:2 * c, 2 * c:], 0.0) for aa in aa_]
    a_r_ = [jnp.concatenate([jnp.where(incl_bd, aa[2 * c:, :2 * c], 0.0),
                             jnp.where(incl_bd, aa[2 * c:, 2 * c:], 0.0)], axis=1) for aa in aa_]

    t_inv_ = [eye2 + a for a in a_ab_]
    pw_ = a_ab_
    for _ in range(c.bit_length() - 2):
        pw_ = [_mm("nn", pw, pw, 1) for pw in pw_]
        t_inv_ = [t + _mm("nn", t, pw, 1) for t, pw in zip(t_inv_, pw_)]

    v2_ = [jnp.concatenate([v, v], axis=0) for v in v_]
    akv_ = [_mm("nn", a_ak, v2, 1) for a_ak, v2 in zip(a_ak_, v2_)]
    s0_ = [s_ref[p] for p in pairs]
    ars_ = [_mm("nt", jnp.concatenate([at, rt], axis=0), s0, 1) for at, rt, s0 in zip(at_, rt_, s0_)]
    u_st_ = [_mm("nn", t, jnp.concatenate([ars[:c], ars[:c]], axis=0) + akv, 1)
             for t, ars, akv in zip(t_inv_, ars_, akv_)]
    y_st_ = [jnp.concatenate([ars[c:], ars[c:]], axis=0) + _mm("nn", a_r, jnp.concatenate([u_st, v2], axis=0), 1)
             for ars, a_r, u_st, v2 in zip(ars_, a_r_, u_st_, v2_)]
    upd_ = [_mm("tn", jnp.concatenate([unstack(u_st), v], axis=0), jnp.concatenate([bh, kh], axis=0), 1)
            for u_st, v, bh, kh in zip(u_st_, v_, bh_, kh_)]
    for p in pairs:
        s_ref[p] = s0_[p] * pe_[p] + jnp.where(same_head, upd_[p], 0.0)

    inv_n = 1.0 / HEAD_DIM
    for p, sl in enumerate(sls):
        y = unstack(y_st_[p])
        dev = y - head_sum(y) * inv_n
        var = head_sum(dev * dev) * inv_n
        bonus = head_sum(r_[p] * k_[p] * rk_ref[:, sl]) * v_[p]
        y_ref[:, sl] = dev * lax.rsqrt(var + GN_EPS) * lg_ref[:, sl] + lb_ref[:, sl] + bonus


def rwkv_recurrence(r, k, v, lw, a, k_k, k_a, r_k, lnx_g, lnx_b, batch):
    m, d = r.shape
    n_chunks = m // batch // CHUNK
    tile = pl.BlockSpec((CHUNK, d), lambda b, c: (b * n_chunks + c, 0))
    vec = pl.BlockSpec((1, d), lambda b, c: (0, 0))
    vecs = [t.reshape(1, d) for t in (k_k, k_a, r_k, lnx_g, lnx_b)]
    return pl.pallas_call(
        _rwkv_rec_kernel,
        grid=(batch, n_chunks),
        in_specs=[tile] * 5 + [vec] * 5,
        out_specs=tile,
        out_shape=jax.ShapeDtypeStruct((m, d), F32),
        scratch_shapes=[pltpu.VMEM((d // LANES, LANES, LANES), F32)],
        compiler_params=_params(("parallel", "arbitrary")),
        name="rwkv_recurrence",
    )(r, k, v, lw, a, *vecs)


def _out_proj_kernel(has_gate, seq_rows, tm, *refs):
    if has_gate:
        y_ref, g_ref, h_ref, w_ref, o_ref = refs
        y = (y_ref[...] * g_ref[...]).astype(BF16)
    else:
        y_ref, h_ref, w_ref, o_ref = refs
        y = y_ref[...].astype(BF16)
    new_h = h_ref[...] + _dot(y, w_ref[...])
    keep = _row_in_seq(tm, pl.program_id(0), seq_rows) >= PAD
    o_ref[...] = jnp.where(keep, new_h, 0.0)


def out_proj(y, gate, h, w, seq_rows, tm):
    m, d = h.shape
    tile = pl.BlockSpec((tm, d), lambda i: (i, 0))
    args = [y] + ([gate] if gate is not None else []) + [h, w.astype(BF16)]
    in_specs = [tile] * (len(args) - 1) + [pl.BlockSpec(w.shape, lambda i: (0, 0))]
    return pl.pallas_call(
        functools.partial(_out_proj_kernel, gate is not None, seq_rows, tm),
        grid=(m // tm,),
        in_specs=in_specs, out_specs=tile,
        out_shape=jax.ShapeDtypeStruct((m, d), F32),
        compiler_params=_params(("parallel",)),
        name="out_proj",
    )(*args)


def _ffn_kernel(seq_rows, tm, tf, h_ref, halo_ref, gain_ref, wup_ref, cw_ref, cb_ref, wd_ref, o_ref,
                xn_ref, act_ref):
    d_ff = wd_ref.shape[0]
    gain = gain_ref[...]
    xn_ref[0:CONV_HALO, :] = (_rms_hat(halo_ref[...]) * gain).astype(BF16)
    xn_ref[CONV_HALO:, :] = (_rms_hat(h_ref[...]) * gain).astype(BF16)
    xn = xn_ref[...]

    def conv(col):
        u = _dot(xn, wup_ref[:, col:col + tf])
        out = cb_ref[:, col:col + tf] + u[CONV_HALO:, :] * cw_ref[CONV_WIDTH - 1:CONV_WIDTH, col:col + tf]
        for tap in range(CONV_WIDTH - 1):
            back = CONV_WIDTH - 1 - tap
            out = out + pltpu.roll(u, back, 0)[CONV_HALO:, :] * cw_ref[tap:tap + 1, col:col + tf]
        return out

    for col in range(0, d_ff, tf):
        act_ref[:, col:col + tf] = (jax.nn.silu(conv(col)) * conv(d_ff + col)).astype(BF16)

    keep = _row_in_seq(tm, pl.program_id(0), seq_rows) >= PAD
    o_ref[...] = jnp.where(keep, h_ref[...] + _dot(act_ref[...], wd_ref[...]), 0.0)


def conv_ffn(h, gain, w_up, conv_w, conv_b, w_down, seq_rows, tm, tf):
    m, d = h.shape
    d_ff = w_down.shape[0]
    cw = jnp.pad(conv_w, ((0, 8 - CONV_WIDTH), (0, 0)))
    cb = conv_b.reshape(1, 2 * d_ff)
    tile = pl.BlockSpec((tm, d), lambda i: (i, 0))
    halo = pl.BlockSpec((CONV_HALO, d), lambda i: (jnp.maximum(i * (tm // CONV_HALO) - 1, 0), 0))
    const = lambda shape: pl.BlockSpec(shape, lambda i: (0, 0), pipeline_mode=pl.Buffered(1))
    return pl.pallas_call(
        functools.partial(_ffn_kernel, seq_rows, tm, tf),
        grid=(m // tm,),
        in_specs=[tile, halo, const((1, d)), const((d, 2 * d_ff)), const((8, 2 * d_ff)), const((1, 2 * d_ff)),
                  const((d_ff, d))],
        out_specs=tile,
        out_shape=jax.ShapeDtypeStruct((m, d), F32),
        scratch_shapes=[pltpu.VMEM((tm + CONV_HALO, d), BF16), pltpu.VMEM((tm, d_ff), BF16)],
        compiler_params=_params(("parallel",)),
        name="conv_ffn",
    )(h, h, gain.reshape(1, d), w_up.astype(BF16), cw, cb, w_down.astype(BF16))


def _sb_attn_kernel(q_ref, k_ref, v_ref, o_ref):
    qi = pl.program_id(2)
    n_groups = q_ref.shape[1] // LANES
    scale = HEAD_DIM ** -0.5
    assert math.frexp(scale)[0] == 0.5, "the score scale is folded into bf16 q, which needs a power of two"
    head0 = lax.broadcasted_iota(jnp.int32, (BLOCK, LANES), 1) < HEAD_DIM
    q_heads = []
    for g in range(n_groups):
        q = q_ref[:, g * LANES:(g + 1) * LANES] * jnp.asarray(scale, BF16)
        q_heads += [jnp.where(head0, q, jnp.zeros_like(q)), jnp.where(head0, jnp.zeros_like(q), q)]
    q_pos = qi * BLOCK + lax.broadcasted_iota(jnp.int32, (BLOCK, BLOCK), 0)
    k_off = lax.broadcasted_iota(jnp.int32, (BLOCK, BLOCK), 1)
    row_b = lax.broadcasted_iota(jnp.int32, (2 * BLOCK, 2 * BLOCK), 0) % BLOCK
    col_b = lax.broadcasted_iota(jnp.int32, (2 * BLOCK, 2 * BLOCK), 1)
    sum_mat = jnp.where((row_b > col_b) | (col_b >= BLOCK), 1.0, 0.0).astype(BF16)

    def cond(state):
        j, live, _, _ = state
        return (j >= 0) & (live > 0)

    def body(state):
        j, _, carries, accs = state
        start = pl.multiple_of(j * BLOCK, BLOCK)
        k_pos = j * BLOCK + k_off
        valid = (k_pos < q_pos) & (k_pos >= PAD)
        kjs = [k_ref[pl.ds(start, BLOCK), g * LANES:(g + 1) * LANES] for g in range(n_groups)]
        vjs = [v_ref[pl.ds(start, BLOCK), g * LANES:(g + 1) * LANES] for g in range(n_groups)]
        zs = [_dot_nt(qh, kjs[h // 2]) for h, qh in enumerate(q_heads)]
        sps = [jnp.maximum(z, 0.0) + jnp.log(1.0 + jnp.exp(-jnp.abs(z))) for z in zs]
        log_1ms = [jnp.where(valid, -sp, 0.0) for sp in sps]
        sums = [_dot(jnp.concatenate(_split_bf16(l), axis=1), sum_mat) for l in log_1ms]
        ws = [jnp.where(valid, jnp.exp((z - sp) + sm[:, :BLOCK] + c), 0.0)
              for z, sp, sm, c in zip(zs, sps, sums, carries)]
        pvs = [_dot(w.astype(BF16), vjs[h // 2]) for h, w in enumerate(ws)]
        carries = tuple(c + sm[:, BLOCK:] for c, sm in zip(carries, sums))
        accs = tuple(acc + jnp.where(head0, pvs[2 * g], pvs[2 * g + 1]) for g, acc in enumerate(accs))
        top = carries[0]
        for c in carries[1:]:
            top = jnp.maximum(top, c)
        live = (jnp.max(top) > LOG_W_FLOOR).astype(jnp.int32)
        return j - 1, live, carries, accs

    zeros = jnp.zeros((BLOCK, LANES), F32)
    state = lax.while_loop(cond, body, (qi, jnp.int32(1), (zeros,) * (2 * n_groups), (zeros,) * n_groups))
    for g, acc in enumerate(state[-1]):
        o_ref[:, g * LANES:(g + 1) * LANES] = acc


def sb_attention(q, k, v, batch, groups):
    m, d = q.shape
    seq_rows = m // batch
    nq = seq_rows // BLOCK
    width = groups * LANES
    return pl.pallas_call(
        _sb_attn_kernel,
        grid=(batch, d // width, nq),
        in_specs=[
            pl.BlockSpec((BLOCK, width), lambda b, p, i: (b * nq + i, p)),
            pl.BlockSpec((seq_rows, width), lambda b, p, i: (b, p)),
            pl.BlockSpec((seq_rows, width), lambda b, p, i: (b, p)),
        ],
        out_specs=pl.BlockSpec((BLOCK, width), lambda b, p, i: (b * nq + i, p)),
        out_shape=jax.ShapeDtypeStruct((m, d), F32),
        compiler_params=_params(("parallel", "parallel", "arbitrary")),
        name="sb_attention",
    )(q, k, v)


def _row_tile(m, want):
    tm = want
    while m % tm:
        tm //= 2
    return tm


def kernel(x, meta_tokens, norm_mix_g, norm_ffn_g, ffn_up, ffn_conv_w, ffn_conv_b, ffn_down, rw_mix, rw_wr, rw_wk, rw_wv, rw_wo, rw_w0, rw_w1, rw_w2, rw_a0, rw_a1, rw_a2, rw_g1, rw_g2, rw_kk, rw_ka, rw_rk, rw_lnx_g, rw_lnx_b, rw_v0, rw_v1, rw_v2, kv_norm_g, sb_wk, sb_wv, sb_wq, sb_wo, final_norm_g):
    bsz, seq, d = x.shape
    depth = norm_mix_g.shape[0]
    n_a = rw_wr.shape[0]
    d_ff = ffn_down.shape[1]
    seq_rows = PAD + N_META + seq
    assert seq % BLOCK == 0 and d % LANES == 0 and seq_rows % CHUNK == 0
    m = bsz * seq_rows
    tm_big = _row_tile(m, 512)
    tm_proj = _row_tile(m, 256)
    tf = 256 if d_ff % 256 == 0 else LANES

    meta = jnp.broadcast_to(meta_tokens.astype(x.dtype)[None], (bsz, N_META, d))
    h = jnp.concatenate([jnp.zeros((bsz, PAD, d), x.dtype), meta, x], axis=1).reshape(m, d)

    v_first = None
    k_sh = v_sh = None
    for layer in range(depth):
        if layer < n_a:
            i = layer
            vres = (None, None, None) if i == 0 else (rw_v0[i - 1], rw_v1[i - 1], rw_v2[i - 1])
            r, k, v, lw, a, g = rwkv_proj(
                h, norm_mix_g[layer], rw_mix[i], rw_w0[i], rw_a0[i], vres[0], rw_wr[i], rw_wk[i], rw_wv[i],
                rw_w1[i], rw_w2[i], rw_a1[i], rw_a2[i], rw_g1[i], rw_g2[i], vres[1], vres[2],
                None if i == 0 else v_first, seq_rows, tm_proj)
            if i == 0:
                v_first = v
            y = rwkv_recurrence(r, k, v, lw, a, rw_kk[i], rw_ka[i], rw_rk[i].reshape(-1),
                                rw_lnx_g[i], rw_lnx_b[i], bsz)
            h = out_proj(y, g, h, rw_wo[i], seq_rows, tm_big)
        else:
            j = layer - n_a
            if layer == n_a:
                q, k_sh, v_sh = norm_proj(
                    h, [norm_mix_g[layer], kv_norm_g, kv_norm_g],
                    [sb_wq[j].astype(BF16), sb_wk.astype(BF16), sb_wv.astype(BF16)], BF16, tm_big)
            else:
                (q,) = norm_proj(h, [norm_mix_g[layer]], [sb_wq[j].astype(BF16)], BF16, tm_big)
            o = sb_attention(q, k_sh, v_sh, bsz, ATTN_GROUPS if (d // LANES) % ATTN_GROUPS == 0 else 1)
            h = out_proj(o, None, h, sb_wo[j], seq_rows, tm_big)
        h = conv_ffn(h, norm_ffn_g[layer], ffn_up[layer], ffn_conv_w[layer], ffn_conv_b[layer],
                     ffn_down[layer], seq_rows, tm_big, tf)
    out = final_norm(h, final_norm_g, tm_big)
    return out.reshape(bsz, seq_rows, d)[:, PAD + N_META:, :]
```

```python
import functools
import math

import jax
import jax.numpy as jnp
from jax import lax
from jax.experimental import pallas as pl
from jax.experimental.pallas import tpu as pltpu

HEAD_DIM = 64
N_META = 16
BLOCK = 128
PAD = BLOCK - N_META
CONV_WIDTH = 3
GN_EPS = 64e-5
RMS_EPS = 1e-6

LANES = 128
CHUNK = 64
CONV_HALO = 16
SHIFT_HALO = 8
ATTN_GROUPS = 8
ATTN_Q = BLOCK
KEY_STEP = HEAD_DIM
ATTN_KV_BUFFERS = 1
LOG_W_FLOOR = -100.0
VMEM_LIMIT = 56 * 1024 * 1024

F32 = jnp.float32
BF16 = jnp.bfloat16


def _dot(a, b):
    return jnp.dot(a, b, preferred_element_type=F32)


def _dot_nt(a, b):
    return lax.dot_general(a, b, (((1,), (1,)), ((), ())), preferred_element_type=F32)


def _dot_tn(a, b):
    return lax.dot_general(a, b, (((0,), (0,)), ((), ())), preferred_element_type=F32)


def _split_bf16(x):
    hi = x.astype(BF16)
    return hi, (x - hi.astype(F32)).astype(BF16)


def _rms_hat(x):
    return x * lax.rsqrt(jnp.mean(x * x, axis=-1, keepdims=True) + RMS_EPS)


def _params(sem):
    return pltpu.CompilerParams(dimension_semantics=sem, vmem_limit_bytes=VMEM_LIMIT)


def _row_in_seq(tile_rows, tile_index, seq_rows):
    rows = tile_index * tile_rows + lax.broadcasted_iota(jnp.int32, (tile_rows, 1), 0)
    return rows % seq_rows


def _norm_proj_kernel(n_groups, x_ref, *refs):
    g_refs = refs[:n_groups]
    w_refs = refs[n_groups:2 * n_groups]
    o_refs = refs[2 * n_groups:]
    xhat = _rms_hat(x_ref[...])
    for g_ref, w_ref, o_ref in zip(g_refs, w_refs, o_refs):
        xn = (xhat * g_ref[...]).astype(BF16)
        o_ref[...] = _dot(xn, w_ref[...]).astype(o_ref.dtype)


def norm_proj(h, gains, weights, out_dtype, tm):
    m, d = h.shape
    n = len(gains)
    in_specs = [pl.BlockSpec((tm, d), lambda i: (i, 0))]
    in_specs += [pl.BlockSpec((1, d), lambda i: (0, 0)) for _ in gains]
    in_specs += [pl.BlockSpec(w.shape, lambda i: (0, 0)) for w in weights]
    out_specs = [pl.BlockSpec((tm, w.shape[1]), lambda i: (i, 0)) for w in weights]
    out_shape = [jax.ShapeDtypeStruct((m, w.shape[1]), out_dtype) for w in weights]
    return pl.pallas_call(
        functools.partial(_norm_proj_kernel, n),
        grid=(m // tm,),
        in_specs=in_specs, out_specs=out_specs, out_shape=out_shape,
        compiler_params=_params(("parallel",)),
        name="norm_proj",
    )(h, *[g.reshape(1, d) for g in gains], *weights)


def _final_norm_kernel(x_ref, g_ref, o_ref):
    o_ref[...] = _rms_hat(x_ref[...]) * g_ref[...]


def final_norm(h, gain, batch):
    m, d = h.shape
    blocks = m // batch // BLOCK
    return pl.pallas_call(
        _final_norm_kernel,
        grid=(batch, blocks - 1),
        in_specs=[pl.BlockSpec((BLOCK, d), lambda b, i: (b * blocks + i + 1, 0)),
                  pl.BlockSpec((1, d), lambda b, i: (0, 0))],
        out_specs=pl.BlockSpec((BLOCK, d), lambda b, i: (b * (blocks - 1) + i, 0)),
        out_shape=jax.ShapeDtypeStruct((batch * (blocks - 1) * BLOCK, d), F32),
        compiler_params=_params(("parallel", "parallel")),
        name="final_norm",
    )(h, gain.reshape(1, d))


def _rwkv_proj_kernel(has_vres, seq_rows, tm, *refs):
    if has_vres:
        (h_ref, halo_ref, gain_ref, mix_ref, vec_ref, wr_ref, wk_ref, wv_ref, w1_ref, w2_ref,
         a1_ref, a2_ref, g1_ref, g2_ref, v1_ref, v2_ref, vfirst_ref,
         r_ref, k_ref, v_ref, lw_ref, a_ref, g_ref) = refs
    else:
        (h_ref, halo_ref, gain_ref, mix_ref, vec_ref, wr_ref, wk_ref, wv_ref, w1_ref, w2_ref,
         a1_ref, a2_ref, g1_ref, g2_ref,
         r_ref, k_ref, v_ref, lw_ref, a_ref, g_ref) = refs
    i = pl.program_id(0)
    gain = gain_ref[...]
    hn = _rms_hat(h_ref[...]) * gain
    prev_last = (_rms_hat(halo_ref[...]) * gain)[SHIFT_HALO - 1:SHIFT_HALO, :]
    row = lax.broadcasted_iota(jnp.int32, (tm, 1), 0)
    shifted = jnp.where(row == 0, prev_last, pltpu.roll(hn, 1, 0))
    shifted = jnp.where(_row_in_seq(tm, i, seq_rows) == 0, 0.0, shifted)
    xx = shifted - hn

    def mixed(j):
        return (hn + xx * mix_ref[j:j + 1, :]).astype(BF16)

    w0, a0, v0 = vec_ref[0:1, :], vec_ref[1:2, :], vec_ref[2:3, :]
    r_ref[...] = _dot(mixed(0), wr_ref[...])
    t = jnp.tanh(_dot(mixed(1), w1_ref[...])).astype(BF16)
    w_in = w0 + _dot(t, w2_ref[...])
    lw_ref[...] = -jnp.exp(-jax.nn.softplus(-w_in) - 0.5)
    k_ref[...] = _dot(mixed(2), wk_ref[...])
    xv = mixed(3)
    v = _dot(xv, wv_ref[...])
    if has_vres:
        gate = jax.nn.sigmoid(v0 + _dot(_dot(xv, v1_ref[...]).astype(BF16), v2_ref[...]))
        v = v + (vfirst_ref[...] - v) * gate
    v_ref[...] = v
    a_ref[...] = jax.nn.sigmoid(a0 + _dot(_dot(mixed(4), a1_ref[...]).astype(BF16), a2_ref[...]))
    g_ref[...] = _dot(jax.nn.sigmoid(_dot(mixed(5), g1_ref[...])).astype(BF16), g2_ref[...])


def _pad_lora(w_in, w_out):
    hid = w_in.shape[1]
    hid_p = -(-hid // LANES) * LANES
    return (jnp.pad(w_in, ((0, 0), (0, hid_p - hid))).astype(BF16),
            jnp.pad(w_out, ((0, hid_p - hid), (0, 0))).astype(BF16))


def rwkv_proj(h, gain, mix, w0, a0, v0, wr, wk, wv, w1, w2, a1, a2, g1, g2, v1, v2, v_first,
              seq_rows, tm):
    m, d = h.shape
    has_vres = v_first is not None
    mix_p = jnp.pad(mix, ((0, 8 - mix.shape[0]), (0, 0)))
    vecs = jnp.stack([w0, a0, v0 if has_vres else jnp.zeros_like(w0)])
    vecs = jnp.pad(vecs, ((0, 8 - vecs.shape[0]), (0, 0)))
    w1p, w2p = _pad_lora(w1, w2)
    a1p, a2p = _pad_lora(a1, a2)
    g1p, g2p = _pad_lora(g1, g2)
    full = lambda a: pl.BlockSpec(a.shape, lambda i: (0, 0))
    tile = pl.BlockSpec((tm, d), lambda i: (i, 0))
    halo = pl.BlockSpec((SHIFT_HALO, d), lambda i: (jnp.maximum(i * (tm // SHIFT_HALO) - 1, 0), 0))
    args = [h, h, gain.reshape(1, d), mix_p, vecs, wr.astype(BF16), wk.astype(BF16), wv.astype(BF16),
            w1p, w2p, a1p, a2p, g1p, g2p]
    in_specs = [tile, halo] + [full(a) for a in args[2:]]
    if has_vres:
        v1p, v2p = _pad_lora(v1, v2)
        args += [v1p, v2p, v_first]
        in_specs += [full(v1p), full(v2p), tile]
    return pl.pallas_call(
        functools.partial(_rwkv_proj_kernel, has_vres, seq_rows, tm),
        grid=(m // tm,),
        in_specs=in_specs,
        out_specs=[tile] * 6,
        out_shape=[jax.ShapeDtypeStruct((m, d), F32)] * 6,
        compiler_params=_params(("parallel",)),
        name="rwkv_proj",
    )(*args)


def _mm(kind, a, b):
    fn = {"nn": _dot, "nt": _dot_nt, "tn": _dot_tn}[kind]
    return fn(a.astype(BF16), b.astype(BF16))


def _rwkv_rec_kernel(r_ref, k_ref, v_ref, lw_ref, a_ref, kk_ref, ka_ref, rk_ref, lg_ref, lb_ref,
                     y_ref, s_ref):
    c = CHUNK
    n_pairs = r_ref.shape[1] // LANES

    @pl.when(pl.program_id(1) == 0)
    def _():
        s_ref[...] = jnp.zeros_like(s_ref)

    head0 = lax.broadcasted_iota(jnp.int32, (c, LANES), 1) < HEAD_DIM
    row2 = lax.broadcasted_iota(jnp.int32, (2 * c, 2 * c), 0)
    col2 = lax.broadcasted_iota(jnp.int32, (2 * c, 2 * c), 1)
    same_rows = (row2 // c) == (col2 // c)
    strict_bd = same_rows & (row2 % c > col2 % c)
    incl_bd = same_rows & (row2 % c >= col2 % c)
    eye2 = (row2 == col2).astype(F32)
    row_l = lax.broadcasted_iota(jnp.int32, (LANES, LANES), 0)
    col_l = lax.broadcasted_iota(jnp.int32, (LANES, LANES), 1)
    same_head = (row_l // HEAD_DIM) == (col_l // HEAD_DIM)
    ones_bd2 = jnp.concatenate([same_head, same_head], axis=0).astype(BF16)
    row_t = lax.broadcasted_iota(jnp.int32, (c, 2 * c), 0)
    col_t = lax.broadcasted_iota(jnp.int32, (c, 2 * c), 1)
    tri2 = (row_t >= col_t % c).astype(BF16)

    def head_sum(x):
        return _dot(jnp.concatenate(_split_bf16(x), axis=1), ones_bd2)

    def stack(x):
        return jnp.concatenate([jnp.where(head0, x, 0.0), jnp.where(head0, 0.0, x)], axis=0)

    def unstack(x):
        return jnp.where(head0, x[:c], x[c:])

    cum_all = _dot(tri2, jnp.concatenate(_split_bf16(lw_ref[...]), axis=0))

    pairs = range(n_pairs)
    sls = [slice(p * LANES, (p + 1) * LANES) for p in pairs]
    r_, v_, k_, at_, rt_, bh_, kh_, pe_, aa_ = [], [], [], [], [], [], [], [], []
    for sl in sls:
        r, k_raw, v, lw, a_sig = r_ref[:, sl], k_ref[:, sl], v_ref[:, sl], lw_ref[:, sl], a_ref[:, sl]
        cum = cum_all[:, sl]
        kk = k_raw * kk_ref[:, sl]
        kk = kk * lax.rsqrt(jnp.maximum(head_sum(kk * kk), 1e-24))
        k = k_raw * (1.0 + (a_sig - 1.0) * ka_ref[:, sl])
        b_vec = kk * a_sig
        cum_last = cum[c - 1:c, :]
        inv_p = jnp.exp(-cum)
        to_end = jnp.exp(cum_last - cum)
        at = -kk * jnp.exp(cum - lw)
        rt = r * jnp.exp(cum)
        aa_.append(_mm("nt", jnp.concatenate([stack(at), stack(rt)], axis=0),
                       jnp.concatenate([stack(b_vec * inv_p), stack(k * inv_p)], axis=0)))
        r_.append(r); v_.append(v); k_.append(k); at_.append(at); rt_.append(rt)
        bh_.append(b_vec * to_end); kh_.append(k * to_end); pe_.append(jnp.exp(cum_last))

    a_ab_ = [jnp.where(strict_bd, aa[:2 * c, :2 * c], 0.0) for aa in aa_]
    a_ak_ = [jnp.where(strict_bd, aa[:2 * c, 2 * c:], 0.0) for aa in aa_]
    a_r_ = [jnp.concatenate([jnp.where(incl_bd, aa[2 * c:, :2 * c], 0.0),
                             jnp.where(incl_bd, aa[2 * c:, 2 * c:], 0.0)], axis=1) for aa in aa_]

    t_inv_ = [eye2 + a for a in a_ab_]
    pw_ = [_mm("nn", a, a) for a in a_ab_]
    n_steps = c.bit_length() - 2
    for step in range(n_steps):
        if step < n_steps - 1:
            tp_ = [_mm("nn", jnp.concatenate([t, pw], axis=0), pw) for t, pw in zip(t_inv_, pw_)]
            t_inv_ = [t + tp[:2 * c] for t, tp in zip(t_inv_, tp_)]
            pw_ = [tp[2 * c:] for tp in tp_]
        else:
            t_inv_ = [t + _mm("nn", t, pw) for t, pw in zip(t_inv_, pw_)]

    v2_ = [jnp.concatenate([v, v], axis=0) for v in v_]
    akv_ = [_mm("nn", a_ak, v2) for a_ak, v2 in zip(a_ak_, v2_)]
    s0_ = [s_ref[p] for p in pairs]
    ars_ = [_mm("nt", jnp.concatenate([at, rt], axis=0), s0) for at, rt, s0 in zip(at_, rt_, s0_)]
    u_st_ = [_mm("nn", t, jnp.concatenate([ars[:c], ars[:c]], axis=0) + akv)
             for t, ars, akv in zip(t_inv_, ars_, akv_)]
    y_st_ = [jnp.concatenate([ars[c:], ars[c:]], axis=0) + _mm("nn", a_r, jnp.concatenate([u_st, v2], axis=0))
             for ars, a_r, u_st, v2 in zip(ars_, a_r_, u_st_, v2_)]
    upd_ = [_mm("tn", jnp.concatenate([unstack(u_st), v], axis=0), jnp.concatenate([bh, kh], axis=0))
            for u_st, v, bh, kh in zip(u_st_, v_, bh_, kh_)]
    for p in pairs:
        s_ref[p] = s0_[p] * pe_[p] + jnp.where(same_head, upd_[p], 0.0)

    inv_n = 1.0 / HEAD_DIM
    for p, sl in enumerate(sls):
        y = unstack(y_st_[p])
        dev = y - head_sum(y) * inv_n
        var = head_sum(dev * dev) * inv_n
        bonus = head_sum(r_[p] * k_[p] * rk_ref[:, sl]) * v_[p]
        y_ref[:, sl] = dev * lax.rsqrt(var + GN_EPS) * lg_ref[:, sl] + lb_ref[:, sl] + bonus


def rwkv_recurrence(r, k, v, lw, a, k_k, k_a, r_k, lnx_g, lnx_b, batch):
    m, d = r.shape
    n_chunks = m // batch // CHUNK
    tile = pl.BlockSpec((CHUNK, d), lambda b, c: (b * n_chunks + c, 0))
    vec = pl.BlockSpec((1, d), lambda b, c: (0, 0))
    vecs = [t.reshape(1, d) for t in (k_k, k_a, r_k, lnx_g, lnx_b)]
    return pl.pallas_call(
        _rwkv_rec_kernel,
        grid=(batch, n_chunks),
        in_specs=[tile] * 5 + [vec] * 5,
        out_specs=tile,
        out_shape=jax.ShapeDtypeStruct((m, d), F32),
        scratch_shapes=[pltpu.VMEM((d // LANES, LANES, LANES), F32)],
        compiler_params=_params(("parallel", "arbitrary")),
        name="rwkv_recurrence",
    )(r, k, v, lw, a, *vecs)


def _mix_ffn_kernel(has_gate, seq_rows, tm, tf, *refs):
    if has_gate:
        (y_ref, g_ref, h_ref, yh_ref, gh_ref, hh_ref, wo_ref, gain_ref, wup_ref, cw_ref, cb_ref, wd_ref,
         o_ref, hmid_ref, xn_ref, act_ref) = refs
    else:
        (y_ref, h_ref, yh_ref, hh_ref, wo_ref, gain_ref, wup_ref, cw_ref, cb_ref, wd_ref,
         o_ref, hmid_ref, xn_ref, act_ref) = refs
        g_ref = gh_ref = None
    i = pl.program_id(0)
    d_ff = wd_ref.shape[0]
    gain = gain_ref[...]

    def mixed_in(y_r, g_r, h_r, first_row):
        y = y_r[...] if g_r is None else y_r[...] * g_r[...]
        rows = h_r.shape[0]
        row = (first_row + lax.broadcasted_iota(jnp.int32, (rows, 1), 0)) % seq_rows
        return jnp.where(row >= PAD, h_r[...] + _dot(y.astype(BF16), wo_ref[...]), 0.0)

    halo = mixed_in(yh_ref, gh_ref, hh_ref, i * tm - CONV_HALO + seq_rows)
    hmid_ref[...] = mixed_in(y_ref, g_ref, h_ref, i * tm)
    xn_ref[0:CONV_HALO, :] = (_rms_hat(halo) * gain).astype(BF16)
    xn_ref[CONV_HALO:, :] = (_rms_hat(hmid_ref[...]) * gain).astype(BF16)
    xn = xn_ref[...]

    def conv(col):
        u = _dot(xn, wup_ref[:, col:col + tf])
        out = cb_ref[:, col:col + tf] + u[CONV_HALO:, :] * cw_ref[CONV_WIDTH - 1:CONV_WIDTH, col:col + tf]
        for tap in range(CONV_WIDTH - 1):
            back = CONV_WIDTH - 1 - tap
            out = out + pltpu.roll(u, back, 0)[CONV_HALO:, :] * cw_ref[tap:tap + 1, col:col + tf]
        return out

    for col in range(0, d_ff, tf):
        act_ref[:, col:col + tf] = (jax.nn.silu(conv(col)) * conv(d_ff + col)).astype(BF16)

    keep = _row_in_seq(tm, i, seq_rows) >= PAD
    o_ref[...] = jnp.where(keep, hmid_ref[...] + _dot(act_ref[...], wd_ref[...]), 0.0)


def mix_ffn(y, gate, h, w_o, gain, w_up, conv_w, conv_b, w_down, seq_rows, tm, tf):
    m, d = h.shape
    d_ff = w_down.shape[0]
    cw = jnp.pad(conv_w, ((0, 8 - CONV_WIDTH), (0, 0)))
    cb = conv_b.reshape(1, 2 * d_ff)
    tile = pl.BlockSpec((tm, d), lambda i: (i, 0))
    halo = pl.BlockSpec((CONV_HALO, d), lambda i: (jnp.maximum(i * (tm // CONV_HALO) - 1, 0), 0))
    const = lambda shape: pl.BlockSpec(shape, lambda i: (0, 0), pipeline_mode=pl.Buffered(1))
    seqs = [y] + ([gate] if gate is not None else []) + [h]
    return pl.pallas_call(
        functools.partial(_mix_ffn_kernel, gate is not None, seq_rows, tm, tf),
        grid=(m // tm,),
        in_specs=[tile] * len(seqs) + [halo] * len(seqs) + [
            const((d, d)), const((1, d)), const((d, 2 * d_ff)), const((8, 2 * d_ff)), const((1, 2 * d_ff)),
            const((d_ff, d))],
        out_specs=tile,
        out_shape=jax.ShapeDtypeStruct((m, d), F32),
        scratch_shapes=[pltpu.VMEM((tm, d), F32), pltpu.VMEM((tm + CONV_HALO, d), BF16),
                        pltpu.VMEM((tm, d_ff), BF16)],
        compiler_params=_params(("parallel",)),
        name="mix_ffn",
    )(*seqs, *seqs, w_o.astype(BF16), gain.reshape(1, d), w_up.astype(BF16), cw, cb, w_down.astype(BF16))


def _sb_attn_kernel(q_ref, k_ref, v_ref, o_ref):
    qi = pl.program_id(2)
    qb = q_ref.shape[0]
    n_groups = q_ref.shape[1] // LANES
    scale = HEAD_DIM ** -0.5
    assert math.frexp(scale)[0] == 0.5, "the score scale is folded into bf16 q, which needs a power of two"
    qs = [q_ref[:, g * LANES:(g + 1) * LANES] * jnp.asarray(scale, BF16) for g in range(n_groups)]
    head0_k = lax.broadcasted_iota(jnp.int32, (KEY_STEP, LANES), 1) < HEAD_DIM
    q_pos = qi * qb + lax.broadcasted_iota(jnp.int32, (qb, LANES), 0)
    k_off = lax.broadcasted_iota(jnp.int32, (qb, LANES), 1) % KEY_STEP
    row_b = lax.broadcasted_iota(jnp.int32, (2 * LANES, 2 * LANES), 0) % LANES
    col_b = lax.broadcasted_iota(jnp.int32, (2 * LANES, 2 * LANES), 1)
    same_head = (row_b // KEY_STEP) == ((col_b % LANES) // KEY_STEP)
    sum_mat = jnp.where(same_head & ((row_b > col_b) | (col_b >= LANES)), 1.0, 0.0).astype(BF16)

    def stack(x):
        zero = jnp.zeros_like(x)
        return jnp.concatenate([jnp.where(head0_k, x, zero), jnp.where(head0_k, zero, x)], axis=0)

    def cond(state):
        j, live, _, _ = state
        return (j >= PAD // KEY_STEP) & (live > 0)

    def body(state):
        j, _, carries, accs = state
        start = pl.multiple_of(j * KEY_STEP, KEY_STEP)
        k_pos = j * KEY_STEP + k_off
        valid = (k_pos < q_pos) & (k_pos >= PAD)
        kss = [stack(k_ref[pl.ds(start, KEY_STEP), g * LANES:(g + 1) * LANES]) for g in range(n_groups)]
        vss = [stack(v_ref[pl.ds(start, KEY_STEP), g * LANES:(g + 1) * LANES]) for g in range(n_groups)]
        zs = [_dot_nt(q, ks) for q, ks in zip(qs, kss)]
        sps = [jnp.maximum(z, 0.0) + jnp.log(1.0 + jnp.exp(-jnp.abs(z))) for z in zs]
        log_1ms = [jnp.where(valid, -sp, 0.0) for sp in sps]
        sums = [_dot(jnp.concatenate(_split_bf16(l), axis=1), sum_mat) for l in log_1ms]
        ws = [jnp.where(valid, jnp.exp((z - sp) + sm[:, :LANES] + c), 0.0)
              for z, sp, sm, c in zip(zs, sps, sums, carries)]
        accs = tuple(acc + _dot(w.astype(BF16), vs) for acc, w, vs in zip(accs, ws, vss))
        carries = tuple(c + sm[:, LANES:] for c, sm in zip(carries, sums))
        top = carries[0]
        for c in carries[1:]:
            top = jnp.maximum(top, c)
        live = (jnp.max(top) > LOG_W_FLOOR).astype(jnp.int32)
        return j - 1, live, carries, accs

    zeros = (jnp.zeros((qb, LANES), F32),) * n_groups
    first = (qi * qb + qb - 1) // KEY_STEP
    state = lax.while_loop(cond, body, (first, jnp.int32(1), zeros, zeros))
    for g, acc in enumerate(state[-1]):
        o_ref[:, g * LANES:(g + 1) * LANES] = acc.astype(o_ref.dtype)


def sb_attention(q, k, v, batch, groups):
    m, d = q.shape
    seq_rows = m // batch
    nq = seq_rows // ATTN_Q
    width = groups * LANES
    whole_seq = pl.BlockSpec((seq_rows, width), lambda b, p, i: (b, p), pipeline_mode=pl.Buffered(ATTN_KV_BUFFERS))
    return pl.pallas_call(
        _sb_attn_kernel,
        grid=(batch, d // width, nq),
        in_specs=[pl.BlockSpec((ATTN_Q, width), lambda b, p, i: (b * nq + i, p)), whole_seq, whole_seq],
        out_specs=pl.BlockSpec((ATTN_Q, width), lambda b, p, i: (b * nq + i, p)),
        out_shape=jax.ShapeDtypeStruct((m, d), BF16),
        compiler_params=_params(("parallel", "parallel", "arbitrary")),
        name="sb_attention",
    )(q, k, v)


def _row_tile(m, want):
    tm = want
    while m % tm:
        tm //= 2
    return tm


def kernel(x, meta_tokens, norm_mix_g, norm_ffn_g, ffn_up, ffn_conv_w, ffn_conv_b, ffn_down, rw_mix, rw_wr, rw_wk, rw_wv, rw_wo, rw_w0, rw_w1, rw_w2, rw_a0, rw_a1, rw_a2, rw_g1, rw_g2, rw_kk, rw_ka, rw_rk, rw_lnx_g, rw_lnx_b, rw_v0, rw_v1, rw_v2, kv_norm_g, sb_wk, sb_wv, sb_wq, sb_wo, final_norm_g):
    bsz, seq, d = x.shape
    depth = norm_mix_g.shape[0]
    n_a = rw_wr.shape[0]
    d_ff = ffn_down.shape[1]
    seq_rows = PAD + N_META + seq
    assert seq % BLOCK == 0 and d % LANES == 0 and seq_rows % CHUNK == 0
    m = bsz * seq_rows
    tm_big = _row_tile(m, 512)
    tm_proj = _row_tile(m, 256)
    tf = 256 if d_ff % 256 == 0 else LANES

    meta = jnp.broadcast_to(meta_tokens.astype(x.dtype)[None], (bsz, N_META, d))
    h = jnp.concatenate([jnp.zeros((bsz, PAD, d), x.dtype), meta, x], axis=1).reshape(m, d)

    v_first = None
    k_sh = v_sh = None
    for layer in range(depth):
        if layer < n_a:
            i = layer
            vres = (None, None, None) if i == 0 else (rw_v0[i - 1], rw_v1[i - 1], rw_v2[i - 1])
            r, k, v, lw, a, g = rwkv_proj(
                h, norm_mix_g[layer], rw_mix[i], rw_w0[i], rw_a0[i], vres[0], rw_wr[i], rw_wk[i], rw_wv[i],
                rw_w1[i], rw_w2[i], rw_a1[i], rw_a2[i], rw_g1[i], rw_g2[i], vres[1], vres[2],
                None if i == 0 else v_first, seq_rows, tm_proj)
            if i == 0:
                v_first = v
            y = rwkv_recurrence(r, k, v, lw, a, rw_kk[i], rw_ka[i], rw_rk[i].reshape(-1),
                                rw_lnx_g[i], rw_lnx_b[i], bsz)
            mixer = (y, g, rw_wo[i])
        else:
            j = layer - n_a
            if layer == n_a:
                q, k_sh, v_sh = norm_proj(
                    h, [norm_mix_g[layer], kv_norm_g, kv_norm_g],
                    [sb_wq[j].astype(BF16), sb_wk.astype(BF16), sb_wv.astype(BF16)], BF16, tm_big)
            else:
                (q,) = norm_proj(h, [norm_mix_g[layer]], [sb_wq[j].astype(BF16)], BF16, tm_big)
            o = sb_attention(q, k_sh, v_sh, bsz, ATTN_GROUPS if (d // LANES) % ATTN_GROUPS == 0 else 1)
            mixer = (o, None, sb_wo[j])
        h = mix_ffn(mixer[0], mixer[1], h, mixer[2], norm_ffn_g[layer], ffn_up[layer], ffn_conv_w[layer],
                    ffn_conv_b[layer], ffn_down[layer], seq_rows, tm_big, tf)
    return final_norm(h, final_norm_g, bsz).reshape(bsz, seq, d)
```

```python
import functools
import math

import jax
import jax.numpy as jnp
from jax import lax
from jax.experimental import pallas as pl
from jax.experimental.pallas import tpu as pltpu

HEAD_DIM = 64
N_META = 16
BLOCK = 128
PAD = BLOCK - N_META
CONV_WIDTH = 3
GN_EPS = 64e-5
RMS_EPS = 1e-6

LANES = 128
CHUNK = 64
CONV_HALO = 16
SHIFT_HALO = 8
REC_SEQS = 2
ATTN_GROUPS = 8
ATTN_Q = BLOCK
KEY_STEP = HEAD_DIM
ATTN_KV_BUFFERS = 1
LOG_W_FLOOR = -100.0
VMEM_LIMIT = 56 * 1024 * 1024

F32 = jnp.float32
BF16 = jnp.bfloat16


def _dot(a, b):
    return jnp.dot(a, b, preferred_element_type=F32)


def _dot_nt(a, b):
    return lax.dot_general(a, b, (((1,), (1,)), ((), ())), preferred_element_type=F32)


def _dot_tn(a, b):
    return lax.dot_general(a, b, (((0,), (0,)), ((), ())), preferred_element_type=F32)


def _split_bf16(x):
    hi = x.astype(BF16)
    return hi, (x - hi.astype(F32)).astype(BF16)


def _rms_hat(x):
    return x * lax.rsqrt(jnp.mean(x * x, axis=-1, keepdims=True) + RMS_EPS)


def _params(sem):
    return pltpu.CompilerParams(dimension_semantics=sem, vmem_limit_bytes=VMEM_LIMIT)


def _row_in_seq(tile_rows, tile_index, seq_rows):
    rows = tile_index * tile_rows + lax.broadcasted_iota(jnp.int32, (tile_rows, 1), 0)
    return rows % seq_rows


def _norm_proj_kernel(n_groups, x_ref, *refs):
    g_refs = refs[:n_groups]
    w_refs = refs[n_groups:2 * n_groups]
    o_refs = refs[2 * n_groups:]
    xhat = _rms_hat(x_ref[...])
    for g_ref, w_ref, o_ref in zip(g_refs, w_refs, o_refs):
        xn = (xhat * g_ref[...]).astype(BF16)
        o_ref[...] = _dot(xn, w_ref[...]).astype(o_ref.dtype)


def norm_proj(h, gains, weights, out_dtype, tm):
    m, d = h.shape
    n = len(gains)
    in_specs = [pl.BlockSpec((tm, d), lambda i: (i, 0))]
    in_specs += [pl.BlockSpec((1, d), lambda i: (0, 0)) for _ in gains]
    in_specs += [pl.BlockSpec(w.shape, lambda i: (0, 0)) for w in weights]
    out_specs = [pl.BlockSpec((tm, w.shape[1]), lambda i: (i, 0)) for w in weights]
    out_shape = [jax.ShapeDtypeStruct((m, w.shape[1]), out_dtype) for w in weights]
    return pl.pallas_call(
        functools.partial(_norm_proj_kernel, n),
        grid=(m // tm,),
        in_specs=in_specs, out_specs=out_specs, out_shape=out_shape,
        compiler_params=_params(("parallel",)),
        name="norm_proj",
    )(h, *[g.reshape(1, d) for g in gains], *weights)


def _rwkv_proj_kernel(has_vres, seq_rows, tm, *refs):
    if has_vres:
        (h_ref, halo_ref, gain_ref, mix_ref, vec_ref, wr_ref, wk_ref, wv_ref, w1_ref, w2_ref,
         a1_ref, a2_ref, g1_ref, g2_ref, v1_ref, v2_ref, vfirst_ref,
         r_ref, k_ref, v_ref, lw_ref, a_ref, g_ref) = refs
    else:
        (h_ref, halo_ref, gain_ref, mix_ref, vec_ref, wr_ref, wk_ref, wv_ref, w1_ref, w2_ref,
         a1_ref, a2_ref, g1_ref, g2_ref,
         r_ref, k_ref, v_ref, lw_ref, a_ref, g_ref) = refs
    i = pl.program_id(0)
    gain = gain_ref[...]
    hn = _rms_hat(h_ref[...]) * gain
    prev_last = (_rms_hat(halo_ref[...]) * gain)[SHIFT_HALO - 1:SHIFT_HALO, :]
    row = lax.broadcasted_iota(jnp.int32, (tm, 1), 0)
    shifted = jnp.where(row == 0, prev_last, pltpu.roll(hn, 1, 0))
    shifted = jnp.where(_row_in_seq(tm, i, seq_rows) == 0, 0.0, shifted)
    xx = shifted - hn

    def mixed(j):
        return (hn + xx * mix_ref[j:j + 1, :]).astype(BF16)

    w0, a0, v0 = vec_ref[0:1, :], vec_ref[1:2, :], vec_ref[2:3, :]
    r_ref[...] = _dot(mixed(0), wr_ref[...])
    t = jnp.tanh(_dot(mixed(1), w1_ref[...])).astype(BF16)
    w_in = w0 + _dot(t, w2_ref[...])
    lw_ref[...] = -jnp.exp(-jax.nn.softplus(-w_in) - 0.5)
    k_ref[...] = _dot(mixed(2), wk_ref[...])
    xv = mixed(3)
    v = _dot(xv, wv_ref[...])
    if has_vres:
        gate = jax.nn.sigmoid(v0 + _dot(_dot(xv, v1_ref[...]).astype(BF16), v2_ref[...]))
        v = v + (vfirst_ref[...] - v) * gate
    v_ref[...] = v
    a_ref[...] = jax.nn.sigmoid(a0 + _dot(_dot(mixed(4), a1_ref[...]).astype(BF16), a2_ref[...]))
    g_ref[...] = _dot(jax.nn.sigmoid(_dot(mixed(5), g1_ref[...])).astype(BF16), g2_ref[...])


def _pad_lora(w_in, w_out):
    hid = w_in.shape[1]
    hid_p = -(-hid // LANES) * LANES
    return (jnp.pad(w_in, ((0, 0), (0, hid_p - hid))).astype(BF16),
            jnp.pad(w_out, ((0, hid_p - hid), (0, 0))).astype(BF16))


def rwkv_proj(h, gain, mix, w0, a0, v0, wr, wk, wv, w1, w2, a1, a2, g1, g2, v1, v2, v_first,
              seq_rows, tm):
    m, d = h.shape
    has_vres = v_first is not None
    mix_p = jnp.pad(mix, ((0, 8 - mix.shape[0]), (0, 0)))
    vecs = jnp.stack([w0, a0, v0 if has_vres else jnp.zeros_like(w0)])
    vecs = jnp.pad(vecs, ((0, 8 - vecs.shape[0]), (0, 0)))
    w1p, w2p = _pad_lora(w1, w2)
    a1p, a2p = _pad_lora(a1, a2)
    g1p, g2p = _pad_lora(g1, g2)
    full = lambda a: pl.BlockSpec(a.shape, lambda i: (0, 0), pipeline_mode=pl.Buffered(1))
    tile = pl.BlockSpec((tm, d), lambda i: (i, 0))
    halo = pl.BlockSpec((SHIFT_HALO, d), lambda i: (jnp.maximum(i * (tm // SHIFT_HALO) - 1, 0), 0))
    args = [h, h, gain.reshape(1, d), mix_p, vecs, wr.astype(BF16), wk.astype(BF16), wv.astype(BF16),
            w1p, w2p, a1p, a2p, g1p, g2p]
    in_specs = [tile, halo] + [full(a) for a in args[2:]]
    if has_vres:
        v1p, v2p = _pad_lora(v1, v2)
        args += [v1p, v2p, v_first]
        in_specs += [full(v1p), full(v2p), tile]
    return pl.pallas_call(
        functools.partial(_rwkv_proj_kernel, has_vres, seq_rows, tm),
        grid=(m // tm,),
        in_specs=in_specs,
        out_specs=[tile] * 6,
        out_shape=[jax.ShapeDtypeStruct((m, d), F32)] * 6,
        compiler_params=_params(("parallel",)),
        name="rwkv_proj",
    )(*args)


def _mm(kind, a, b):
    fn = {"nn": _dot, "nt": _dot_nt, "tn": _dot_tn}[kind]
    return fn(a.astype(BF16), b.astype(BF16))


def _rwkv_rec_kernel(r_ref, k_ref, v_ref, lw_ref, a_ref, kk_ref, ka_ref, rk_ref, lg_ref, lb_ref,
                     y_ref, s_ref):
    c = CHUNK
    n_seq = r_ref.shape[0]
    n_pairs = r_ref.shape[2] // LANES

    @pl.when(pl.program_id(1) == 0)
    def _():
        s_ref[...] = jnp.zeros_like(s_ref)

    head0 = lax.broadcasted_iota(jnp.int32, (c, LANES), 1) < HEAD_DIM
    row2 = lax.broadcasted_iota(jnp.int32, (2 * c, 2 * c), 0)
    col2 = lax.broadcasted_iota(jnp.int32, (2 * c, 2 * c), 1)
    same_rows = (row2 // c) == (col2 // c)
    strict_bd = same_rows & (row2 % c > col2 % c)
    incl_bd = same_rows & (row2 % c >= col2 % c)
    eye2 = (row2 == col2).astype(F32)
    row_l = lax.broadcasted_iota(jnp.int32, (LANES, LANES), 0)
    col_l = lax.broadcasted_iota(jnp.int32, (LANES, LANES), 1)
    same_head = (row_l // HEAD_DIM) == (col_l // HEAD_DIM)
    ones_bd2 = jnp.concatenate([same_head, same_head], axis=0).astype(BF16)
    row_t = lax.broadcasted_iota(jnp.int32, (c, 2 * c), 0)
    col_t = lax.broadcasted_iota(jnp.int32, (c, 2 * c), 1)
    tri2 = (row_t >= col_t % c).astype(BF16)

    def head_sums(xs):
        x = jnp.concatenate(xs, axis=0)
        out = _dot(jnp.concatenate(_split_bf16(x), axis=1), ones_bd2)
        return [out[i * c:(i + 1) * c] for i in range(len(xs))]

    def stack(x):
        return jnp.concatenate([jnp.where(head0, x, 0.0), jnp.where(head0, 0.0, x)], axis=0)

    def unstack(x):
        return jnp.where(head0, x[:c], x[c:])

    cum_all = [_dot(tri2, jnp.concatenate(_split_bf16(lw_ref[q]), axis=0)) for q in range(n_seq)]

    pairs = range(n_seq * n_pairs)
    seqs = [p // n_pairs for p in pairs]
    sls = [slice((p % n_pairs) * LANES, (p % n_pairs + 1) * LANES) for p in pairs]
    ins_ = [(r_ref[q, :, sl], k_ref[q, :, sl], v_ref[q, :, sl], lw_ref[q, :, sl], a_ref[q, :, sl])
            for q, sl in zip(seqs, sls)]
    kku_ = [k_raw * kk_ref[:, sl] for (_, k_raw, _, _, _), sl in zip(ins_, sls)]
    kk_sq_ = head_sums([kk * kk for kk in kku_])
    r_, v_, k_, at_, rt_, bh_, kh_, pe_, aa_ = [], [], [], [], [], [], [], [], []
    for (r, k_raw, v, lw, a_sig), q, sl, kk, kk_sq in zip(ins_, seqs, sls, kku_, kk_sq_):
        cum = cum_all[q][:, sl]
        kk = kk * lax.rsqrt(jnp.maximum(kk_sq, 1e-24))
        k = k_raw * (1.0 + (a_sig - 1.0) * ka_ref[:, sl])
        b_vec = kk * a_sig
        cum_last = cum[c - 1:c, :]
        inv_p = jnp.exp(-cum)
        to_end = jnp.exp(cum_last - cum)
        at = -kk * jnp.exp(cum - lw)
        rt = r * jnp.exp(cum)
        aa_.append(_mm("nt", jnp.concatenate([stack(at), stack(rt)], axis=0),
                       jnp.concatenate([stack(b_vec * inv_p), stack(k * inv_p)], axis=0)))
        r_.append(r); v_.append(v); k_.append(k); at_.append(at); rt_.append(rt)
        bh_.append(b_vec * to_end); kh_.append(k * to_end); pe_.append(jnp.exp(cum_last))

    a_ab_ = [jnp.where(strict_bd, aa[:2 * c, :2 * c], 0.0) for aa in aa_]
    a_ak_ = [jnp.where(strict_bd, aa[:2 * c, 2 * c:], 0.0) for aa in aa_]
    a_r_ = [jnp.concatenate([jnp.where(incl_bd, aa[2 * c:, :2 * c], 0.0),
                             jnp.where(incl_bd, aa[2 * c:, 2 * c:], 0.0)], axis=1) for aa in aa_]

    t_inv_ = [eye2 + a for a in a_ab_]
    pw_ = [_mm("nn", a, a) for a in a_ab_]
    n_steps = c.bit_length() - 2
    for step in range(n_steps):
        if step < n_steps - 1:
            tp_ = [_mm("nn", jnp.concatenate([t, pw], axis=0), pw) for t, pw in zip(t_inv_, pw_)]
            t_inv_ = [t + tp[:2 * c] for t, tp in zip(t_inv_, tp_)]
            pw_ = [tp[2 * c:] for tp in tp_]
        else:
            t_inv_ = [t + _mm("nn", t, pw) for t, pw in zip(t_inv_, pw_)]

    v2_ = [jnp.concatenate([v, v], axis=0) for v in v_]
    akv_ = [_mm("nn", a_ak, v2) for a_ak, v2 in zip(a_ak_, v2_)]
    s0_ = [s_ref[p] for p in pairs]
    ars_ = [_mm("nt", jnp.concatenate([at, rt], axis=0), s0) for at, rt, s0 in zip(at_, rt_, s0_)]
    u_st_ = [_mm("nn", t, jnp.concatenate([ars[:c], ars[:c]], axis=0) + akv)
             for t, ars, akv in zip(t_inv_, ars_, akv_)]
    y_st_ = [jnp.concatenate([ars[c:], ars[c:]], axis=0) + _mm("nn", a_r, jnp.concatenate([u_st, v2], axis=0))
             for ars, a_r, u_st, v2 in zip(ars_, a_r_, u_st_, v2_)]
    upd_ = [_mm("tn", jnp.concatenate([unstack(u_st), v], axis=0), jnp.concatenate([bh, kh], axis=0))
            for u_st, v, bh, kh in zip(u_st_, v_, bh_, kh_)]
    for p in pairs:
        s_ref[p] = s0_[p] * pe_[p] + jnp.where(same_head, upd_[p], 0.0)

    inv_n = 1.0 / HEAD_DIM
    y_ = [unstack(y_st) for y_st in y_st_]
    dev_ = [y - mu * inv_n for y, mu in zip(y_, head_sums(y_))]
    var_ = [ss * inv_n for ss in head_sums([dev * dev for dev in dev_])]
    rk_ = head_sums([r * k * rk_ref[:, sl] for r, k, sl in zip(r_, k_, sls)])
    for p, sl in enumerate(sls):
        y_ref[seqs[p], :, sl] = (dev_[p] * lax.rsqrt(var_[p] + GN_EPS) * lg_ref[:, sl] + lb_ref[:, sl]
                                 + rk_[p] * v_[p])


def rwkv_recurrence(r, k, v, lw, a, k_k, k_a, r_k, lnx_g, lnx_b, batch):
    m, d = r.shape
    seq_rows = m // batch
    n_seq = REC_SEQS if batch % REC_SEQS == 0 else 1
    tile = pl.BlockSpec((n_seq, CHUNK, d), lambda b, c: (b, c, 0))
    vec = pl.BlockSpec((1, d), lambda b, c: (0, 0))
    vecs = [t.reshape(1, d) for t in (k_k, k_a, r_k, lnx_g, lnx_b)]
    seqs = [t.reshape(batch, seq_rows, d) for t in (r, k, v, lw, a)]
    return pl.pallas_call(
        _rwkv_rec_kernel,
        grid=(batch // n_seq, seq_rows // CHUNK),
        in_specs=[tile] * 5 + [vec] * 5,
        out_specs=tile,
        out_shape=jax.ShapeDtypeStruct((batch, seq_rows, d), F32),
        scratch_shapes=[pltpu.VMEM((n_seq * (d // LANES), LANES, LANES), F32)],
        compiler_params=_params(("parallel", "arbitrary")),
        name="rwkv_recurrence",
    )(*seqs, *vecs).reshape(m, d)


def _mix_ffn_kernel(has_gate, has_final, seq_rows, tm, tf, *refs):
    refs = list(refs)
    y_ref = refs.pop(0)
    g_ref = refs.pop(0) if has_gate else None
    h_ref = refs.pop(0)
    yh_ref = refs.pop(0)
    gh_ref = refs.pop(0) if has_gate else None
    hh_ref, wo_ref, gain_ref, wup_ref, cw_ref, cb_ref, wd_ref = refs[:7]
    fgain_ref = refs[7] if has_final else None
    o_ref, hmid_ref, xn_ref, act_ref = refs[-4:]
    i = pl.program_id(0)
    d_ff = wd_ref.shape[0]
    gain = gain_ref[...]

    def mixed_in(y_r, g_r, h_r, first_row):
        y = y_r[...] if g_r is None else y_r[...] * g_r[...]
        rows = h_r.shape[0]
        row = (first_row + lax.broadcasted_iota(jnp.int32, (rows, 1), 0)) % seq_rows
        return jnp.where(row >= PAD, h_r[...] + _dot(y.astype(BF16), wo_ref[...]), 0.0)

    halo = mixed_in(yh_ref, gh_ref, hh_ref, i * tm - CONV_HALO + seq_rows)
    hmid_ref[...] = mixed_in(y_ref, g_ref, h_ref, i * tm)
    xn_ref[0:CONV_HALO, :] = (_rms_hat(halo) * gain).astype(BF16)
    xn_ref[CONV_HALO:, :] = (_rms_hat(hmid_ref[...]) * gain).astype(BF16)
    xn = xn_ref[...]

    def conv(col):
        u = _dot(xn, wup_ref[:, col:col + tf])
        out = cb_ref[:, col:col + tf] + u[CONV_HALO:, :] * cw_ref[CONV_WIDTH - 1:CONV_WIDTH, col:col + tf]
        for tap in range(CONV_WIDTH - 1):
            back = CONV_WIDTH - 1 - tap
            out = out + pltpu.roll(u, back, 0)[CONV_HALO:, :] * cw_ref[tap:tap + 1, col:col + tf]
        return out

    for col in range(0, d_ff, tf):
        act_ref[:, col:col + tf] = (jax.nn.silu(conv(col)) * conv(d_ff + col)).astype(BF16)

    keep = _row_in_seq(tm, i, seq_rows) >= PAD
    h_out = jnp.where(keep, hmid_ref[...] + _dot(act_ref[...], wd_ref[...]), 0.0)
    o_ref[...] = h_out if fgain_ref is None else _rms_hat(h_out) * fgain_ref[...]


def mix_ffn(y, gate, h, w_o, gain, w_up, conv_w, conv_b, w_down, final_gain, seq_rows, tm, tf):
    m, d = h.shape
    d_ff = w_down.shape[0]
    cw = jnp.pad(conv_w, ((0, 8 - CONV_WIDTH), (0, 0)))
    cb = conv_b.reshape(1, 2 * d_ff)
    tile = pl.BlockSpec((tm, d), lambda i: (i, 0))
    halo = pl.BlockSpec((CONV_HALO, d), lambda i: (jnp.maximum(i * (tm // CONV_HALO) - 1, 0), 0))
    const = lambda shape: pl.BlockSpec(shape, lambda i: (0, 0), pipeline_mode=pl.Buffered(1))
    seqs = [y] + ([gate] if gate is not None else []) + [h]
    finals = [] if final_gain is None else [final_gain.reshape(1, d)]
    return pl.pallas_call(
        functools.partial(_mix_ffn_kernel, gate is not None, final_gain is not None, seq_rows, tm, tf),
        grid=(m // tm,),
        in_specs=[tile] * len(seqs) + [halo] * len(seqs) + [
            const((d, d)), const((1, d)), const((d, 2 * d_ff)), const((8, 2 * d_ff)), const((1, 2 * d_ff)),
            const((d_ff, d))] + [const((1, d))] * len(finals),
        out_specs=tile,
        out_shape=jax.ShapeDtypeStruct((m, d), F32),
        scratch_shapes=[pltpu.VMEM((tm, d), F32), pltpu.VMEM((tm + CONV_HALO, d), BF16),
                        pltpu.VMEM((tm, d_ff), BF16)],
        compiler_params=_params(("parallel",)),
        name="mix_ffn",
    )(*seqs, *seqs, w_o.astype(BF16), gain.reshape(1, d), w_up.astype(BF16), cw, cb, w_down.astype(BF16), *finals)


def _sb_attn_kernel(q_ref, k_ref, v_ref, o_ref):
    qi = pl.program_id(2)
    qb = q_ref.shape[0]
    n_groups = q_ref.shape[1] // LANES
    scale = HEAD_DIM ** -0.5
    assert math.frexp(scale)[0] == 0.5, "the score scale is folded into bf16 q, which needs a power of two"
    qs = [q_ref[:, g * LANES:(g + 1) * LANES] * jnp.asarray(scale, BF16) for g in range(n_groups)]
    head0_k = lax.broadcasted_iota(jnp.int32, (KEY_STEP, LANES), 1) < HEAD_DIM
    q_pos = qi * qb + lax.broadcasted_iota(jnp.int32, (qb, LANES), 0)
    k_off = lax.broadcasted_iota(jnp.int32, (qb, LANES), 1) % KEY_STEP
    row_b = lax.broadcasted_iota(jnp.int32, (2 * LANES, 2 * LANES), 0) % LANES
    col_b = lax.broadcasted_iota(jnp.int32, (2 * LANES, 2 * LANES), 1)
    same_head = (row_b // KEY_STEP) == ((col_b % LANES) // KEY_STEP)
    sum_mat = jnp.where(same_head & ((row_b > col_b) | (col_b >= LANES)), 1.0, 0.0).astype(BF16)

    def stack(x):
        zero = jnp.zeros_like(x)
        return jnp.concatenate([jnp.where(head0_k, x, zero), jnp.where(head0_k, zero, x)], axis=0)

    def cond(state):
        j, live, _, _ = state
        return (j >= PAD // KEY_STEP) & (live > 0)

    def body(state):
        j, _, carries, accs = state
        start = pl.multiple_of(j * KEY_STEP, KEY_STEP)
        k_pos = j * KEY_STEP + k_off
        valid = (k_pos < q_pos) & (k_pos >= PAD)
        kss = [stack(k_ref[pl.ds(start, KEY_STEP), g * LANES:(g + 1) * LANES]) for g in range(n_groups)]
        vss = [stack(v_ref[pl.ds(start, KEY_STEP), g * LANES:(g + 1) * LANES]) for g in range(n_groups)]
        zs = [_dot_nt(q, ks) for q, ks in zip(qs, kss)]
        sps = [jnp.maximum(z, 0.0) + jnp.log(1.0 + jnp.exp(-jnp.abs(z))) for z in zs]
        log_1ms = [jnp.where(valid, -sp, 0.0) for sp in sps]
        sums = [_dot(jnp.concatenate(_split_bf16(l), axis=1), sum_mat) for l in log_1ms]
        ws = [jnp.where(valid, jnp.exp((z - sp) + sm[:, :LANES] + c), 0.0)
              for z, sp, sm, c in zip(zs, sps, sums, carries)]
        accs = tuple(acc + _dot(w.astype(BF16), vs) for acc, w, vs in zip(accs, ws, vss))
        carries = tuple(c + sm[:, LANES:] for c, sm in zip(carries, sums))
        top = carries[0]
        for c in carries[1:]:
            top = jnp.maximum(top, c)
        live = (jnp.max(top) > LOG_W_FLOOR).astype(jnp.int32)
        return j - 1, live, carries, accs

    zeros = (jnp.zeros((qb, LANES), F32),) * n_groups
    first = (qi * qb + qb - 1) // KEY_STEP
    state = lax.while_loop(cond, body, (first, jnp.int32(1), zeros, zeros))
    for g, acc in enumerate(state[-1]):
        o_ref[:, g * LANES:(g + 1) * LANES] = acc.astype(o_ref.dtype)


def sb_attention(q, k, v, batch, groups):
    m, d = q.shape
    seq_rows = m // batch
    nq = seq_rows // ATTN_Q
    width = groups * LANES
    whole_seq = pl.BlockSpec((seq_rows, width), lambda b, p, i: (b, p), pipeline_mode=pl.Buffered(ATTN_KV_BUFFERS))
    return pl.pallas_call(
        _sb_attn_kernel,
        grid=(batch, d // width, nq),
        in_specs=[pl.BlockSpec((ATTN_Q, width), lambda b, p, i: (b * nq + i, p)), whole_seq, whole_seq],
        out_specs=pl.BlockSpec((ATTN_Q, width), lambda b, p, i: (b * nq + i, p)),
        out_shape=jax.ShapeDtypeStruct((m, d), BF16),
        compiler_params=_params(("parallel", "parallel", "arbitrary")),
        name="sb_attention",
    )(q, k, v)


def _row_tile(m, want):
    tm = want
    while m % tm:
        tm //= 2
    return tm


def kernel(x, meta_tokens, norm_mix_g, norm_ffn_g, ffn_up, ffn_conv_w, ffn_conv_b, ffn_down, rw_mix, rw_wr, rw_wk, rw_wv, rw_wo, rw_w0, rw_w1, rw_w2, rw_a0, rw_a1, rw_a2, rw_g1, rw_g2, rw_kk, rw_ka, rw_rk, rw_lnx_g, rw_lnx_b, rw_v0, rw_v1, rw_v2, kv_norm_g, sb_wk, sb_wv, sb_wq, sb_wo, final_norm_g):
    bsz, seq, d = x.shape
    depth = norm_mix_g.shape[0]
    n_a = rw_wr.shape[0]
    d_ff = ffn_down.shape[1]
    seq_rows = PAD + N_META + seq
    assert seq % BLOCK == 0 and d % LANES == 0 and seq_rows % CHUNK == 0
    m = bsz * seq_rows
    tm_big = _row_tile(m, 512)
    tm_proj = _row_tile(m, 512)
    tf = 256 if d_ff % 256 == 0 else LANES

    meta = jnp.broadcast_to(meta_tokens.astype(x.dtype)[None], (bsz, N_META, d))
    h = jnp.concatenate([jnp.zeros((bsz, PAD, d), x.dtype), meta, x], axis=1).reshape(m, d)

    v_first = None
    k_sh = v_sh = None
    for layer in range(depth):
        if layer < n_a:
            i = layer
            vres = (None, None, None) if i == 0 else (rw_v0[i - 1], rw_v1[i - 1], rw_v2[i - 1])
            r, k, v, lw, a, g = rwkv_proj(
                h, norm_mix_g[layer], rw_mix[i], rw_w0[i], rw_a0[i], vres[0], rw_wr[i], rw_wk[i], rw_wv[i],
                rw_w1[i], rw_w2[i], rw_a1[i], rw_a2[i], rw_g1[i], rw_g2[i], vres[1], vres[2],
                None if i == 0 else v_first, seq_rows, tm_proj)
            if i == 0:
                v_first = v
            y = rwkv_recurrence(r, k, v, lw, a, rw_kk[i], rw_ka[i], rw_rk[i].reshape(-1),
                                rw_lnx_g[i], rw_lnx_b[i], bsz)
            mixer = (y, g, rw_wo[i])
        else:
            j = layer - n_a
            if layer == n_a:
                q, k_sh, v_sh = norm_proj(
                    h, [norm_mix_g[layer], kv_norm_g, kv_norm_g],
                    [sb_wq[j].astype(BF16), sb_wk.astype(BF16), sb_wv.astype(BF16)], BF16, tm_big)
            else:
                (q,) = norm_proj(h, [norm_mix_g[layer]], [sb_wq[j].astype(BF16)], BF16, tm_big)
            o = sb_attention(q, k_sh, v_sh, bsz, ATTN_GROUPS if (d // LANES) % ATTN_GROUPS == 0 else 1)
            mixer = (o, None, sb_wo[j])
        h = mix_ffn(mixer[0], mixer[1], h, mixer[2], norm_ffn_g[layer], ffn_up[layer], ffn_conv_w[layer],
                    ffn_conv_b[layer], ffn_down[layer], final_norm_g if layer == depth - 1 else None,
                    seq_rows, tm_big, tf)
    return h.reshape(bsz, seq_rows, d)[:, PAD + N_META:, :]
```

```python
import functools
import math

import jax
import jax.numpy as jnp
from jax import lax
from jax.experimental import pallas as pl
from jax.experimental.pallas import tpu as pltpu

HEAD_DIM = 64
N_META = 16
BLOCK = 128
PAD = BLOCK - N_META
CONV_WIDTH = 3
GN_EPS = 64e-5
RMS_EPS = 1e-6

LANES = 128
CHUNK = 64
CONV_HALO = 16
SHIFT_HALO = 8
REC_SEQS = 2
ATTN_GROUPS = 8
ATTN_Q = BLOCK
KEY_STEP = HEAD_DIM
ATTN_KV_BUFFERS = 1
LOG_W_FLOOR = -100.0
VMEM_LIMIT = 56 * 1024 * 1024

F32 = jnp.float32
BF16 = jnp.bfloat16


def _dot(a, b):
    return jnp.dot(a, b, preferred_element_type=F32)


def _dot_nt(a, b):
    return lax.dot_general(a, b, (((1,), (1,)), ((), ())), preferred_element_type=F32)


def _dot_tn(a, b):
    return lax.dot_general(a, b, (((0,), (0,)), ((), ())), preferred_element_type=F32)


def _split_bf16(x):
    hi = x.astype(BF16)
    return hi, (x - hi.astype(F32)).astype(BF16)


def _rms_hat(x):
    return x * lax.rsqrt(jnp.mean(x * x, axis=-1, keepdims=True) + RMS_EPS)


def _params(sem):
    return pltpu.CompilerParams(dimension_semantics=sem, vmem_limit_bytes=VMEM_LIMIT)


def _row_in_seq(tile_rows, tile_index, seq_rows):
    rows = tile_index * tile_rows + lax.broadcasted_iota(jnp.int32, (tile_rows, 1), 0)
    return rows % seq_rows


def _norm_proj_kernel(n_groups, x_ref, *refs):
    g_refs = refs[:n_groups]
    w_refs = refs[n_groups:2 * n_groups]
    o_refs = refs[2 * n_groups:]
    xhat = _rms_hat(x_ref[...])
    for g_ref, w_ref, o_ref in zip(g_refs, w_refs, o_refs):
        xn = (xhat * g_ref[...]).astype(BF16)
        o_ref[...] = _dot(xn, w_ref[...]).astype(o_ref.dtype)


def norm_proj(h, gains, weights, out_dtype, tm):
    m, d = h.shape
    n = len(gains)
    in_specs = [pl.BlockSpec((tm, d), lambda i: (i, 0))]
    in_specs += [pl.BlockSpec((1, d), lambda i: (0, 0)) for _ in gains]
    in_specs += [pl.BlockSpec(w.shape, lambda i: (0, 0)) for w in weights]
    out_specs = [pl.BlockSpec((tm, w.shape[1]), lambda i: (i, 0)) for w in weights]
    out_shape = [jax.ShapeDtypeStruct((m, w.shape[1]), out_dtype) for w in weights]
    return pl.pallas_call(
        functools.partial(_norm_proj_kernel, n),
        grid=(m // tm,),
        in_specs=in_specs, out_specs=out_specs, out_shape=out_shape,
        compiler_params=_params(("parallel",)),
        name="norm_proj",
    )(h, *[g.reshape(1, d) for g in gains], *weights)


def _rwkv_proj_kernel(has_vres, seq_rows, tm, *refs):
    if has_vres:
        (h_ref, halo_ref, gain_ref, mix_ref, vec_ref, wr_ref, wk_ref, wv_ref, w1_ref, w2_ref,
         a1_ref, a2_ref, g1_ref, g2_ref, v1_ref, v2_ref, vfirst_ref,
         r_ref, k_ref, v_ref, lw_ref, a_ref, g_ref) = refs
    else:
        (h_ref, halo_ref, gain_ref, mix_ref, vec_ref, wr_ref, wk_ref, wv_ref, w1_ref, w2_ref,
         a1_ref, a2_ref, g1_ref, g2_ref,
         r_ref, k_ref, v_ref, lw_ref, a_ref, g_ref) = refs
    i = pl.program_id(0)
    gain = gain_ref[...]
    hn = _rms_hat(h_ref[...]) * gain
    prev_last = (_rms_hat(halo_ref[...]) * gain)[SHIFT_HALO - 1:SHIFT_HALO, :]
    row = lax.broadcasted_iota(jnp.int32, (tm, 1), 0)
    shifted = jnp.where(row == 0, prev_last, pltpu.roll(hn, 1, 0))
    shifted = jnp.where(_row_in_seq(tm, i, seq_rows) == 0, 0.0, shifted)
    xx = shifted - hn

    def mixed(j):
        return (hn + xx * mix_ref[j:j + 1, :]).astype(BF16)

    w0, a0, v0 = vec_ref[0:1, :], vec_ref[1:2, :], vec_ref[2:3, :]
    r_ref[...] = _dot(mixed(0), wr_ref[...])
    t = jnp.tanh(_dot(mixed(1), w1_ref[...])).astype(BF16)
    w_in = w0 + _dot(t, w2_ref[...])
    lw_ref[...] = -math.exp(-0.5) * jax.nn.sigmoid(w_in)
    k_ref[...] = _dot(mixed(2), wk_ref[...])
    xv = mixed(3)
    v = _dot(xv, wv_ref[...])
    if has_vres:
        gate = jax.nn.sigmoid(v0 + _dot(_dot(xv, v1_ref[...]).astype(BF16), v2_ref[...]))
        v = v + (vfirst_ref[...] - v) * gate
    v_ref[...] = v
    a_ref[...] = jax.nn.sigmoid(a0 + _dot(_dot(mixed(4), a1_ref[...]).astype(BF16), a2_ref[...]))
    g_ref[...] = _dot(jax.nn.sigmoid(_dot(mixed(5), g1_ref[...])).astype(BF16), g2_ref[...])


def _pad_lora(w_in, w_out):
    hid = w_in.shape[1]
    hid_p = -(-hid // LANES) * LANES
    return (jnp.pad(w_in, ((0, 0), (0, hid_p - hid))).astype(BF16),
            jnp.pad(w_out, ((0, hid_p - hid), (0, 0))).astype(BF16))


def rwkv_proj(h, gain, mix, w0, a0, v0, wr, wk, wv, w1, w2, a1, a2, g1, g2, v1, v2, v_first,
              seq_rows, tm):
    m, d = h.shape
    has_vres = v_first is not None
    mix_p = jnp.pad(mix, ((0, 8 - mix.shape[0]), (0, 0)))
    vecs = jnp.stack([w0, a0, v0 if has_vres else jnp.zeros_like(w0)])
    vecs = jnp.pad(vecs, ((0, 8 - vecs.shape[0]), (0, 0)))
    w1p, w2p = _pad_lora(w1, w2)
    a1p, a2p = _pad_lora(a1, a2)
    g1p, g2p = _pad_lora(g1, g2)
    full = lambda a: pl.BlockSpec(a.shape, lambda i: (0, 0), pipeline_mode=pl.Buffered(1))
    tile = pl.BlockSpec((tm, d), lambda i: (i, 0))
    halo = pl.BlockSpec((SHIFT_HALO, d), lambda i: (jnp.maximum(i * (tm // SHIFT_HALO) - 1, 0), 0))
    args = [h, h, gain.reshape(1, d), mix_p, vecs, wr.astype(BF16), wk.astype(BF16), wv.astype(BF16),
            w1p, w2p, a1p, a2p, g1p, g2p]
    in_specs = [tile, halo] + [full(a) for a in args[2:]]
    if has_vres:
        v1p, v2p = _pad_lora(v1, v2)
        args += [v1p, v2p, v_first]
        in_specs += [full(v1p), full(v2p), tile]
    return pl.pallas_call(
        functools.partial(_rwkv_proj_kernel, has_vres, seq_rows, tm),
        grid=(m // tm,),
        in_specs=in_specs,
        out_specs=[tile] * 6,
        out_shape=[jax.ShapeDtypeStruct((m, d), F32)] * 6,
        compiler_params=_params(("parallel",)),
        name="rwkv_proj",
    )(*args)


def _mm(kind, a, b):
    fn = {"nn": _dot, "nt": _dot_nt, "tn": _dot_tn}[kind]
    return fn(a.astype(BF16), b.astype(BF16))


def _rwkv_rec_kernel(r_ref, k_ref, v_ref, lw_ref, a_ref, kk_ref, ka_ref, rk_ref, lg_ref, lb_ref,
                     y_ref, s_ref):
    c = CHUNK
    n_seq = r_ref.shape[0]
    n_pairs = r_ref.shape[2] // LANES

    @pl.when(pl.program_id(1) == 0)
    def _():
        s_ref[...] = jnp.zeros_like(s_ref)

    head0 = lax.broadcasted_iota(jnp.int32, (c, LANES), 1) < HEAD_DIM
    row2 = lax.broadcasted_iota(jnp.int32, (2 * c, 2 * c), 0)
    col2 = lax.broadcasted_iota(jnp.int32, (2 * c, 2 * c), 1)
    same_rows = (row2 // c) == (col2 // c)
    strict_bd = same_rows & (row2 % c > col2 % c)
    incl_bd = same_rows & (row2 % c >= col2 % c)
    eye2 = (row2 == col2).astype(F32)
    row_l = lax.broadcasted_iota(jnp.int32, (LANES, LANES), 0)
    col_l = lax.broadcasted_iota(jnp.int32, (LANES, LANES), 1)
    same_head = (row_l // HEAD_DIM) == (col_l // HEAD_DIM)
    ones_bd2 = jnp.concatenate([same_head, same_head], axis=0).astype(BF16)
    row_t = lax.broadcasted_iota(jnp.int32, (c, 2 * c), 0)
    col_t = lax.broadcasted_iota(jnp.int32, (c, 2 * c), 1)
    tri2 = (row_t >= col_t % c).astype(BF16)

    def head_sums(xs):
        x = jnp.concatenate(xs, axis=0)
        out = _dot(jnp.concatenate(_split_bf16(x), axis=1), ones_bd2)
        return [out[i * c:(i + 1) * c] for i in range(len(xs))]

    def stack(x):
        return jnp.concatenate([jnp.where(head0, x, 0.0), jnp.where(head0, 0.0, x)], axis=0)

    def unstack(x):
        return jnp.where(head0, x[:c], x[c:])

    cum_all = [_dot(tri2, jnp.concatenate(_split_bf16(lw_ref[q]), axis=0)) for q in range(n_seq)]

    pairs = range(n_seq * n_pairs)
    seqs = [p // n_pairs for p in pairs]
    sls = [slice((p % n_pairs) * LANES, (p % n_pairs + 1) * LANES) for p in pairs]
    ins_ = [(r_ref[q, :, sl], k_ref[q, :, sl], v_ref[q, :, sl], lw_ref[q, :, sl], a_ref[q, :, sl])
            for q, sl in zip(seqs, sls)]
    kku_ = [k_raw * kk_ref[:, sl] for (_, k_raw, _, _, _), sl in zip(ins_, sls)]
    kk_sq_ = head_sums([kk * kk for kk in kku_])
    r_, v_, k_, at_, rt_, bh_, kh_, pe_, aa_ = [], [], [], [], [], [], [], [], []
    for (r, k_raw, v, lw, a_sig), q, sl, kk, kk_sq in zip(ins_, seqs, sls, kku_, kk_sq_):
        cum = cum_all[q][:, sl]
        kk = kk * lax.rsqrt(jnp.maximum(kk_sq, 1e-24))
        k = k_raw * (1.0 + (a_sig - 1.0) * ka_ref[:, sl])
        b_vec = kk * a_sig
        cum_last = cum[c - 1:c, :]
        inv_p = jnp.exp(-cum)
        to_end = jnp.exp(cum_last - cum)
        at = -kk * jnp.exp(cum - lw)
        rt = r * jnp.exp(cum)
        aa_.append(_mm("nt", jnp.concatenate([stack(at), stack(rt)], axis=0),
                       jnp.concatenate([stack(b_vec * inv_p), stack(k * inv_p)], axis=0)))
        r_.append(r); v_.append(v); k_.append(k); at_.append(at); rt_.append(rt)
        bh_.append(b_vec * to_end); kh_.append(k * to_end); pe_.append(jnp.exp(cum_last))

    a_ab_ = [jnp.where(strict_bd, aa[:2 * c, :2 * c], 0.0) for aa in aa_]
    a_ak_ = [jnp.where(strict_bd, aa[:2 * c, 2 * c:], 0.0) for aa in aa_]
    a_r_ = [jnp.concatenate([jnp.where(incl_bd, aa[2 * c:, :2 * c], 0.0),
                             jnp.where(incl_bd, aa[2 * c:, 2 * c:], 0.0)], axis=1) for aa in aa_]

    t_inv_ = [eye2 + a for a in a_ab_]
    pw_ = [_mm("nn", a, a) for a in a_ab_]
    n_steps = c.bit_length() - 2
    for step in range(n_steps):
        if step < n_steps - 1:
            tp_ = [_mm("nn", jnp.concatenate([t, pw], axis=0), pw) for t, pw in zip(t_inv_, pw_)]
            t_inv_ = [t + tp[:2 * c] for t, tp in zip(t_inv_, tp_)]
            pw_ = [tp[2 * c:] for tp in tp_]
        else:
            t_inv_ = [t + _mm("nn", t, pw) for t, pw in zip(t_inv_, pw_)]

    v2_ = [jnp.concatenate([v, v], axis=0) for v in v_]
    akv_ = [_mm("nn", a_ak, v2) for a_ak, v2 in zip(a_ak_, v2_)]
    s0_ = [s_ref[p] for p in pairs]
    ars_ = [_mm("nt", jnp.concatenate([at, rt], axis=0), s0) for at, rt, s0 in zip(at_, rt_, s0_)]
    u_st_ = [_mm("nn", t, jnp.concatenate([ars[:c], ars[:c]], axis=0) + akv)
             for t, ars, akv in zip(t_inv_, ars_, akv_)]
    y_st_ = [jnp.concatenate([ars[c:], ars[c:]], axis=0) + _mm("nn", a_r, jnp.concatenate([u_st, v2], axis=0))
             for ars, a_r, u_st, v2 in zip(ars_, a_r_, u_st_, v2_)]
    upd_ = [_mm("tn", jnp.concatenate([unstack(u_st), v], axis=0), jnp.concatenate([bh, kh], axis=0))
            for u_st, v, bh, kh in zip(u_st_, v_, bh_, kh_)]
    for p in pairs:
        s_ref[p] = s0_[p] * pe_[p] + jnp.where(same_head, upd_[p], 0.0)

    inv_n = 1.0 / HEAD_DIM
    y_ = [unstack(y_st) for y_st in y_st_]
    dev_ = [y - mu * inv_n for y, mu in zip(y_, head_sums(y_))]
    var_ = [ss * inv_n for ss in head_sums([dev * dev for dev in dev_])]
    rk_ = head_sums([r * k * rk_ref[:, sl] for r, k, sl in zip(r_, k_, sls)])
    for p, sl in enumerate(sls):
        y_ref[seqs[p], :, sl] = (dev_[p] * lax.rsqrt(var_[p] + GN_EPS) * lg_ref[:, sl] + lb_ref[:, sl]
                                 + rk_[p] * v_[p])


def rwkv_recurrence(r, k, v, lw, a, k_k, k_a, r_k, lnx_g, lnx_b, batch):
    m, d = r.shape
    seq_rows = m // batch
    n_seq = REC_SEQS if batch % REC_SEQS == 0 else 1
    tile = pl.BlockSpec((n_seq, CHUNK, d), lambda b, c: (b, c, 0))
    vec = pl.BlockSpec((1, d), lambda b, c: (0, 0))
    vecs = [t.reshape(1, d) for t in (k_k, k_a, r_k, lnx_g, lnx_b)]
    seqs = [t.reshape(batch, seq_rows, d) for t in (r, k, v, lw, a)]
    return pl.pallas_call(
        _rwkv_rec_kernel,
        grid=(batch // n_seq, seq_rows // CHUNK),
        in_specs=[tile] * 5 + [vec] * 5,
        out_specs=tile,
        out_shape=jax.ShapeDtypeStruct((batch, seq_rows, d), F32),
        scratch_shapes=[pltpu.VMEM((n_seq * (d // LANES), LANES, LANES), F32)],
        compiler_params=_params(("parallel", "arbitrary")),
        name="rwkv_recurrence",
    )(*seqs, *vecs).reshape(m, d)


def _mix_ffn_kernel(has_gate, has_final, seq_rows, tm, tf, *refs):
    refs = list(refs)
    y_ref = refs.pop(0)
    g_ref = refs.pop(0) if has_gate else None
    h_ref = refs.pop(0)
    yh_ref = refs.pop(0)
    gh_ref = refs.pop(0) if has_gate else None
    hh_ref, wo_ref, gain_ref, wup_ref, cw_ref, cb_ref, wd_ref = refs[:7]
    fgain_ref = refs[7] if has_final else None
    o_ref, hmid_ref, xn_ref, act_ref = refs[-4:]
    i = pl.program_id(0)
    d_ff = wd_ref.shape[0]
    gain = gain_ref[...]

    def mixed_in(y_r, g_r, h_r, first_row):
        y = y_r[...] if g_r is None else y_r[...] * g_r[...]
        rows = h_r.shape[0]
        row = (first_row + lax.broadcasted_iota(jnp.int32, (rows, 1), 0)) % seq_rows
        return jnp.where(row >= PAD, h_r[...] + _dot(y.astype(BF16), wo_ref[...]), 0.0)

    halo = mixed_in(yh_ref, gh_ref, hh_ref, i * tm - CONV_HALO + seq_rows)
    hmid_ref[...] = mixed_in(y_ref, g_ref, h_ref, i * tm)
    xn_ref[0:CONV_HALO, :] = (_rms_hat(halo) * gain).astype(BF16)
    xn_ref[CONV_HALO:, :] = (_rms_hat(hmid_ref[...]) * gain).astype(BF16)
    xn = xn_ref[...]

    def conv(col):
        u = _dot(xn, wup_ref[:, col:col + tf])
        out = cb_ref[:, col:col + tf] + u[CONV_HALO:, :] * cw_ref[CONV_WIDTH - 1:CONV_WIDTH, col:col + tf]
        for tap in range(CONV_WIDTH - 1):
            back = CONV_WIDTH - 1 - tap
            out = out + pltpu.roll(u, back, 0)[CONV_HALO:, :] * cw_ref[tap:tap + 1, col:col + tf]
        return out

    for col in range(0, d_ff, tf):
        act_ref[:, col:col + tf] = (jax.nn.silu(conv(col)) * conv(d_ff + col)).astype(BF16)

    keep = _row_in_seq(tm, i, seq_rows) >= PAD
    h_out = jnp.where(keep, hmid_ref[...] + _dot(act_ref[...], wd_ref[...]), 0.0)
    o_ref[...] = h_out if fgain_ref is None else _rms_hat(h_out) * fgain_ref[...]


def mix_ffn(y, gate, h, w_o, gain, w_up, conv_w, conv_b, w_down, final_gain, seq_rows, tm, tf):
    m, d = h.shape
    d_ff = w_down.shape[0]
    cw = jnp.pad(conv_w, ((0, 8 - CONV_WIDTH), (0, 0)))
    cb = conv_b.reshape(1, 2 * d_ff)
    tile = pl.BlockSpec((tm, d), lambda i: (i, 0))
    halo = pl.BlockSpec((CONV_HALO, d), lambda i: (jnp.maximum(i * (tm // CONV_HALO) - 1, 0), 0))
    const = lambda shape: pl.BlockSpec(shape, lambda i: (0, 0), pipeline_mode=pl.Buffered(1))
    seqs = [y] + ([gate] if gate is not None else []) + [h]
    finals = [] if final_gain is None else [final_gain.reshape(1, d)]
    return pl.pallas_call(
        functools.partial(_mix_ffn_kernel, gate is not None, final_gain is not None, seq_rows, tm, tf),
        grid=(m // tm,),
        in_specs=[tile] * len(seqs) + [halo] * len(seqs) + [
            const((d, d)), const((1, d)), const((d, 2 * d_ff)), const((8, 2 * d_ff)), const((1, 2 * d_ff)),
            const((d_ff, d))] + [const((1, d))] * len(finals),
        out_specs=tile,
        out_shape=jax.ShapeDtypeStruct((m, d), F32),
        scratch_shapes=[pltpu.VMEM((tm, d), F32), pltpu.VMEM((tm + CONV_HALO, d), BF16),
                        pltpu.VMEM((tm, d_ff), BF16)],
        compiler_params=_params(("parallel",)),
        name="mix_ffn",
    )(*seqs, *seqs, w_o.astype(BF16), gain.reshape(1, d), w_up.astype(BF16), cw, cb, w_down.astype(BF16), *finals)


def _sb_attn_kernel(q_ref, k_ref, v_ref, o_ref):
    qi = pl.program_id(2)
    qb = q_ref.shape[0]
    n_groups = q_ref.shape[1] // LANES
    scale = HEAD_DIM ** -0.5
    assert math.frexp(scale)[0] == 0.5, "the score scale is folded into bf16 q, which needs a power of two"
    qs = [q_ref[:, g * LANES:(g + 1) * LANES] * jnp.asarray(scale, BF16) for g in range(n_groups)]
    head0_k = lax.broadcasted_iota(jnp.int32, (KEY_STEP, LANES), 1) < HEAD_DIM
    q_pos = qi * qb + lax.broadcasted_iota(jnp.int32, (qb, LANES), 0)
    k_off = lax.broadcasted_iota(jnp.int32, (qb, LANES), 1) % KEY_STEP
    row_b = lax.broadcasted_iota(jnp.int32, (2 * LANES, 2 * LANES), 0) % LANES
    col_b = lax.broadcasted_iota(jnp.int32, (2 * LANES, 2 * LANES), 1)
    same_head = (row_b // KEY_STEP) == ((col_b % LANES) // KEY_STEP)
    sum_mat = jnp.where(same_head & ((row_b > col_b) | (col_b >= LANES)), 1.0, 0.0).astype(BF16)

    def stack(x):
        zero = jnp.zeros_like(x)
        return jnp.concatenate([jnp.where(head0_k, x, zero), jnp.where(head0_k, zero, x)], axis=0)

    def cond(state):
        j, live, _, _ = state
        return (j >= PAD // KEY_STEP) & (live > 0)

    def body(state):
        j, _, carries, accs = state
        start = pl.multiple_of(j * KEY_STEP, KEY_STEP)
        k_pos = j * KEY_STEP + k_off
        valid = (k_pos < q_pos) & (k_pos >= PAD)
        kss = [stack(k_ref[pl.ds(start, KEY_STEP), g * LANES:(g + 1) * LANES]) for g in range(n_groups)]
        vss = [stack(v_ref[pl.ds(start, KEY_STEP), g * LANES:(g + 1) * LANES]) for g in range(n_groups)]
        zs = [_dot_nt(q, ks) for q, ks in zip(qs, kss)]
        sps = [jnp.maximum(z, 0.0) + jnp.log(1.0 + jnp.exp(-jnp.abs(z))) for z in zs]
        log_1ms = [jnp.where(valid, -sp, 0.0) for sp in sps]
        sums = [_dot(jnp.concatenate(_split_bf16(l), axis=1), sum_mat) for l in log_1ms]
        ws = [jnp.where(valid, jnp.exp((z - sp) + sm[:, :LANES] + c), 0.0)
              for z, sp, sm, c in zip(zs, sps, sums, carries)]
        accs = tuple(acc + _dot(w.astype(BF16), vs) for acc, w, vs in zip(accs, ws, vss))
        carries = tuple(c + sm[:, LANES:] for c, sm in zip(carries, sums))
        top = carries[0]
        for c in carries[1:]:
            top = jnp.maximum(top, c)
        live = (jnp.max(top) > LOG_W_FLOOR).astype(jnp.int32)
        return j - 1, live, carries, accs

    zeros = (jnp.zeros((qb, LANES), F32),) * n_groups
    first = (qi * qb + qb - 1) // KEY_STEP
    state = lax.while_loop(cond, body, (first, jnp.int32(1), zeros, zeros))
    for g, acc in enumerate(state[-1]):
        o_ref[:, g * LANES:(g + 1) * LANES] = acc.astype(o_ref.dtype)


def sb_attention(q, k, v, batch, groups):
    m, d = q.shape
    seq_rows = m // batch
    nq = seq_rows // ATTN_Q
    width = groups * LANES
    whole_seq = pl.BlockSpec((seq_rows, width), lambda b, p, i: (b, p), pipeline_mode=pl.Buffered(ATTN_KV_BUFFERS))
    return pl.pallas_call(
        _sb_attn_kernel,
        grid=(batch, d // width, nq),
        in_specs=[pl.BlockSpec((ATTN_Q, width), lambda b, p, i: (b * nq + i, p)), whole_seq, whole_seq],
        out_specs=pl.BlockSpec((ATTN_Q, width), lambda b, p, i: (b * nq + i, p)),
        out_shape=jax.ShapeDtypeStruct((m, d), BF16),
        compiler_params=_params(("parallel", "parallel", "arbitrary")),
        name="sb_attention",
    )(q, k, v)


def _row_tile(m, want):
    tm = want
    while m % tm:
        tm //= 2
    return tm


def kernel(x, meta_tokens, norm_mix_g, norm_ffn_g, ffn_up, ffn_conv_w, ffn_conv_b, ffn_down, rw_mix, rw_wr, rw_wk, rw_wv, rw_wo, rw_w0, rw_w1, rw_w2, rw_a0, rw_a1, rw_a2, rw_g1, rw_g2, rw_kk, rw_ka, rw_rk, rw_lnx_g, rw_lnx_b, rw_v0, rw_v1, rw_v2, kv_norm_g, sb_wk, sb_wv, sb_wq, sb_wo, final_norm_g):
    bsz, seq, d = x.shape
    depth = norm_mix_g.shape[0]
    n_a = rw_wr.shape[0]
    d_ff = ffn_down.shape[1]
    seq_rows = PAD + N_META + seq
    assert seq % BLOCK == 0 and d % LANES == 0 and seq_rows % CHUNK == 0
    m = bsz * seq_rows
    tm_big = _row_tile(m, 512)
    tm_proj = _row_tile(m, 512)
    tf = 256 if d_ff % 256 == 0 else LANES

    meta = jnp.broadcast_to(meta_tokens.astype(x.dtype)[None], (bsz, N_META, d))
    h = jnp.concatenate([jnp.zeros((bsz, PAD, d), x.dtype), meta, x], axis=1).reshape(m, d)

    v_first = None
    k_sh = v_sh = None
    for layer in range(depth):
        if layer < n_a:
            i = layer
            vres = (None, None, None) if i == 0 else (rw_v0[i - 1], rw_v1[i - 1], rw_v2[i - 1])
            r, k, v, lw, a, g = rwkv_proj(
                h, norm_mix_g[layer], rw_mix[i], rw_w0[i], rw_a0[i], vres[0], rw_wr[i], rw_wk[i], rw_wv[i],
                rw_w1[i], rw_w2[i], rw_a1[i], rw_a2[i], rw_g1[i], rw_g2[i], vres[1], vres[2],
                None if i == 0 else v_first, seq_rows, tm_proj)
            if i == 0:
                v_first = v
            y = rwkv_recurrence(r, k, v, lw, a, rw_kk[i], rw_ka[i], rw_rk[i].reshape(-1),
                                rw_lnx_g[i], rw_lnx_b[i], bsz)
            mixer = (y, g, rw_wo[i])
        else:
            j = layer - n_a
            if layer == n_a:
                q, k_sh, v_sh = norm_proj(
                    h, [norm_mix_g[layer], kv_norm_g, kv_norm_g],
                    [sb_wq[j].astype(BF16), sb_wk.astype(BF16), sb_wv.astype(BF16)], BF16, tm_big)
            else:
                (q,) = norm_proj(h, [norm_mix_g[layer]], [sb_wq[j].astype(BF16)], BF16, tm_big)
            o = sb_attention(q, k_sh, v_sh, bsz, ATTN_GROUPS if (d // LANES) % ATTN_GROUPS == 0 else 1)
            mixer = (o, None, sb_wo[j])
        h = mix_ffn(mixer[0], mixer[1], h, mixer[2], norm_ffn_g[layer], ffn_up[layer], ffn_conv_w[layer],
                    ffn_conv_b[layer], ffn_down[layer], final_norm_g if layer == depth - 1 else None,
                    seq_rows, tm_big, tf)
    return h.reshape(bsz, seq_rows, d)[:, PAD + N_META:, :]
```

```python
import functools
import math

import jax
import jax.numpy as jnp
from jax import lax
from jax.experimental import pallas as pl
from jax.experimental.pallas import tpu as pltpu

HEAD_DIM = 64
N_META = 16
BLOCK = 128
PAD = BLOCK - N_META
CONV_WIDTH = 3
GN_EPS = 64e-5
RMS_EPS = 1e-6

LANES = 128
CHUNK = 64
CONV_HALO = 16
SHIFT_HALO = 8
REC_SEQS = 2
ATTN_GROUPS = 8
ATTN_Q = BLOCK
KEY_STEP = HEAD_DIM
ATTN_UNROLLED_STEPS = 5
ATTN_KV_BUFFERS = 1
LOG_W_FLOOR = -100.0
VMEM_LIMIT = 56 * 1024 * 1024

F32 = jnp.float32
BF16 = jnp.bfloat16


def _dot(a, b):
    return jnp.dot(a, b, preferred_element_type=F32)


def _dot_nt(a, b):
    return lax.dot_general(a, b, (((1,), (1,)), ((), ())), preferred_element_type=F32)


def _dot_tn(a, b):
    return lax.dot_general(a, b, (((0,), (0,)), ((), ())), preferred_element_type=F32)


def _split_bf16(x):
    hi = x.astype(BF16)
    return hi, (x - hi.astype(F32)).astype(BF16)


def _rms_hat(x):
    return x * lax.rsqrt(jnp.mean(x * x, axis=-1, keepdims=True) + RMS_EPS)


def _params(sem):
    return pltpu.CompilerParams(dimension_semantics=sem, vmem_limit_bytes=VMEM_LIMIT)


def _row_in_seq(tile_rows, tile_index, seq_rows):
    rows = tile_index * tile_rows + lax.broadcasted_iota(jnp.int32, (tile_rows, 1), 0)
    return rows % seq_rows


def _norm_proj_kernel(n_groups, x_ref, *refs):
    g_refs = refs[:n_groups]
    w_refs = refs[n_groups:2 * n_groups]
    o_refs = refs[2 * n_groups:]
    xhat = _rms_hat(x_ref[...])
    for g_ref, w_ref, o_ref in zip(g_refs, w_refs, o_refs):
        xn = (xhat * g_ref[...]).astype(BF16)
        o_ref[...] = _dot(xn, w_ref[...]).astype(o_ref.dtype)


def norm_proj(h, gains, weights, out_dtype, tm):
    m, d = h.shape
    n = len(gains)
    in_specs = [pl.BlockSpec((tm, d), lambda i: (i, 0))]
    in_specs += [pl.BlockSpec((1, d), lambda i: (0, 0)) for _ in gains]
    in_specs += [pl.BlockSpec(w.shape, lambda i: (0, 0)) for w in weights]
    out_specs = [pl.BlockSpec((tm, w.shape[1]), lambda i: (i, 0)) for w in weights]
    out_shape = [jax.ShapeDtypeStruct((m, w.shape[1]), out_dtype) for w in weights]
    return pl.pallas_call(
        functools.partial(_norm_proj_kernel, n),
        grid=(m // tm,),
        in_specs=in_specs, out_specs=out_specs, out_shape=out_shape,
        compiler_params=_params(("parallel",)),
        name="norm_proj",
    )(h, *[g.reshape(1, d) for g in gains], *weights)


def _rwkv_proj_kernel(has_vres, seq_rows, tm, *refs):
    if has_vres:
        (h_ref, halo_ref, gain_ref, mix_ref, vec_ref, wr_ref, wk_ref, wv_ref, w1_ref, w2_ref,
         a1_ref, a2_ref, g1_ref, g2_ref, v1_ref, v2_ref, vfirst_ref,
         r_ref, k_ref, v_ref, lw_ref, a_ref, g_ref) = refs
    else:
        (h_ref, halo_ref, gain_ref, mix_ref, vec_ref, wr_ref, wk_ref, wv_ref, w1_ref, w2_ref,
         a1_ref, a2_ref, g1_ref, g2_ref,
         r_ref, k_ref, v_ref, lw_ref, a_ref, g_ref) = refs
    i = pl.program_id(0)
    gain = gain_ref[...]
    hn = _rms_hat(h_ref[...]) * gain
    prev_last = (_rms_hat(halo_ref[...]) * gain)[SHIFT_HALO - 1:SHIFT_HALO, :]
    row = lax.broadcasted_iota(jnp.int32, (tm, 1), 0)
    shifted = jnp.where(row == 0, prev_last, pltpu.roll(hn, 1, 0))
    shifted = jnp.where(_row_in_seq(tm, i, seq_rows) == 0, 0.0, shifted)
    xx = shifted - hn

    def mixed(j):
        return (hn + xx * mix_ref[j:j + 1, :]).astype(BF16)

    w0, a0, v0 = vec_ref[0:1, :], vec_ref[1:2, :], vec_ref[2:3, :]
    r_ref[...] = _dot(mixed(0), wr_ref[...])
    t = jnp.tanh(_dot(mixed(1), w1_ref[...])).astype(BF16)
    w_in = w0 + _dot(t, w2_ref[...])
    lw_ref[...] = -math.exp(-0.5) * jax.nn.sigmoid(w_in)
    k_ref[...] = _dot(mixed(2), wk_ref[...])
    xv = mixed(3)
    v = _dot(xv, wv_ref[...])
    if has_vres:
        gate = jax.nn.sigmoid(v0 + _dot(_dot(xv, v1_ref[...]).astype(BF16), v2_ref[...]))
        v = v + (vfirst_ref[...] - v) * gate
    v_ref[...] = v
    a_ref[...] = jax.nn.sigmoid(a0 + _dot(_dot(mixed(4), a1_ref[...]).astype(BF16), a2_ref[...]))
    g_ref[...] = _dot(jax.nn.sigmoid(_dot(mixed(5), g1_ref[...])).astype(BF16), g2_ref[...])


def _pad_lora(w_in, w_out):
    hid = w_in.shape[1]
    hid_p = -(-hid // LANES) * LANES
    return (jnp.pad(w_in, ((0, 0), (0, hid_p - hid))).astype(BF16),
            jnp.pad(w_out, ((0, hid_p - hid), (0, 0))).astype(BF16))


def rwkv_proj(h, gain, mix, w0, a0, v0, wr, wk, wv, w1, w2, a1, a2, g1, g2, v1, v2, v_first,
              seq_rows, tm):
    m, d = h.shape
    has_vres = v_first is not None
    mix_p = jnp.pad(mix, ((0, 8 - mix.shape[0]), (0, 0)))
    vecs = jnp.stack([w0, a0, v0 if has_vres else jnp.zeros_like(w0)])
    vecs = jnp.pad(vecs, ((0, 8 - vecs.shape[0]), (0, 0)))
    w1p, w2p = _pad_lora(w1, w2)
    a1p, a2p = _pad_lora(a1, a2)
    g1p, g2p = _pad_lora(g1, g2)
    full = lambda a: pl.BlockSpec(a.shape, lambda i: (0, 0), pipeline_mode=pl.Buffered(1))
    tile = pl.BlockSpec((tm, d), lambda i: (i, 0))
    halo = pl.BlockSpec((SHIFT_HALO, d), lambda i: (jnp.maximum(i * (tm // SHIFT_HALO) - 1, 0), 0))
    args = [h, h, gain.reshape(1, d), mix_p, vecs, wr.astype(BF16), wk.astype(BF16), wv.astype(BF16),
            w1p, w2p, a1p, a2p, g1p, g2p]
    in_specs = [tile, halo] + [full(a) for a in args[2:]]
    if has_vres:
        v1p, v2p = _pad_lora(v1, v2)
        args += [v1p, v2p, v_first]
        in_specs += [full(v1p), full(v2p), tile]
    return pl.pallas_call(
        functools.partial(_rwkv_proj_kernel, has_vres, seq_rows, tm),
        grid=(m // tm,),
        in_specs=in_specs,
        out_specs=[tile] * 6,
        out_shape=[jax.ShapeDtypeStruct((m, d), F32)] * 6,
        compiler_params=_params(("parallel",)),
        name="rwkv_proj",
    )(*args)


def _mm(kind, a, b):
    fn = {"nn": _dot, "nt": _dot_nt, "tn": _dot_tn}[kind]
    return fn(a.astype(BF16), b.astype(BF16))


def _rwkv_rec_kernel(r_ref, k_ref, v_ref, lw_ref, a_ref, kk_ref, ka_ref, rk_ref, lg_ref, lb_ref,
                     y_ref, s_ref):
    c = CHUNK
    n_seq = r_ref.shape[0]
    n_pairs = r_ref.shape[2] // LANES

    @pl.when(pl.program_id(1) == 0)
    def _():
        s_ref[...] = jnp.zeros_like(s_ref)

    head0 = lax.broadcasted_iota(jnp.int32, (c, LANES), 1) < HEAD_DIM
    row2 = lax.broadcasted_iota(jnp.int32, (2 * c, 2 * c), 0)
    col2 = lax.broadcasted_iota(jnp.int32, (2 * c, 2 * c), 1)
    same_rows = (row2 // c) == (col2 // c)
    strict_bd = same_rows & (row2 % c > col2 % c)
    incl_bd = same_rows & (row2 % c >= col2 % c)
    eye2 = (row2 == col2).astype(F32)
    row_l = lax.broadcasted_iota(jnp.int32, (LANES, LANES), 0)
    col_l = lax.broadcasted_iota(jnp.int32, (LANES, LANES), 1)
    same_head = (row_l // HEAD_DIM) == (col_l // HEAD_DIM)
    ones_bd2 = jnp.concatenate([same_head, same_head], axis=0).astype(BF16)
    row_t = lax.broadcasted_iota(jnp.int32, (c, 2 * c), 0)
    col_t = lax.broadcasted_iota(jnp.int32, (c, 2 * c), 1)
    tri2 = (row_t >= col_t % c).astype(BF16)

    def head_sums(xs):
        x = jnp.concatenate(xs, axis=0)
        out = _dot(jnp.concatenate(_split_bf16(x), axis=1), ones_bd2)
        return [out[i * c:(i + 1) * c] for i in range(len(xs))]

    def stack(x):
        return jnp.concatenate([jnp.where(head0, x, 0.0), jnp.where(head0, 0.0, x)], axis=0)

    def unstack(x):
        return jnp.where(head0, x[:c], x[c:])

    cum_all = [_dot(tri2, jnp.concatenate(_split_bf16(lw_ref[q]), axis=0)) for q in range(n_seq)]

    pairs = range(n_seq * n_pairs)
    seqs = [p // n_pairs for p in pairs]
    sls = [slice((p % n_pairs) * LANES, (p % n_pairs + 1) * LANES) for p in pairs]
    ins_ = [(r_ref[q, :, sl], k_ref[q, :, sl], v_ref[q, :, sl], lw_ref[q, :, sl], a_ref[q, :, sl])
            for q, sl in zip(seqs, sls)]
    kku_ = [k_raw * kk_ref[:, sl] for (_, k_raw, _, _, _), sl in zip(ins_, sls)]
    kk_sq_ = head_sums([kk * kk for kk in kku_])
    r_, v_, k_, at_, rt_, bh_, kh_, pe_, aa_ = [], [], [], [], [], [], [], [], []
    for (r, k_raw, v, lw, a_sig), q, sl, kk, kk_sq in zip(ins_, seqs, sls, kku_, kk_sq_):
        cum = cum_all[q][:, sl]
        kk = kk * lax.rsqrt(jnp.maximum(kk_sq, 1e-24))
        k = k_raw * (1.0 + (a_sig - 1.0) * ka_ref[:, sl])
        b_vec = kk * a_sig
        cum_last = cum[c - 1:c, :]
        inv_p = jnp.exp(-cum)
        to_end = jnp.exp(cum_last - cum)
        at = -kk * jnp.exp(cum - lw)
        rt = r * jnp.exp(cum)
        aa_.append(_mm("nt", jnp.concatenate([stack(at), stack(rt)], axis=0),
                       jnp.concatenate([stack(b_vec * inv_p), stack(k * inv_p)], axis=0)))
        r_.append(r); v_.append(v); k_.append(k); at_.append(at); rt_.append(rt)
        bh_.append(b_vec * to_end); kh_.append(k * to_end); pe_.append(jnp.exp(cum_last))

    a_ab_ = [jnp.where(strict_bd, aa[:2 * c, :2 * c], 0.0) for aa in aa_]
    a_ak_ = [jnp.where(strict_bd, aa[:2 * c, 2 * c:], 0.0) for aa in aa_]
    a_r_ = [jnp.concatenate([jnp.where(incl_bd, aa[2 * c:, :2 * c], 0.0),
                             jnp.where(incl_bd, aa[2 * c:, 2 * c:], 0.0)], axis=1) for aa in aa_]

    t_inv_ = [eye2 + a for a in a_ab_]
    pw_ = [_mm("nn", a, a) for a in a_ab_]
    n_steps = c.bit_length() - 2
    for step in range(n_steps):
        if step < n_steps - 1:
            tp_ = [_mm("nn", jnp.concatenate([t, pw], axis=0), pw) for t, pw in zip(t_inv_, pw_)]
            t_inv_ = [t + tp[:2 * c] for t, tp in zip(t_inv_, tp_)]
            pw_ = [tp[2 * c:] for tp in tp_]
        else:
            t_inv_ = [t + _mm("nn", t, pw) for t, pw in zip(t_inv_, pw_)]

    v2_ = [jnp.concatenate([v, v], axis=0) for v in v_]
    akv_ = [_mm("nn", a_ak, v2) for a_ak, v2 in zip(a_ak_, v2_)]
    s0_ = [s_ref[p] for p in pairs]
    ars_ = [_mm("nt", jnp.concatenate([at, rt], axis=0), s0) for at, rt, s0 in zip(at_, rt_, s0_)]
    u_st_ = [_mm("nn", t, jnp.concatenate([ars[:c], ars[:c]], axis=0) + akv)
             for t, ars, akv in zip(t_inv_, ars_, akv_)]
    y_st_ = [jnp.concatenate([ars[c:], ars[c:]], axis=0) + _mm("nn", a_r, jnp.concatenate([u_st, v2], axis=0))
             for ars, a_r, u_st, v2 in zip(ars_, a_r_, u_st_, v2_)]
    upd_ = [_mm("tn", jnp.concatenate([unstack(u_st), v], axis=0), jnp.concatenate([bh, kh], axis=0))
            for u_st, v, bh, kh in zip(u_st_, v_, bh_, kh_)]
    for p in pairs:
        s_ref[p] = s0_[p] * pe_[p] + jnp.where(same_head, upd_[p], 0.0)

    inv_n = 1.0 / HEAD_DIM
    y_ = [unstack(y_st) for y_st in y_st_]
    dev_ = [y - mu * inv_n for y, mu in zip(y_, head_sums(y_))]
    var_ = [ss * inv_n for ss in head_sums([dev * dev for dev in dev_])]
    rk_ = head_sums([r * k * rk_ref[:, sl] for r, k, sl in zip(r_, k_, sls)])
    for p, sl in enumerate(sls):
        y_ref[seqs[p], :, sl] = (dev_[p] * lax.rsqrt(var_[p] + GN_EPS) * lg_ref[:, sl] + lb_ref[:, sl]
                                 + rk_[p] * v_[p])


def rwkv_recurrence(r, k, v, lw, a, k_k, k_a, r_k, lnx_g, lnx_b, batch):
    m, d = r.shape
    seq_rows = m // batch
    n_seq = REC_SEQS if batch % REC_SEQS == 0 else 1
    tile = pl.BlockSpec((n_seq, CHUNK, d), lambda b, c: (b, c, 0))
    vec = pl.BlockSpec((1, d), lambda b, c: (0, 0))
    vecs = [t.reshape(1, d) for t in (k_k, k_a, r_k, lnx_g, lnx_b)]
    seqs = [t.reshape(batch, seq_rows, d) for t in (r, k, v, lw, a)]
    return pl.pallas_call(
        _rwkv_rec_kernel,
        grid=(batch // n_seq, seq_rows // CHUNK),
        in_specs=[tile] * 5 + [vec] * 5,
        out_specs=tile,
        out_shape=jax.ShapeDtypeStruct((batch, seq_rows, d), F32),
        scratch_shapes=[pltpu.VMEM((n_seq * (d // LANES), LANES, LANES), F32)],
        compiler_params=_params(("parallel", "arbitrary")),
        name="rwkv_recurrence",
    )(*seqs, *vecs).reshape(m, d)


def _mix_ffn_kernel(has_gate, has_final, seq_rows, tm, tf, *refs):
    refs = list(refs)
    y_ref = refs.pop(0)
    g_ref = refs.pop(0) if has_gate else None
    h_ref = refs.pop(0)
    yh_ref = refs.pop(0)
    gh_ref = refs.pop(0) if has_gate else None
    hh_ref, wo_ref, gain_ref, wup_ref, cw_ref, cb_ref, wd_ref = refs[:7]
    fgain_ref = refs[7] if has_final else None
    o_ref, hmid_ref, xn_ref, act_ref = refs[-4:]
    i = pl.program_id(0)
    d_ff = wd_ref.shape[0]
    gain = gain_ref[...]

    def mixed_in(y_r, g_r, h_r, first_row):
        y = y_r[...] if g_r is None else y_r[...] * g_r[...]
        rows = h_r.shape[0]
        row = (first_row + lax.broadcasted_iota(jnp.int32, (rows, 1), 0)) % seq_rows
        return jnp.where(row >= PAD, h_r[...] + _dot(y.astype(BF16), wo_ref[...]), 0.0)

    halo = mixed_in(yh_ref, gh_ref, hh_ref, i * tm - CONV_HALO + seq_rows)
    hmid_ref[...] = mixed_in(y_ref, g_ref, h_ref, i * tm)
    xn_ref[0:CONV_HALO, :] = (_rms_hat(halo) * gain).astype(BF16)
    xn_ref[CONV_HALO:, :] = (_rms_hat(hmid_ref[...]) * gain).astype(BF16)
    xn = xn_ref[...]

    def conv(col):
        u = _dot(xn, wup_ref[:, col:col + tf])
        out = cb_ref[:, col:col + tf] + u[CONV_HALO:, :] * cw_ref[CONV_WIDTH - 1:CONV_WIDTH, col:col + tf]
        for tap in range(CONV_WIDTH - 1):
            back = CONV_WIDTH - 1 - tap
            out = out + pltpu.roll(u, back, 0)[CONV_HALO:, :] * cw_ref[tap:tap + 1, col:col + tf]
        return out

    for col in range(0, d_ff, tf):
        act_ref[:, col:col + tf] = (jax.nn.silu(conv(col)) * conv(d_ff + col)).astype(BF16)

    keep = _row_in_seq(tm, i, seq_rows) >= PAD
    h_out = jnp.where(keep, hmid_ref[...] + _dot(act_ref[...], wd_ref[...]), 0.0)
    o_ref[...] = h_out if fgain_ref is None else _rms_hat(h_out) * fgain_ref[...]


def mix_ffn(y, gate, h, w_o, gain, w_up, conv_w, conv_b, w_down, final_gain, seq_rows, tm, tf):
    m, d = h.shape
    d_ff = w_down.shape[0]
    cw = jnp.pad(conv_w, ((0, 8 - CONV_WIDTH), (0, 0)))
    cb = conv_b.reshape(1, 2 * d_ff)
    tile = pl.BlockSpec((tm, d), lambda i: (i, 0))
    halo = pl.BlockSpec((CONV_HALO, d), lambda i: (jnp.maximum(i * (tm // CONV_HALO) - 1, 0), 0))
    const = lambda shape: pl.BlockSpec(shape, lambda i: (0, 0), pipeline_mode=pl.Buffered(1))
    seqs = [y] + ([gate] if gate is not None else []) + [h]
    finals = [] if final_gain is None else [final_gain.reshape(1, d)]
    return pl.pallas_call(
        functools.partial(_mix_ffn_kernel, gate is not None, final_gain is not None, seq_rows, tm, tf),
        grid=(m // tm,),
        in_specs=[tile] * len(seqs) + [halo] * len(seqs) + [
            const((d, d)), const((1, d)), const((d, 2 * d_ff)), const((8, 2 * d_ff)), const((1, 2 * d_ff)),
            const((d_ff, d))] + [const((1, d))] * len(finals),
        out_specs=tile,
        out_shape=jax.ShapeDtypeStruct((m, d), F32),
        scratch_shapes=[pltpu.VMEM((tm, d), F32), pltpu.VMEM((tm + CONV_HALO, d), BF16),
                        pltpu.VMEM((tm, d_ff), BF16)],
        compiler_params=_params(("parallel",)),
        name="mix_ffn",
    )(*seqs, *seqs, w_o.astype(BF16), gain.reshape(1, d), w_up.astype(BF16), cw, cb, w_down.astype(BF16), *finals)


def _sb_attn_kernel(q_ref, k_ref, v_ref, o_ref):
    qi = pl.program_id(2)
    qb = q_ref.shape[0]
    n_groups = q_ref.shape[1] // LANES
    scale = HEAD_DIM ** -0.5
    assert math.frexp(scale)[0] == 0.5, "the score scale is folded into bf16 q, which needs a power of two"
    qs = [q_ref[:, g * LANES:(g + 1) * LANES] * jnp.asarray(scale, BF16) for g in range(n_groups)]
    head0_k = lax.broadcasted_iota(jnp.int32, (KEY_STEP, LANES), 1) < HEAD_DIM
    q_pos = qi * qb + lax.broadcasted_iota(jnp.int32, (qb, LANES), 0)
    k_off = lax.broadcasted_iota(jnp.int32, (qb, LANES), 1) % KEY_STEP
    row_b = lax.broadcasted_iota(jnp.int32, (2 * LANES, 2 * LANES), 0) % LANES
    col_b = lax.broadcasted_iota(jnp.int32, (2 * LANES, 2 * LANES), 1)
    same_head = (row_b // KEY_STEP) == ((col_b % LANES) // KEY_STEP)
    sum_mat = jnp.where(same_head & ((row_b > col_b) | (col_b >= LANES)), 1.0, 0.0).astype(BF16)

    def stack(x):
        zero = jnp.zeros_like(x)
        return jnp.concatenate([jnp.where(head0_k, x, zero), jnp.where(head0_k, zero, x)], axis=0)

    def key_rows(ref, j, g):
        start = pl.multiple_of(jnp.maximum(j, 0) * KEY_STEP, KEY_STEP)
        return stack(ref[pl.ds(start, KEY_STEP), g * LANES:(g + 1) * LANES])

    def valid_keys(j):
        k_pos = j * KEY_STEP + k_off
        return (k_pos < q_pos) & (k_pos >= PAD)

    def scores(zs, valids):
        log_betas = [jnp.minimum(z, 0.0) - jnp.log(1.0 + jnp.exp(-jnp.abs(z))) for z in zs]
        splits = [jnp.concatenate(_split_bf16(jnp.where(ok, lb - z, 0.0)), axis=1)
                  for lb, z, ok in zip(log_betas, zs, valids)]
        sums = _dot(jnp.concatenate(splits, axis=0), sum_mat)
        return log_betas, [sums[i * qb:(i + 1) * qb] for i in range(len(zs))]

    first = (qi * qb + qb - 1) // KEY_STEP

    steps = [first - t for t in range(ATTN_UNROLLED_STEPS)]
    valids = [valid_keys(j) for j in steps]
    zs, tile_valid = [], []
    for g in range(n_groups):
        keys = jnp.concatenate([key_rows(k_ref, j, g) for j in steps], axis=0)
        z_all = _dot_nt(qs[g], keys)
        zs += [z_all[:, t * LANES:(t + 1) * LANES] for t in range(len(steps))]
        tile_valid += valids
    log_betas, sums = scores(zs, tile_valid)
    carries, accs = [], []
    for g in range(n_groups):
        carry = jnp.zeros((qb, LANES), F32)
        ws = []
        for t in range(len(steps)):
            i = g * len(steps) + t
            ws.append(jnp.where(valids[t], jnp.exp(log_betas[i] + sums[i][:, :LANES] + carry), 0.0).astype(BF16))
            carry = carry + sums[i][:, LANES:]
        values = jnp.concatenate([key_rows(v_ref, j, g) for j in steps], axis=0)
        accs.append(_dot(jnp.concatenate(ws, axis=1), values))
        carries.append(carry)

    def alive(carries):
        top = carries[0]
        for c in carries[1:]:
            top = jnp.maximum(top, c)
        return (jnp.max(top) > LOG_W_FLOOR).astype(jnp.int32)

    def cond(state):
        j, live, _, _ = state
        return (j >= PAD // KEY_STEP) & (live > 0)

    def body(state):
        j, _, carries, accs = state
        ok = valid_keys(j)
        zs = [_dot_nt(qs[g], key_rows(k_ref, j, g)) for g in range(n_groups)]
        log_betas, sums = scores(zs, [ok] * n_groups)
        ws = [jnp.where(ok, jnp.exp(lb + sm[:, :LANES] + c), 0.0) for lb, sm, c in zip(log_betas, sums, carries)]
        accs = tuple(acc + _dot(w.astype(BF16), key_rows(v_ref, j, g)) for g, (acc, w) in enumerate(zip(accs, ws)))
        carries = tuple(c + sm[:, LANES:] for c, sm in zip(carries, sums))
        return j - 1, alive(carries), carries, accs

    state = lax.while_loop(cond, body, (first - ATTN_UNROLLED_STEPS, alive(carries), tuple(carries), tuple(accs)))
    for g, acc in enumerate(state[-1]):
        o_ref[:, g * LANES:(g + 1) * LANES] = acc.astype(o_ref.dtype)


def sb_attention(q, k, v, batch, groups):
    m, d = q.shape
    seq_rows = m // batch
    nq = seq_rows // ATTN_Q
    width = groups * LANES
    whole_seq = pl.BlockSpec((seq_rows, width), lambda b, p, i: (b, p), pipeline_mode=pl.Buffered(ATTN_KV_BUFFERS))
    return pl.pallas_call(
        _sb_attn_kernel,
        grid=(batch, d // width, nq),
        in_specs=[pl.BlockSpec((ATTN_Q, width), lambda b, p, i: (b * nq + i, p)), whole_seq, whole_seq],
        out_specs=pl.BlockSpec((ATTN_Q, width), lambda b, p, i: (b * nq + i, p)),
        out_shape=jax.ShapeDtypeStruct((m, d), BF16),
        compiler_params=_params(("parallel", "parallel", "arbitrary")),
        name="sb_attention",
    )(q, k, v)


def _row_tile(m, want):
    tm = want
    while m % tm:
        tm //= 2
    return tm


def kernel(x, meta_tokens, norm_mix_g, norm_ffn_g, ffn_up, ffn_conv_w, ffn_conv_b, ffn_down, rw_mix, rw_wr, rw_wk, rw_wv, rw_wo, rw_w0, rw_w1, rw_w2, rw_a0, rw_a1, rw_a2, rw_g1, rw_g2, rw_kk, rw_ka, rw_rk, rw_lnx_g, rw_lnx_b, rw_v0, rw_v1, rw_v2, kv_norm_g, sb_wk, sb_wv, sb_wq, sb_wo, final_norm_g):
    bsz, seq, d = x.shape
    depth = norm_mix_g.shape[0]
    n_a = rw_wr.shape[0]
    d_ff = ffn_down.shape[1]
    seq_rows = PAD + N_META + seq
    assert seq % BLOCK == 0 and d % LANES == 0 and seq_rows % CHUNK == 0
    m = bsz * seq_rows
    tm_big = _row_tile(m, 512)
    tm_proj = _row_tile(m, 512)
    tf = 256 if d_ff % 256 == 0 else LANES

    meta = jnp.broadcast_to(meta_tokens.astype(x.dtype)[None], (bsz, N_META, d))
    h = jnp.concatenate([jnp.zeros((bsz, PAD, d), x.dtype), meta, x], axis=1).reshape(m, d)

    v_first = None
    k_sh = v_sh = None
    for layer in range(depth):
        if layer < n_a:
            i = layer
            vres = (None, None, None) if i == 0 else (rw_v0[i - 1], rw_v1[i - 1], rw_v2[i - 1])
            r, k, v, lw, a, g = rwkv_proj(
                h, norm_mix_g[layer], rw_mix[i], rw_w0[i], rw_a0[i], vres[0], rw_wr[i], rw_wk[i], rw_wv[i],
                rw_w1[i], rw_w2[i], rw_a1[i], rw_a2[i], rw_g1[i], rw_g2[i], vres[1], vres[2],
                None if i == 0 else v_first, seq_rows, tm_proj)
            if i == 0:
                v_first = v
            y = rwkv_recurrence(r, k, v, lw, a, rw_kk[i], rw_ka[i], rw_rk[i].reshape(-1),
                                rw_lnx_g[i], rw_lnx_b[i], bsz)
            mixer = (y, g, rw_wo[i])
        else:
            j = layer - n_a
            if layer == n_a:
                q, k_sh, v_sh = norm_proj(
                    h, [norm_mix_g[layer], kv_norm_g, kv_norm_g],
                    [sb_wq[j].astype(BF16), sb_wk.astype(BF16), sb_wv.astype(BF16)], BF16, tm_big)
            else:
                (q,) = norm_proj(h, [norm_mix_g[layer]], [sb_wq[j].astype(BF16)], BF16, tm_big)
            o = sb_attention(q, k_sh, v_sh, bsz, ATTN_GROUPS if (d // LANES) % ATTN_GROUPS == 0 else 1)
            mixer = (o, None, sb_wo[j])
        h = mix_ffn(mixer[0], mixer[1], h, mixer[2], norm_ffn_g[layer], ffn_up[layer], ffn_conv_w[layer],
                    ffn_conv_b[layer], ffn_down[layer], final_norm_g if layer == depth - 1 else None,
                    seq_rows, tm_big, tf)
    return h.reshape(bsz, seq_rows, d)[:, PAD + N_META:, :]
```

```python
import functools
import math

import jax
import jax.numpy as jnp
from jax import lax
from jax.experimental import pallas as pl
from jax.experimental.pallas import tpu as pltpu

HEAD_DIM = 64
N_META = 16
BLOCK = 128
PAD = BLOCK - N_META
CONV_WIDTH = 3
GN_EPS = 64e-5
RMS_EPS = 1e-6

LANES = 128
CHUNK = 64
CONV_HALO = 16
SHIFT_HALO = 8
REC_SEQS = 2
ATTN_GROUPS = 8
ATTN_Q = BLOCK
KEY_STEP = HEAD_DIM
ATTN_UNROLLED_STEPS = 4
ATTN_KV_BUFFERS = 1
LOG_W_FLOOR = -100.0
VMEM_LIMIT = 56 * 1024 * 1024

F32 = jnp.float32
BF16 = jnp.bfloat16


def _dot(a, b):
    return jnp.dot(a, b, preferred_element_type=F32)


def _dot_nt(a, b):
    return lax.dot_general(a, b, (((1,), (1,)), ((), ())), preferred_element_type=F32)


def _dot_tn(a, b):
    return lax.dot_general(a, b, (((0,), (0,)), ((), ())), preferred_element_type=F32)


def _split_bf16(x):
    hi = x.astype(BF16)
    return hi, (x - hi.astype(F32)).astype(BF16)


def _rms_hat(x):
    return x * lax.rsqrt(jnp.mean(x * x, axis=-1, keepdims=True) + RMS_EPS)


def _params(sem):
    return pltpu.CompilerParams(dimension_semantics=sem, vmem_limit_bytes=VMEM_LIMIT)


def _row_in_seq(tile_rows, tile_index, seq_rows):
    rows = tile_index * tile_rows + lax.broadcasted_iota(jnp.int32, (tile_rows, 1), 0)
    return rows % seq_rows


def _norm_proj_kernel(n_groups, x_ref, *refs):
    g_refs = refs[:n_groups]
    w_refs = refs[n_groups:2 * n_groups]
    o_refs = refs[2 * n_groups:]
    xhat = _rms_hat(x_ref[...])
    for g_ref, w_ref, o_ref in zip(g_refs, w_refs, o_refs):
        xn = (xhat * g_ref[...]).astype(BF16)
        o_ref[...] = _dot(xn, w_ref[...]).astype(o_ref.dtype)


def norm_proj(h, gains, weights, out_dtype, tm):
    m, d = h.shape
    n = len(gains)
    in_specs = [pl.BlockSpec((tm, d), lambda i: (i, 0))]
    in_specs += [pl.BlockSpec((1, d), lambda i: (0, 0)) for _ in gains]
    in_specs += [pl.BlockSpec(w.shape, lambda i: (0, 0)) for w in weights]
    out_specs = [pl.BlockSpec((tm, w.shape[1]), lambda i: (i, 0)) for w in weights]
    out_shape = [jax.ShapeDtypeStruct((m, w.shape[1]), out_dtype) for w in weights]
    return pl.pallas_call(
        functools.partial(_norm_proj_kernel, n),
        grid=(m // tm,),
        in_specs=in_specs, out_specs=out_specs, out_shape=out_shape,
        compiler_params=_params(("parallel",)),
        name="norm_proj",
    )(h, *[g.reshape(1, d) for g in gains], *weights)


def _rwkv_proj_kernel(has_vres, seq_rows, tm, *refs):
    if has_vres:
        (h_ref, halo_ref, gain_ref, mix_ref, vec_ref, wr_ref, wk_ref, wv_ref, w1_ref, w2_ref,
         a1_ref, a2_ref, g1_ref, g2_ref, v1_ref, v2_ref, vfirst_ref,
         r_ref, k_ref, v_ref, lw_ref, a_ref, g_ref) = refs
    else:
        (h_ref, halo_ref, gain_ref, mix_ref, vec_ref, wr_ref, wk_ref, wv_ref, w1_ref, w2_ref,
         a1_ref, a2_ref, g1_ref, g2_ref,
         r_ref, k_ref, v_ref, lw_ref, a_ref, g_ref) = refs
    i = pl.program_id(0)
    gain = gain_ref[...]
    hn = _rms_hat(h_ref[...]) * gain
    prev_last = (_rms_hat(halo_ref[...]) * gain)[SHIFT_HALO - 1:SHIFT_HALO, :]
    row = lax.broadcasted_iota(jnp.int32, (tm, 1), 0)
    shifted = jnp.where(row == 0, prev_last, pltpu.roll(hn, 1, 0))
    shifted = jnp.where(_row_in_seq(tm, i, seq_rows) == 0, 0.0, shifted)
    xx = shifted - hn

    def mixed(j):
        return (hn + xx * mix_ref[j:j + 1, :]).astype(BF16)

    w0, a0, v0 = vec_ref[0:1, :], vec_ref[1:2, :], vec_ref[2:3, :]
    r_ref[...] = _dot(mixed(0), wr_ref[...])
    t = jnp.tanh(_dot(mixed(1), w1_ref[...])).astype(BF16)
    w_in = w0 + _dot(t, w2_ref[...])
    lw_ref[...] = -math.exp(-0.5) * jax.nn.sigmoid(w_in)
    k_ref[...] = _dot(mixed(2), wk_ref[...])
    xv = mixed(3)
    v = _dot(xv, wv_ref[...])
    if has_vres:
        gate = jax.nn.sigmoid(v0 + _dot(_dot(xv, v1_ref[...]).astype(BF16), v2_ref[...]))
        v = v + (vfirst_ref[...] - v) * gate
    v_ref[...] = v
    a_ref[...] = jax.nn.sigmoid(a0 + _dot(_dot(mixed(4), a1_ref[...]).astype(BF16), a2_ref[...]))
    g_ref[...] = _dot(jax.nn.sigmoid(_dot(mixed(5), g1_ref[...])).astype(BF16), g2_ref[...])


def _pad_lora(w_in, w_out):
    hid = w_in.shape[1]
    hid_p = -(-hid // LANES) * LANES
    return (jnp.pad(w_in, ((0, 0), (0, hid_p - hid))).astype(BF16),
            jnp.pad(w_out, ((0, hid_p - hid), (0, 0))).astype(BF16))


def rwkv_proj(h, gain, mix, w0, a0, v0, wr, wk, wv, w1, w2, a1, a2, g1, g2, v1, v2, v_first,
              seq_rows, tm):
    m, d = h.shape
    has_vres = v_first is not None
    mix_p = jnp.pad(mix, ((0, 8 - mix.shape[0]), (0, 0)))
    vecs = jnp.stack([w0, a0, v0 if has_vres else jnp.zeros_like(w0)])
    vecs = jnp.pad(vecs, ((0, 8 - vecs.shape[0]), (0, 0)))
    w1p, w2p = _pad_lora(w1, w2)
    a1p, a2p = _pad_lora(a1, a2)
    g1p, g2p = _pad_lora(g1, g2)
    full = lambda a: pl.BlockSpec(a.shape, lambda i: (0, 0), pipeline_mode=pl.Buffered(1))
    tile = pl.BlockSpec((tm, d), lambda i: (i, 0))
    halo = pl.BlockSpec((SHIFT_HALO, d), lambda i: (jnp.maximum(i * (tm // SHIFT_HALO) - 1, 0), 0))
    args = [h, h, gain.reshape(1, d), mix_p, vecs, wr.astype(BF16), wk.astype(BF16), wv.astype(BF16),
            w1p, w2p, a1p, a2p, g1p, g2p]
    in_specs = [tile, halo] + [full(a) for a in args[2:]]
    if has_vres:
        v1p, v2p = _pad_lora(v1, v2)
        args += [v1p, v2p, v_first]
        in_specs += [full(v1p), full(v2p), tile]
    return pl.pallas_call(
        functools.partial(_rwkv_proj_kernel, has_vres, seq_rows, tm),
        grid=(m // tm,),
        in_specs=in_specs,
        out_specs=[tile] * 6,
        out_shape=[jax.ShapeDtypeStruct((m, d), F32)] * 6,
        compiler_params=_params(("parallel",)),
        name="rwkv_proj",
    )(*args)


def _mm(kind, a, b):
    fn = {"nn": _dot, "nt": _dot_nt, "tn": _dot_tn}[kind]
    return fn(a.astype(BF16), b.astype(BF16))


def _rwkv_rec_kernel(r_ref, k_ref, v_ref, lw_ref, a_ref, kk_ref, ka_ref, rk_ref, lg_ref, lb_ref,
                     y_ref, s_ref):
    c = CHUNK
    n_seq = r_ref.shape[0]
    n_pairs = r_ref.shape[2] // LANES

    @pl.when(pl.program_id(1) == 0)
    def _():
        s_ref[...] = jnp.zeros_like(s_ref)

    head0 = lax.broadcasted_iota(jnp.int32, (c, LANES), 1) < HEAD_DIM
    row2 = lax.broadcasted_iota(jnp.int32, (2 * c, 2 * c), 0)
    col2 = lax.broadcasted_iota(jnp.int32, (2 * c, 2 * c), 1)
    same_rows = (row2 // c) == (col2 // c)
    strict_bd = same_rows & (row2 % c > col2 % c)
    incl_bd = same_rows & (row2 % c >= col2 % c)
    eye2 = (row2 == col2).astype(F32)
    row_l = lax.broadcasted_iota(jnp.int32, (LANES, LANES), 0)
    col_l = lax.broadcasted_iota(jnp.int32, (LANES, LANES), 1)
    same_head = (row_l // HEAD_DIM) == (col_l // HEAD_DIM)
    ones_bd2 = jnp.concatenate([same_head, same_head], axis=0).astype(BF16)
    row_t = lax.broadcasted_iota(jnp.int32, (c, 2 * c), 0)
    col_t = lax.broadcasted_iota(jnp.int32, (c, 2 * c), 1)
    tri2 = (row_t >= col_t % c).astype(BF16)

    def head_sums(xs):
        x = jnp.concatenate(xs, axis=0)
        out = _dot(jnp.concatenate(_split_bf16(x), axis=1), ones_bd2)
        return [out[i * c:(i + 1) * c] for i in range(len(xs))]

    def stack(x):
        return jnp.concatenate([jnp.where(head0, x, 0.0), jnp.where(head0, 0.0, x)], axis=0)

    def unstack(x):
        return jnp.where(head0, x[:c], x[c:])

    cum_all = [_dot(tri2, jnp.concatenate(_split_bf16(lw_ref[q]), axis=0)) for q in range(n_seq)]

    pairs = range(n_seq * n_pairs)
    seqs = [p // n_pairs for p in pairs]
    sls = [slice((p % n_pairs) * LANES, (p % n_pairs + 1) * LANES) for p in pairs]
    ins_ = [(r_ref[q, :, sl], k_ref[q, :, sl], v_ref[q, :, sl], lw_ref[q, :, sl], a_ref[q, :, sl])
            for q, sl in zip(seqs, sls)]
    kku_ = [k_raw * kk_ref[:, sl] for (_, k_raw, _, _, _), sl in zip(ins_, sls)]
    kk_sq_ = head_sums([kk * kk for kk in kku_])
    r_, v_, k_, at_, rt_, bh_, kh_, pe_, aa_ = [], [], [], [], [], [], [], [], []
    for (r, k_raw, v, lw, a_sig), q, sl, kk, kk_sq in zip(ins_, seqs, sls, kku_, kk_sq_):
        cum = cum_all[q][:, sl]
        kk = kk * lax.rsqrt(jnp.maximum(kk_sq, 1e-24))
        k = k_raw * (1.0 + (a_sig - 1.0) * ka_ref[:, sl])
        b_vec = kk * a_sig
        cum_last = cum[c - 1:c, :]
        inv_p = jnp.exp(-cum)
        to_end = jnp.exp(cum_last - cum)
        at = -kk * jnp.exp(cum - lw)
        rt = r * jnp.exp(cum)
        aa_.append(_mm("nt", jnp.concatenate([stack(at), stack(rt)], axis=0),
                       jnp.concatenate([stack(b_vec * inv_p), stack(k * inv_p)], axis=0)))
        r_.append(r); v_.append(v); k_.append(k); at_.append(at); rt_.append(rt)
        bh_.append(b_vec * to_end); kh_.append(k * to_end); pe_.append(jnp.exp(cum_last))

    a_ab_ = [jnp.where(strict_bd, aa[:2 * c, :2 * c], 0.0) for aa in aa_]
    a_ak_ = [jnp.where(strict_bd, aa[:2 * c, 2 * c:], 0.0) for aa in aa_]
    a_r_ = [jnp.concatenate([jnp.where(incl_bd, aa[2 * c:, :2 * c], 0.0),
                             jnp.where(incl_bd, aa[2 * c:, 2 * c:], 0.0)], axis=1) for aa in aa_]

    t_inv_ = [eye2 + a for a in a_ab_]
    pw_ = [_mm("nn", a, a) for a in a_ab_]
    n_steps = c.bit_length() - 2
    for step in range(n_steps):
        if step < n_steps - 1:
            tp_ = [_mm("nn", jnp.concatenate([t, pw], axis=0), pw) for t, pw in zip(t_inv_, pw_)]
            t_inv_ = [t + tp[:2 * c] for t, tp in zip(t_inv_, tp_)]
            pw_ = [tp[2 * c:] for tp in tp_]
        else:
            t_inv_ = [t + _mm("nn", t, pw) for t, pw in zip(t_inv_, pw_)]

    v2_ = [jnp.concatenate([v, v], axis=0) for v in v_]
    akv_ = [_mm("nn", a_ak, v2) for a_ak, v2 in zip(a_ak_, v2_)]
    s0_ = [s_ref[p] for p in pairs]
    ars_ = [_mm("nt", jnp.concatenate([at, rt], axis=0), s0) for at, rt, s0 in zip(at_, rt_, s0_)]
    u_st_ = [_mm("nn", t, jnp.concatenate([ars[:c], ars[:c]], axis=0) + akv)
             for t, ars, akv in zip(t_inv_, ars_, akv_)]
    y_st_ = [jnp.concatenate([ars[c:], ars[c:]], axis=0) + _mm("nn", a_r, jnp.concatenate([u_st, v2], axis=0))
             for ars, a_r, u_st, v2 in zip(ars_, a_r_, u_st_, v2_)]
    upd_ = [_mm("tn", jnp.concatenate([unstack(u_st), v], axis=0), jnp.concatenate([bh, kh], axis=0))
            for u_st, v, bh, kh in zip(u_st_, v_, bh_, kh_)]
    for p in pairs:
        s_ref[p] = s0_[p] * pe_[p] + jnp.where(same_head, upd_[p], 0.0)

    inv_n = 1.0 / HEAD_DIM
    y_ = [unstack(y_st) for y_st in y_st_]
    dev_ = [y - mu * inv_n for y, mu in zip(y_, head_sums(y_))]
    var_ = [ss * inv_n for ss in head_sums([dev * dev for dev in dev_])]
    rk_ = head_sums([r * k * rk_ref[:, sl] for r, k, sl in zip(r_, k_, sls)])
    for p, sl in enumerate(sls):
        y_ref[seqs[p], :, sl] = (dev_[p] * lax.rsqrt(var_[p] + GN_EPS) * lg_ref[:, sl] + lb_ref[:, sl]
                                 + rk_[p] * v_[p])


def rwkv_recurrence(r, k, v, lw, a, k_k, k_a, r_k, lnx_g, lnx_b, batch):
    m, d = r.shape
    seq_rows = m // batch
    n_seq = REC_SEQS if batch % REC_SEQS == 0 else 1
    tile = pl.BlockSpec((n_seq, CHUNK, d), lambda b, c: (b, c, 0))
    vec = pl.BlockSpec((1, d), lambda b, c: (0, 0))
    vecs = [t.reshape(1, d) for t in (k_k, k_a, r_k, lnx_g, lnx_b)]
    seqs = [t.reshape(batch, seq_rows, d) for t in (r, k, v, lw, a)]
    return pl.pallas_call(
        _rwkv_rec_kernel,
        grid=(batch // n_seq, seq_rows // CHUNK),
        in_specs=[tile] * 5 + [vec] * 5,
        out_specs=tile,
        out_shape=jax.ShapeDtypeStruct((batch, seq_rows, d), F32),
        scratch_shapes=[pltpu.VMEM((n_seq * (d // LANES), LANES, LANES), F32)],
        compiler_params=_params(("parallel", "arbitrary")),
        name="rwkv_recurrence",
    )(*seqs, *vecs).reshape(m, d)


def _mix_ffn_kernel(has_gate, has_final, seq_rows, tm, tf, *refs):
    refs = list(refs)
    y_ref = refs.pop(0)
    g_ref = refs.pop(0) if has_gate else None
    h_ref = refs.pop(0)
    yh_ref = refs.pop(0)
    gh_ref = refs.pop(0) if has_gate else None
    hh_ref, wo_ref, gain_ref, wup_ref, cw_ref, cb_ref, wd_ref = refs[:7]
    fgain_ref = refs[7] if has_final else None
    o_ref, hmid_ref, xn_ref, act_ref = refs[-4:]
    i = pl.program_id(0)
    d_ff = wd_ref.shape[0]
    gain = gain_ref[...]

    def mixed_in(y_r, g_r, h_r, first_row):
        y = y_r[...] if g_r is None else y_r[...] * g_r[...]
        rows = h_r.shape[0]
        row = (first_row + lax.broadcasted_iota(jnp.int32, (rows, 1), 0)) % seq_rows
        return jnp.where(row >= PAD, h_r[...] + _dot(y.astype(BF16), wo_ref[...]), 0.0)

    halo = mixed_in(yh_ref, gh_ref, hh_ref, i * tm - CONV_HALO + seq_rows)
    hmid_ref[...] = mixed_in(y_ref, g_ref, h_ref, i * tm)
    xn_ref[0:CONV_HALO, :] = (_rms_hat(halo) * gain).astype(BF16)
    xn_ref[CONV_HALO:, :] = (_rms_hat(hmid_ref[...]) * gain).astype(BF16)
    xn = xn_ref[...]

    def conv(col):
        u = _dot(xn, wup_ref[:, col:col + tf])
        out = cb_ref[:, col:col + tf] + u[CONV_HALO:, :] * cw_ref[CONV_WIDTH - 1:CONV_WIDTH, col:col + tf]
        for tap in range(CONV_WIDTH - 1):
            back = CONV_WIDTH - 1 - tap
            out = out + pltpu.roll(u, back, 0)[CONV_HALO:, :] * cw_ref[tap:tap + 1, col:col + tf]
        return out

    for col in range(0, d_ff, tf):
        act_ref[:, col:col + tf] = (jax.nn.silu(conv(col)) * conv(d_ff + col)).astype(BF16)

    keep = _row_in_seq(tm, i, seq_rows) >= PAD
    h_out = jnp.where(keep, hmid_ref[...] + _dot(act_ref[...], wd_ref[...]), 0.0)
    o_ref[...] = h_out if fgain_ref is None else _rms_hat(h_out) * fgain_ref[...]


def mix_ffn(y, gate, h, w_o, gain, w_up, conv_w, conv_b, w_down, final_gain, seq_rows, tm, tf):
    m, d = h.shape
    d_ff = w_down.shape[0]
    cw = jnp.pad(conv_w, ((0, 8 - CONV_WIDTH), (0, 0)))
    cb = conv_b.reshape(1, 2 * d_ff)
    tile = pl.BlockSpec((tm, d), lambda i: (i, 0))
    halo = pl.BlockSpec((CONV_HALO, d), lambda i: (jnp.maximum(i * (tm // CONV_HALO) - 1, 0), 0))
    const = lambda shape: pl.BlockSpec(shape, lambda i: (0, 0), pipeline_mode=pl.Buffered(1))
    seqs = [y] + ([gate] if gate is not None else []) + [h]
    finals = [] if final_gain is None else [final_gain.reshape(1, d)]
    return pl.pallas_call(
        functools.partial(_mix_ffn_kernel, gate is not None, final_gain is not None, seq_rows, tm, tf),
        grid=(m // tm,),
        in_specs=[tile] * len(seqs) + [halo] * len(seqs) + [
            const((d, d)), const((1, d)), const((d, 2 * d_ff)), const((8, 2 * d_ff)), const((1, 2 * d_ff)),
            const((d_ff, d))] + [const((1, d))] * len(finals),
        out_specs=tile,
        out_shape=jax.ShapeDtypeStruct((m, d), F32),
        scratch_shapes=[pltpu.VMEM((tm, d), F32), pltpu.VMEM((tm + CONV_HALO, d), BF16),
                        pltpu.VMEM((tm, d_ff), BF16)],
        compiler_params=_params(("parallel",)),
        name="mix_ffn",
    )(*seqs, *seqs, w_o.astype(BF16), gain.reshape(1, d), w_up.astype(BF16), cw, cb, w_down.astype(BF16), *finals)


def _sb_attn_kernel(q_ref, k_ref, v_ref, o_ref):
    qi = pl.program_id(2)
    qb = q_ref.shape[0]
    n_groups = q_ref.shape[1] // LANES
    scale = HEAD_DIM ** -0.5
    assert math.frexp(scale)[0] == 0.5, "the score scale is folded into bf16 q, which needs a power of two"
    qs = [q_ref[:, g * LANES:(g + 1) * LANES] * jnp.asarray(scale, BF16) for g in range(n_groups)]
    head0_k = lax.broadcasted_iota(jnp.int32, (KEY_STEP, LANES), 1) < HEAD_DIM
    q_pos = qi * qb + lax.broadcasted_iota(jnp.int32, (qb, LANES), 0)
    row_b = lax.broadcasted_iota(jnp.int32, (2 * LANES, 2 * LANES), 0) % LANES
    col_b = lax.broadcasted_iota(jnp.int32, (2 * LANES, 2 * LANES), 1)
    same_head = (row_b // KEY_STEP) == ((col_b % LANES) // KEY_STEP)
    sum_mat = jnp.where(same_head & ((row_b > col_b) | (col_b >= LANES)), 1.0, 0.0).astype(BF16)

    def stack(x):
        zero = jnp.zeros_like(x)
        return jnp.concatenate([jnp.where(head0_k, x, zero), jnp.where(head0_k, zero, x)], axis=0)

    def key_rows(ref, j, g):
        start = pl.multiple_of(jnp.maximum(j, 0) * KEY_STEP, KEY_STEP)
        return stack(ref[pl.ds(start, KEY_STEP), g * LANES:(g + 1) * LANES])

    def valid_keys(j, pos):
        k_pos = j * KEY_STEP + lax.broadcasted_iota(jnp.int32, pos.shape, 1) % KEY_STEP
        return (k_pos < pos) & (k_pos >= PAD)

    def scores(zs, valids):
        rows = zs[0].shape[0]
        log_betas = [jnp.minimum(z, 0.0) - jnp.log(1.0 + jnp.exp(-jnp.abs(z))) for z in zs]
        splits = [jnp.concatenate(_split_bf16(jnp.where(ok, lb - z, 0.0)), axis=1)
                  for lb, z, ok in zip(log_betas, zs, valids)]
        sums = _dot(jnp.concatenate(splits, axis=0), sum_mat)
        return log_betas, [sums[i * rows:(i + 1) * rows] for i in range(len(zs))]

    n_sub = qb // KEY_STEP
    sub_rows = [slice(sb * KEY_STEP, (sb + 1) * KEY_STEP) for sb in range(n_sub)]
    sub_steps = [[qi * n_sub + sb - t for t in range(ATTN_UNROLLED_STEPS)] for sb in range(n_sub)]
    sub_pos = [qi * qb + sb * KEY_STEP + lax.broadcasted_iota(jnp.int32, (KEY_STEP, LANES), 0) for sb in range(n_sub)]
    valids = [[valid_keys(j, pos) for j in steps] for pos, steps in zip(sub_pos, sub_steps)]
    window = [qi * n_sub + n_sub - 1 - t for t in range(ATTN_UNROLLED_STEPS + n_sub - 1)]
    zs, tile_valid = [], []
    for g in range(n_groups):
        k_tiles = [key_rows(k_ref, j, g) for j in window]
        for sb in range(n_sub):
            lo = n_sub - 1 - sb
            keys = jnp.concatenate(k_tiles[lo:lo + ATTN_UNROLLED_STEPS], axis=0)
            z_all = _dot_nt(qs[g][sub_rows[sb]], keys)
            zs += [z_all[:, t * LANES:(t + 1) * LANES] for t in range(ATTN_UNROLLED_STEPS)]
            tile_valid += valids[sb]
    log_betas, sums = scores(zs, tile_valid)
    carries, accs = [], []
    for g in range(n_groups):
        v_tiles = [key_rows(v_ref, j, g) for j in window]
        carry_g, acc_g = [], []
        for sb in range(n_sub):
            carry = jnp.zeros((KEY_STEP, LANES), F32)
            ws = []
            for t in range(ATTN_UNROLLED_STEPS):
                i = (g * n_sub + sb) * ATTN_UNROLLED_STEPS + t
                ws.append(jnp.where(valids[sb][t], jnp.exp(log_betas[i] + sums[i][:, :LANES] + carry), 0.0)
                          .astype(BF16))
                carry = carry + sums[i][:, LANES:]
            lo = n_sub - 1 - sb
            values = jnp.concatenate(v_tiles[lo:lo + ATTN_UNROLLED_STEPS], axis=0)
            acc_g.append(_dot(jnp.concatenate(ws, axis=1), values))
            carry_g.append(carry)
        accs.append(jnp.concatenate(acc_g, axis=0))
        carries.append(jnp.concatenate(carry_g, axis=0))

    def alive(carries):
        top = carries[0]
        for c in carries[1:]:
            top = jnp.maximum(top, c)
        return (jnp.max(top) > LOG_W_FLOOR).astype(jnp.int32)

    def cond(state):
        j, live, _, _ = state
        return (j >= PAD // KEY_STEP) & (live > 0)

    def body(state):
        j, _, carries, accs = state
        ok = valid_keys(j, q_pos) & (j <= q_pos // KEY_STEP - ATTN_UNROLLED_STEPS)
        zs = [_dot_nt(qs[g], key_rows(k_ref, j, g)) for g in range(n_groups)]
        log_betas, sums = scores(zs, [ok] * n_groups)
        ws = [jnp.where(ok, jnp.exp(lb + sm[:, :LANES] + c), 0.0) for lb, sm, c in zip(log_betas, sums, carries)]
        accs = tuple(acc + _dot(w.astype(BF16), key_rows(v_ref, j, g)) for g, (acc, w) in enumerate(zip(accs, ws)))
        carries = tuple(c + sm[:, LANES:] for c, sm in zip(carries, sums))
        return j - 1, alive(carries), carries, accs

    rest = qi * n_sub + n_sub - 1 - ATTN_UNROLLED_STEPS
    state = lax.while_loop(cond, body, (rest, alive(carries), tuple(carries), tuple(accs)))
    for g, acc in enumerate(state[-1]):
        o_ref[:, g * LANES:(g + 1) * LANES] = acc.astype(o_ref.dtype)


def sb_attention(q, k, v, batch, groups):
    m, d = q.shape
    seq_rows = m // batch
    nq = seq_rows // ATTN_Q
    width = groups * LANES
    whole_seq = pl.BlockSpec((seq_rows, width), lambda b, p, i: (b, p), pipeline_mode=pl.Buffered(ATTN_KV_BUFFERS))
    return pl.pallas_call(
        _sb_attn_kernel,
        grid=(batch, d // width, nq),
        in_specs=[pl.BlockSpec((ATTN_Q, width), lambda b, p, i: (b * nq + i, p)), whole_seq, whole_seq],
        out_specs=pl.BlockSpec((ATTN_Q, width), lambda b, p, i: (b * nq + i, p)),
        out_shape=jax.ShapeDtypeStruct((m, d), BF16),
        compiler_params=_params(("parallel", "parallel", "arbitrary")),
        name="sb_attention",
    )(q, k, v)


def _row_tile(m, want):
    tm = want
    while m % tm:
        tm //= 2
    return tm


def kernel(x, meta_tokens, norm_mix_g, norm_ffn_g, ffn_up, ffn_conv_w, ffn_conv_b, ffn_down, rw_mix, rw_wr, rw_wk, rw_wv, rw_wo, rw_w0, rw_w1, rw_w2, rw_a0, rw_a1, rw_a2, rw_g1, rw_g2, rw_kk, rw_ka, rw_rk, rw_lnx_g, rw_lnx_b, rw_v0, rw_v1, rw_v2, kv_norm_g, sb_wk, sb_wv, sb_wq, sb_wo, final_norm_g):
    bsz, seq, d = x.shape
    depth = norm_mix_g.shape[0]
    n_a = rw_wr.shape[0]
    d_ff = ffn_down.shape[1]
    seq_rows = PAD + N_META + seq
    assert seq % BLOCK == 0 and d % LANES == 0 and seq_rows % CHUNK == 0
    m = bsz * seq_rows
    tm_big = _row_tile(m, 512)
    tm_proj = _row_tile(m, 512)
    tf = 256 if d_ff % 256 == 0 else LANES

    meta = jnp.broadcast_to(meta_tokens.astype(x.dtype)[None], (bsz, N_META, d))
    h = jnp.concatenate([jnp.zeros((bsz, PAD, d), x.dtype), meta, x], axis=1).reshape(m, d)

    v_first = None
    k_sh = v_sh = None
    for layer in range(depth):
        if layer < n_a:
            i = layer
            vres = (None, None, None) if i == 0 else (rw_v0[i - 1], rw_v1[i - 1], rw_v2[i - 1])
            r, k, v, lw, a, g = rwkv_proj(
                h, norm_mix_g[layer], rw_mix[i], rw_w0[i], rw_a0[i], vres[0], rw_wr[i], rw_wk[i], rw_wv[i],
                rw_w1[i], rw_w2[i], rw_a1[i], rw_a2[i], rw_g1[i], rw_g2[i], vres[1], vres[2],
                None if i == 0 else v_first, seq_rows, tm_proj)
            if i == 0:
                v_first = v
            y = rwkv_recurrence(r, k, v, lw, a, rw_kk[i], rw_ka[i], rw_rk[i].reshape(-1),
                                rw_lnx_g[i], rw_lnx_b[i], bsz)
            mixer = (y, g, rw_wo[i])
        else:
            j = layer - n_a
            if layer == n_a:
                q, k_sh, v_sh = norm_proj(
                    h, [norm_mix_g[layer], kv_norm_g, kv_norm_g],
                    [sb_wq[j].astype(BF16), sb_wk.astype(BF16), sb_wv.astype(BF16)], BF16, tm_big)
            else:
                (q,) = norm_proj(h, [norm_mix_g[layer]], [sb_wq[j].astype(BF16)], BF16, tm_big)
            o = sb_attention(q, k_sh, v_sh, bsz, ATTN_GROUPS if (d // LANES) % ATTN_GROUPS == 0 else 1)
            mixer = (o, None, sb_wo[j])
        h = mix_ffn(mixer[0], mixer[1], h, mixer[2], norm_ffn_g[layer], ffn_up[layer], ffn_conv_w[layer],
                    ffn_conv_b[layer], ffn_down[layer], final_norm_g if layer == depth - 1 else None,
                    seq_rows, tm_big, tf)
    return h.reshape(bsz, seq_rows, d)[:, PAD + N_META:, :]
```

```python
import functools
import math

import jax
import jax.numpy as jnp
from jax import lax
from jax.experimental import pallas as pl
from jax.experimental.pallas import tpu as pltpu

HEAD_DIM = 64
N_META = 16
BLOCK = 128
PAD = BLOCK - N_META
CONV_WIDTH = 3
GN_EPS = 64e-5
RMS_EPS = 1e-6

LANES = 128
CHUNK = 64
CONV_HALO = 16
SHIFT_HALO = 8
REC_SEQS = 4
ATTN_GROUPS = 8
ATTN_Q = BLOCK
KEY_STEP = HEAD_DIM
ATTN_UNROLLED_STEPS = 4
ATTN_KV_BUFFERS = 1
LOG_W_FLOOR = -100.0
VMEM_LIMIT = 56 * 1024 * 1024

F32 = jnp.float32
BF16 = jnp.bfloat16


def _dot(a, b):
    return jnp.dot(a, b, preferred_element_type=F32)


def _dot_nt(a, b):
    return lax.dot_general(a, b, (((1,), (1,)), ((), ())), preferred_element_type=F32)


def _dot_tn(a, b):
    return lax.dot_general(a, b, (((0,), (0,)), ((), ())), preferred_element_type=F32)


def _split_bf16(x):
    hi = x.astype(BF16)
    return hi, (x - hi.astype(F32)).astype(BF16)


def _rms_hat(x):
    return x * lax.rsqrt(jnp.mean(x * x, axis=-1, keepdims=True) + RMS_EPS)


def _params(sem):
    return pltpu.CompilerParams(dimension_semantics=sem, vmem_limit_bytes=VMEM_LIMIT)


def _row_in_seq(tile_rows, tile_index, seq_rows):
    rows = tile_index * tile_rows + lax.broadcasted_iota(jnp.int32, (tile_rows, 1), 0)
    return rows % seq_rows


def _norm_proj_kernel(n_groups, x_ref, *refs):
    g_refs = refs[:n_groups]
    w_refs = refs[n_groups:2 * n_groups]
    o_refs = refs[2 * n_groups:]
    xhat = _rms_hat(x_ref[...])
    for g_ref, w_ref, o_ref in zip(g_refs, w_refs, o_refs):
        xn = (xhat * g_ref[...]).astype(BF16)
        o_ref[...] = _dot(xn, w_ref[...]).astype(o_ref.dtype)


def norm_proj(h, gains, weights, out_dtype, tm):
    m, d = h.shape
    n = len(gains)
    in_specs = [pl.BlockSpec((tm, d), lambda i: (i, 0))]
    in_specs += [pl.BlockSpec((1, d), lambda i: (0, 0)) for _ in gains]
    in_specs += [pl.BlockSpec(w.shape, lambda i: (0, 0)) for w in weights]
    out_specs = [pl.BlockSpec((tm, w.shape[1]), lambda i: (i, 0)) for w in weights]
    out_shape = [jax.ShapeDtypeStruct((m, w.shape[1]), out_dtype) for w in weights]
    return pl.pallas_call(
        functools.partial(_norm_proj_kernel, n),
        grid=(m // tm,),
        in_specs=in_specs, out_specs=out_specs, out_shape=out_shape,
        compiler_params=_params(("parallel",)),
        name="norm_proj",
    )(h, *[g.reshape(1, d) for g in gains], *weights)


def _rwkv_proj_kernel(has_vres, seq_rows, tm, *refs):
    if has_vres:
        (h_ref, halo_ref, gain_ref, mix_ref, vec_ref, wr_ref, wk_ref, wv_ref, w1_ref, w2_ref,
         a1_ref, a2_ref, g1_ref, g2_ref, v1_ref, v2_ref, vfirst_ref,
         r_ref, k_ref, v_ref, lw_ref, a_ref, g_ref) = refs
    else:
        (h_ref, halo_ref, gain_ref, mix_ref, vec_ref, wr_ref, wk_ref, wv_ref, w1_ref, w2_ref,
         a1_ref, a2_ref, g1_ref, g2_ref,
         r_ref, k_ref, v_ref, lw_ref, a_ref, g_ref) = refs
    i = pl.program_id(0)
    gain = gain_ref[...]
    hn = _rms_hat(h_ref[...]) * gain
    prev_last = (_rms_hat(halo_ref[...]) * gain)[SHIFT_HALO - 1:SHIFT_HALO, :]
    row = lax.broadcasted_iota(jnp.int32, (tm, 1), 0)
    shifted = jnp.where(row == 0, prev_last, pltpu.roll(hn, 1, 0))
    shifted = jnp.where(_row_in_seq(tm, i, seq_rows) == 0, 0.0, shifted)
    xx = shifted - hn

    def mixed(j):
        return (hn + xx * mix_ref[j:j + 1, :]).astype(BF16)

    w0, a0, v0 = vec_ref[0:1, :], vec_ref[1:2, :], vec_ref[2:3, :]
    r_ref[...] = _dot(mixed(0), wr_ref[...])
    t = jnp.tanh(_dot(mixed(1), w1_ref[...])).astype(BF16)
    w_in = w0 + _dot(t, w2_ref[...])
    lw_ref[...] = -math.exp(-0.5) * jax.nn.sigmoid(w_in)
    k_ref[...] = _dot(mixed(2), wk_ref[...])
    xv = mixed(3)
    v = _dot(xv, wv_ref[...])
    if has_vres:
        gate = jax.nn.sigmoid(v0 + _dot(_dot(xv, v1_ref[...]).astype(BF16), v2_ref[...]))
        v = v + (vfirst_ref[...] - v) * gate
    v_ref[...] = v
    a_ref[...] = jax.nn.sigmoid(a0 + _dot(_dot(mixed(4), a1_ref[...]).astype(BF16), a2_ref[...]))
    g_ref[...] = _dot(jax.nn.sigmoid(_dot(mixed(5), g1_ref[...])).astype(BF16), g2_ref[...])


def _pad_lora(w_in, w_out):
    hid = w_in.shape[1]
    hid_p = -(-hid // LANES) * LANES
    return (jnp.pad(w_in, ((0, 0), (0, hid_p - hid))).astype(BF16),
            jnp.pad(w_out, ((0, hid_p - hid), (0, 0))).astype(BF16))


def rwkv_proj(h, gain, mix, w0, a0, v0, wr, wk, wv, w1, w2, a1, a2, g1, g2, v1, v2, v_first,
              seq_rows, tm):
    m, d = h.shape
    has_vres = v_first is not None
    mix_p = jnp.pad(mix, ((0, 8 - mix.shape[0]), (0, 0)))
    vecs = jnp.stack([w0, a0, v0 if has_vres else jnp.zeros_like(w0)])
    vecs = jnp.pad(vecs, ((0, 8 - vecs.shape[0]), (0, 0)))
    w1p, w2p = _pad_lora(w1, w2)
    a1p, a2p = _pad_lora(a1, a2)
    g1p, g2p = _pad_lora(g1, g2)
    full = lambda a: pl.BlockSpec(a.shape, lambda i: (0, 0), pipeline_mode=pl.Buffered(1))
    tile = pl.BlockSpec((tm, d), lambda i: (i, 0))
    halo = pl.BlockSpec((SHIFT_HALO, d), lambda i: (jnp.maximum(i * (tm // SHIFT_HALO) - 1, 0), 0))
    args = [h, h, gain.reshape(1, d), mix_p, vecs, wr.astype(BF16), wk.astype(BF16), wv.astype(BF16),
            w1p, w2p, a1p, a2p, g1p, g2p]
    in_specs = [tile, halo] + [full(a) for a in args[2:]]
    if has_vres:
        v1p, v2p = _pad_lora(v1, v2)
        args += [v1p, v2p, v_first]
        in_specs += [full(v1p), full(v2p), tile]
    return pl.pallas_call(
        functools.partial(_rwkv_proj_kernel, has_vres, seq_rows, tm),
        grid=(m // tm,),
        in_specs=in_specs,
        out_specs=[tile] * 6,
        out_shape=[jax.ShapeDtypeStruct((m, d), F32)] * 6,
        compiler_params=_params(("parallel",)),
        name="rwkv_proj",
    )(*args)


def _rwkv_rec_kernel(r_ref, k_ref, v_ref, lw_ref, a_ref, kk_ref, ka_ref, rk_ref, lg_ref, lb_ref,
                     y_ref, s_ref):
    c = CHUNK
    n_seq = r_ref.shape[0]
    n_pairs = r_ref.shape[2] // LANES

    @pl.when(pl.program_id(1) == 0)
    def _():
        s_ref[...] = jnp.zeros_like(s_ref)

    assert c == HEAD_DIM, "a head's chunk-by-chunk matrices fill exactly its lanes"
    lane = lax.broadcasted_iota(jnp.int32, (c, LANES), 1)
    row = lax.broadcasted_iota(jnp.int32, (c, LANES), 0)
    head0 = lane < HEAD_DIM
    strict_s = row > lane % c
    incl_s = row >= lane % c
    eye_s = (row == lane % c).astype(F32)
    row_l = lax.broadcasted_iota(jnp.int32, (LANES, LANES), 0)
    col_l = lax.broadcasted_iota(jnp.int32, (LANES, LANES), 1)
    same_head = (row_l // HEAD_DIM) == (col_l // HEAD_DIM)
    ones_bd2 = jnp.concatenate([same_head, same_head], axis=0).astype(BF16)
    row_t = lax.broadcasted_iota(jnp.int32, (c, 2 * c), 0)
    col_t = lax.broadcasted_iota(jnp.int32, (c, 2 * c), 1)
    tri2 = (row_t >= col_t % c).astype(BF16)

    def head_sums(xs):
        x = jnp.concatenate(xs, axis=0)
        out = _dot(jnp.concatenate(_split_bf16(x), axis=1), ones_bd2)
        return [out[i * c:(i + 1) * c] for i in range(len(xs))]

    def stack(x):
        x = x.astype(BF16)
        zero = jnp.zeros_like(x)
        return jnp.concatenate([jnp.where(head0, x, zero), jnp.where(head0, zero, x)], axis=0)

    cum_all = [_dot(tri2, jnp.concatenate(_split_bf16(lw_ref[q]), axis=0)) for q in range(n_seq)]

    pairs = range(n_seq * n_pairs)
    seqs = [p // n_pairs for p in pairs]
    sls = [slice((p % n_pairs) * LANES, (p % n_pairs + 1) * LANES) for p in pairs]
    ins_ = [(r_ref[q, :, sl], k_ref[q, :, sl], v_ref[q, :, sl], lw_ref[q, :, sl], a_ref[q, :, sl])
            for q, sl in zip(seqs, sls)]
    kku_ = [k_raw * kk_ref[:, sl] for (_, k_raw, _, _, _), sl in zip(ins_, sls)]
    kk_sq_ = head_sums([kk * kk for kk in kku_])
    r_, v_, k_, atrt_, bhkh_, pe_, aa_ = [], [], [], [], [], [], []
    for (r, k_raw, v, lw, a_sig), q, sl, kk, kk_sq in zip(ins_, seqs, sls, kku_, kk_sq_):
        cum = cum_all[q][:, sl]
        kk = kk * lax.rsqrt(jnp.maximum(kk_sq, 1e-24))
        k = k_raw * (1.0 + (a_sig - 1.0) * ka_ref[:, sl])
        b_vec = kk * a_sig
        cum_last = cum[c - 1:c, :]
        inv_p = jnp.exp(-cum)
        to_end = jnp.exp(cum_last - cum)
        atrt = jnp.concatenate([-kk * jnp.exp(cum - lw), r * jnp.exp(cum)], axis=0).astype(BF16)
        aa_.append(_dot_nt(atrt, jnp.concatenate([stack(b_vec * inv_p), stack(k * inv_p)], axis=0)))
        r_.append(r); v_.append(v); k_.append(k); atrt_.append(atrt)
        bhkh_.append(jnp.concatenate([b_vec * to_end, k * to_end], axis=0).astype(BF16))
        pe_.append(jnp.exp(cum_last))

    a_ab_ = [jnp.where(strict_s, aa[:c, :LANES], 0.0) for aa in aa_]
    a_ak_ = [jnp.where(strict_s, aa[:c, LANES:], 0.0) for aa in aa_]
    a_r_ = [jnp.concatenate([jnp.where(incl_s, aa[c:, :LANES], 0.0), jnp.where(incl_s, aa[c:, LANES:], 0.0)], axis=1)
            for aa in aa_]

    t_inv_ = [eye_s + a for a in a_ab_]
    pw_ = [_dot(a.astype(BF16), stack(a)) for a in a_ab_]
    n_steps = c.bit_length() - 2
    for step in range(n_steps):
        if step < n_steps - 1:
            tp_ = [_dot(jnp.concatenate([t, pw], axis=0).astype(BF16), stack(pw)) for t, pw in zip(t_inv_, pw_)]
            t_inv_ = [t + tp[:c] for t, tp in zip(t_inv_, tp_)]
            pw_ = [tp[c:] for tp in tp_]
        else:
            t_inv_ = [t + _dot(t.astype(BF16), stack(pw)) for t, pw in zip(t_inv_, pw_)]

    vst_ = [stack(v) for v in v_]
    akv_ = [_dot(a_ak.astype(BF16), vst) for a_ak, vst in zip(a_ak_, vst_)]
    s0_ = [s_ref[p] for p in pairs]
    ars_ = [_dot_nt(atrt, s0.astype(BF16)) for atrt, s0 in zip(atrt_, s0_)]
    u_ = [_dot(t.astype(BF16), stack(ars[:c] + akv)) for t, ars, akv in zip(t_inv_, ars_, akv_)]
    y_ = [ars[c:] + _dot(a_r.astype(BF16), jnp.concatenate([stack(u), vst], axis=0))
          for ars, a_r, u, vst in zip(ars_, a_r_, u_, vst_)]
    upd_ = [_dot_tn(jnp.concatenate([u, v], axis=0).astype(BF16), bhkh) for u, v, bhkh in zip(u_, v_, bhkh_)]
    for p in pairs:
        s_ref[p] = s0_[p] * pe_[p] + jnp.where(same_head, upd_[p], 0.0)

    inv_n = 1.0 / HEAD_DIM
    dev_ = [y - mu * inv_n for y, mu in zip(y_, head_sums(y_))]
    var_ = [ss * inv_n for ss in head_sums([dev * dev for dev in dev_])]
    rk_ = head_sums([r * k * rk_ref[:, sl] for r, k, sl in zip(r_, k_, sls)])
    for p, sl in enumerate(sls):
        y_ref[seqs[p], :, sl] = (dev_[p] * lax.rsqrt(var_[p] + GN_EPS) * lg_ref[:, sl] + lb_ref[:, sl]
                                 + rk_[p] * v_[p])


def rwkv_recurrence(r, k, v, lw, a, k_k, k_a, r_k, lnx_g, lnx_b, batch):
    m, d = r.shape
    seq_rows = m // batch
    n_seq = math.gcd(batch, REC_SEQS)
    tile = pl.BlockSpec((n_seq, CHUNK, d), lambda b, c: (b, c, 0))
    vec = pl.BlockSpec((1, d), lambda b, c: (0, 0))
    vecs = [t.reshape(1, d) for t in (k_k, k_a, r_k, lnx_g, lnx_b)]
    seqs = [t.reshape(batch, seq_rows, d) for t in (r, k, v, lw, a)]
    return pl.pallas_call(
        _rwkv_rec_kernel,
        grid=(batch // n_seq, seq_rows // CHUNK),
        in_specs=[tile] * 5 + [vec] * 5,
        out_specs=tile,
        out_shape=jax.ShapeDtypeStruct((batch, seq_rows, d), F32),
        scratch_shapes=[pltpu.VMEM((n_seq * (d // LANES), LANES, LANES), F32)],
        compiler_params=_params(("parallel", "arbitrary")),
        name="rwkv_recurrence",
    )(*seqs, *vecs).reshape(m, d)


def _mix_ffn_kernel(has_gate, has_final, seq_rows, tm, tf, *refs):
    refs = list(refs)
    y_ref = refs.pop(0)
    g_ref = refs.pop(0) if has_gate else None
    h_ref = refs.pop(0)
    yh_ref = refs.pop(0)
    gh_ref = refs.pop(0) if has_gate else None
    hh_ref, wo_ref, gain_ref, wup_ref, cw_ref, cb_ref, wd_ref = refs[:7]
    fgain_ref = refs[7] if has_final else None
    o_ref, hmid_ref, xn_ref, act_ref = refs[-4:]
    i = pl.program_id(0)
    d_ff = wd_ref.shape[0]
    gain = gain_ref[...]

    def mixed_in(y_r, g_r, h_r, first_row):
        y = y_r[...] if g_r is None else y_r[...] * g_r[...]
        rows = h_r.shape[0]
        row = (first_row + lax.broadcasted_iota(jnp.int32, (rows, 1), 0)) % seq_rows
        return jnp.where(row >= PAD, h_r[...] + _dot(y.astype(BF16), wo_ref[...]), 0.0)

    halo = mixed_in(yh_ref, gh_ref, hh_ref, i * tm - CONV_HALO + seq_rows)
    hmid_ref[...] = mixed_in(y_ref, g_ref, h_ref, i * tm)
    xn_ref[0:CONV_HALO, :] = (_rms_hat(halo) * gain).astype(BF16)
    xn_ref[CONV_HALO:, :] = (_rms_hat(hmid_ref[...]) * gain).astype(BF16)
    xn = xn_ref[...]

    def conv(col):
        u = _dot(xn, wup_ref[:, col:col + tf])
        out = cb_ref[:, col:col + tf] + u[CONV_HALO:, :] * cw_ref[CONV_WIDTH - 1:CONV_WIDTH, col:col + tf]
        for tap in range(CONV_WIDTH - 1):
            back = CONV_WIDTH - 1 - tap
            out = out + pltpu.roll(u, back, 0)[CONV_HALO:, :] * cw_ref[tap:tap + 1, col:col + tf]
        return out

    for col in range(0, d_ff, tf):
        act_ref[:, col:col + tf] = (jax.nn.silu(conv(col)) * conv(d_ff + col)).astype(BF16)

    keep = _row_in_seq(tm, i, seq_rows) >= PAD
    h_out = jnp.where(keep, hmid_ref[...] + _dot(act_ref[...], wd_ref[...]), 0.0)
    o_ref[...] = h_out if fgain_ref is None else _rms_hat(h_out) * fgain_ref[...]


def mix_ffn(y, gate, h, w_o, gain, w_up, conv_w, conv_b, w_down, final_gain, seq_rows, tm, tf):
    m, d = h.shape
    d_ff = w_down.shape[0]
    cw = jnp.pad(conv_w, ((0, 8 - CONV_WIDTH), (0, 0)))
    cb = conv_b.reshape(1, 2 * d_ff)
    tile = pl.BlockSpec((tm, d), lambda i: (i, 0))
    halo = pl.BlockSpec((CONV_HALO, d), lambda i: (jnp.maximum(i * (tm // CONV_HALO) - 1, 0), 0))
    const = lambda shape: pl.BlockSpec(shape, lambda i: (0, 0), pipeline_mode=pl.Buffered(1))
    seqs = [y] + ([gate] if gate is not None else []) + [h]
    finals = [] if final_gain is None else [final_gain.reshape(1, d)]
    return pl.pallas_call(
        functools.partial(_mix_ffn_kernel, gate is not None, final_gain is not None, seq_rows, tm, tf),
        grid=(m // tm,),
        in_specs=[tile] * len(seqs) + [halo] * len(seqs) + [
            const((d, d)), const((1, d)), const((d, 2 * d_ff)), const((8, 2 * d_ff)), const((1, 2 * d_ff)),
            const((d_ff, d))] + [const((1, d))] * len(finals),
        out_specs=tile,
        out_shape=jax.ShapeDtypeStruct((m, d), F32),
        scratch_shapes=[pltpu.VMEM((tm, d), F32), pltpu.VMEM((tm + CONV_HALO, d), BF16),
                        pltpu.VMEM((tm, d_ff), BF16)],
        compiler_params=_params(("parallel",)),
        name="mix_ffn",
    )(*seqs, *seqs, w_o.astype(BF16), gain.reshape(1, d), w_up.astype(BF16), cw, cb, w_down.astype(BF16), *finals)


def _sb_attn_kernel(q_ref, k_ref, v_ref, o_ref):
    qi = pl.program_id(2)
    qb = q_ref.shape[0]
    n_groups = q_ref.shape[1] // LANES
    scale = HEAD_DIM ** -0.5
    assert math.frexp(scale)[0] == 0.5, "the score scale is folded into bf16 q, which needs a power of two"
    qs = [q_ref[:, g * LANES:(g + 1) * LANES] * jnp.asarray(scale, BF16) for g in range(n_groups)]
    head0_k = lax.broadcasted_iota(jnp.int32, (KEY_STEP, LANES), 1) < HEAD_DIM
    q_pos = qi * qb + lax.broadcasted_iota(jnp.int32, (qb, LANES), 0)
    row_b = lax.broadcasted_iota(jnp.int32, (2 * LANES, 2 * LANES), 0) % LANES
    col_b = lax.broadcasted_iota(jnp.int32, (2 * LANES, 2 * LANES), 1)
    same_head = (row_b // KEY_STEP) == ((col_b % LANES) // KEY_STEP)
    sum_mat = jnp.where(same_head & ((row_b > col_b) | (col_b >= LANES)), 1.0, 0.0).astype(BF16)

    def stack(x):
        zero = jnp.zeros_like(x)
        return jnp.concatenate([jnp.where(head0_k, x, zero), jnp.where(head0_k, zero, x)], axis=0)

    def key_rows(ref, j, g):
        start = pl.multiple_of(jnp.maximum(j, 0) * KEY_STEP, KEY_STEP)
        return stack(ref[pl.ds(start, KEY_STEP), g * LANES:(g + 1) * LANES])

    def valid_keys(j, pos):
        k_pos = j * KEY_STEP + lax.broadcasted_iota(jnp.int32, pos.shape, 1) % KEY_STEP
        return (k_pos < pos) & (k_pos >= PAD)

    def scores(zs, valids):
        rows = zs[0].shape[0]
        log_betas = [jnp.minimum(z, 0.0) - jnp.log(1.0 + jnp.exp(-jnp.abs(z))) for z in zs]
        splits = [jnp.concatenate(_split_bf16(jnp.where(ok, lb - z, 0.0)), axis=1)
                  for lb, z, ok in zip(log_betas, zs, valids)]
        sums = _dot(jnp.concatenate(splits, axis=0), sum_mat)
        return log_betas, [sums[i * rows:(i + 1) * rows] for i in range(len(zs))]

    n_sub = qb // KEY_STEP
    sub_rows = [slice(sb * KEY_STEP, (sb + 1) * KEY_STEP) for sb in range(n_sub)]
    sub_steps = [[qi * n_sub + sb - t for t in range(ATTN_UNROLLED_STEPS)] for sb in range(n_sub)]
    sub_pos = [qi * qb + sb * KEY_STEP + lax.broadcasted_iota(jnp.int32, (KEY_STEP, LANES), 0) for sb in range(n_sub)]
    valids = [[valid_keys(j, pos) for j in steps] for pos, steps in zip(sub_pos, sub_steps)]
    window = [qi * n_sub + n_sub - 1 - t for t in range(ATTN_UNROLLED_STEPS + n_sub - 1)]
    zs, tile_valid = [], []
    for g in range(n_groups):
        k_tiles = [key_rows(k_ref, j, g) for j in window]
        for sb in range(n_sub):
            lo = n_sub - 1 - sb
            keys = jnp.concatenate(k_tiles[lo:lo + ATTN_UNROLLED_STEPS], axis=0)
            z_all = _dot_nt(qs[g][sub_rows[sb]], keys)
            zs += [z_all[:, t * LANES:(t + 1) * LANES] for t in range(ATTN_UNROLLED_STEPS)]
            tile_valid += valids[sb]
    log_betas, sums = scores(zs, tile_valid)
    carries, accs = [], []
    for g in range(n_groups):
        v_tiles = [key_rows(v_ref, j, g) for j in window]
        carry_g, acc_g = [], []
        for sb in range(n_sub):
            carry = jnp.zeros((KEY_STEP, LANES), F32)
            ws = []
            for t in range(ATTN_UNROLLED_STEPS):
                i = (g * n_sub + sb) * ATTN_UNROLLED_STEPS + t
                ws.append(jnp.where(valids[sb][t], jnp.exp(log_betas[i] + sums[i][:, :LANES] + carry), 0.0)
                          .astype(BF16))
                carry = carry + sums[i][:, LANES:]
            lo = n_sub - 1 - sb
            values = jnp.concatenate(v_tiles[lo:lo + ATTN_UNROLLED_STEPS], axis=0)
            acc_g.append(_dot(jnp.concatenate(ws, axis=1), values))
            carry_g.append(carry)
        accs.append(jnp.concatenate(acc_g, axis=0))
        carries.append(jnp.concatenate(carry_g, axis=0))

    def alive(carries):
        top = carries[0]
        for c in carries[1:]:
            top = jnp.maximum(top, c)
        return (jnp.max(top) > LOG_W_FLOOR).astype(jnp.int32)

    def cond(state):
        j, live, _, _ = state
        return (j >= PAD // KEY_STEP) & (live > 0)

    def body(state):
        j, _, carries, accs = state
        ok = valid_keys(j, q_pos) & (j <= q_pos // KEY_STEP - ATTN_UNROLLED_STEPS)
        zs = [_dot_nt(qs[g], key_rows(k_ref, j, g)) for g in range(n_groups)]
        log_betas, sums = scores(zs, [ok] * n_groups)
        ws = [jnp.where(ok, jnp.exp(lb + sm[:, :LANES] + c), 0.0) for lb, sm, c in zip(log_betas, sums, carries)]
        accs = tuple(acc + _dot(w.astype(BF16), key_rows(v_ref, j, g)) for g, (acc, w) in enumerate(zip(accs, ws)))
        carries = tuple(c + sm[:, LANES:] for c, sm in zip(carries, sums))
        return j - 1, alive(carries), carries, accs

    rest = qi * n_sub + n_sub - 1 - ATTN_UNROLLED_STEPS
    state = lax.while_loop(cond, body, (rest, alive(carries), tuple(carries), tuple(accs)))
    for g, acc in enumerate(state[-1]):
        o_ref[:, g * LANES:(g + 1) * LANES] = acc.astype(o_ref.dtype)


def sb_attention(q, k, v, batch, groups):
    m, d = q.shape
    seq_rows = m // batch
    nq = seq_rows // ATTN_Q
    width = groups * LANES
    whole_seq = pl.BlockSpec((seq_rows, width), lambda b, p, i: (b, p), pipeline_mode=pl.Buffered(ATTN_KV_BUFFERS))
    return pl.pallas_call(
        _sb_attn_kernel,
        grid=(batch, d // width, nq),
        in_specs=[pl.BlockSpec((ATTN_Q, width), lambda b, p, i: (b * nq + i, p)), whole_seq, whole_seq],
        out_specs=pl.BlockSpec((ATTN_Q, width), lambda b, p, i: (b * nq + i, p)),
        out_shape=jax.ShapeDtypeStruct((m, d), BF16),
        compiler_params=_params(("parallel", "parallel", "arbitrary")),
        name="sb_attention",
    )(q, k, v)


def _row_tile(m, want):
    tm = want
    while m % tm:
        tm //= 2
    return tm


def kernel(x, meta_tokens, norm_mix_g, norm_ffn_g, ffn_up, ffn_conv_w, ffn_conv_b, ffn_down, rw_mix, rw_wr, rw_wk, rw_wv, rw_wo, rw_w0, rw_w1, rw_w2, rw_a0, rw_a1, rw_a2, rw_g1, rw_g2, rw_kk, rw_ka, rw_rk, rw_lnx_g, rw_lnx_b, rw_v0, rw_v1, rw_v2, kv_norm_g, sb_wk, sb_wv, sb_wq, sb_wo, final_norm_g):
    bsz, seq, d = x.shape
    depth = norm_mix_g.shape[0]
    n_a = rw_wr.shape[0]
    d_ff = ffn_down.shape[1]
    seq_rows = PAD + N_META + seq
    assert seq % BLOCK == 0 and d % LANES == 0 and seq_rows % CHUNK == 0
    m = bsz * seq_rows
    tm_big = _row_tile(m, 512)
    tm_proj = _row_tile(m, 512)
    tf = 256 if d_ff % 256 == 0 else LANES

    meta = jnp.broadcast_to(meta_tokens.astype(x.dtype)[None], (bsz, N_META, d))
    h = jnp.concatenate([jnp.zeros((bsz, PAD, d), x.dtype), meta, x], axis=1).reshape(m, d)

    v_first = None
    k_sh = v_sh = None
    for layer in range(depth):
        if layer < n_a:
            i = layer
            vres = (None, None, None) if i == 0 else (rw_v0[i - 1], rw_v1[i - 1], rw_v2[i - 1])
            r, k, v, lw, a, g = rwkv_proj(
                h, norm_mix_g[layer], rw_mix[i], rw_w0[i], rw_a0[i], vres[0], rw_wr[i], rw_wk[i], rw_wv[i],
                rw_w1[i], rw_w2[i], rw_a1[i], rw_a2[i], rw_g1[i], rw_g2[i], vres[1], vres[2],
                None if i == 0 else v_first, seq_rows, tm_proj)
            if i == 0:
                v_first = v
            y = rwkv_recurrence(r, k, v, lw, a, rw_kk[i], rw_ka[i], rw_rk[i].reshape(-1),
                                rw_lnx_g[i], rw_lnx_b[i], bsz)
            mixer = (y, g, rw_wo[i])
        else:
            j = layer - n_a
            if layer == n_a:
                q, k_sh, v_sh = norm_proj(
                    h, [norm_mix_g[layer], kv_norm_g, kv_norm_g],
                    [sb_wq[j].astype(BF16), sb_wk.astype(BF16), sb_wv.astype(BF16)], BF16, tm_big)
            else:
                (q,) = norm_proj(h, [norm_mix_g[layer]], [sb_wq[j].astype(BF16)], BF16, tm_big)
            o = sb_attention(q, k_sh, v_sh, bsz, ATTN_GROUPS if (d // LANES) % ATTN_GROUPS == 0 else 1)
            mixer = (o, None, sb_wo[j])
        h = mix_ffn(mixer[0], mixer[1], h, mixer[2], norm_ffn_g[layer], ffn_up[layer], ffn_conv_w[layer],
                    ffn_conv_b[layer], ffn_down[layer], final_norm_g if layer == depth - 1 else None,
                    seq_rows, tm_big, tf)
    return h.reshape(bsz, seq_rows, d)[:, PAD + N_META:, :]
```

```python
import functools
import math

import jax
import jax.numpy as jnp
from jax import lax
from jax.experimental import pallas as pl
from jax.experimental.pallas import tpu as pltpu

HEAD_DIM = 64
N_META = 16
BLOCK = 128
PAD = BLOCK - N_META
CONV_WIDTH = 3
GN_EPS = 64e-5
RMS_EPS = 1e-6

LANES = 128
CHUNK = 64
CONV_HALO = 16
SHIFT_HALO = 8
REC_SEQS = 4
ATTN_GROUPS = 8
ATTN_Q = BLOCK
KEY_STEP = HEAD_DIM
ATTN_UNROLLED_STEPS = 4
ATTN_KV_BUFFERS = 1
LOG_W_FLOOR = -100.0
VMEM_LIMIT = 56 * 1024 * 1024

F32 = jnp.float32
BF16 = jnp.bfloat16


def _dot(a, b):
    return jnp.dot(a, b, preferred_element_type=F32)


def _dot_nt(a, b):
    return lax.dot_general(a, b, (((1,), (1,)), ((), ())), preferred_element_type=F32)


def _dot_tn(a, b):
    return lax.dot_general(a, b, (((0,), (0,)), ((), ())), preferred_element_type=F32)


def _split_bf16(x):
    hi = x.astype(BF16)
    return hi, (x - hi.astype(F32)).astype(BF16)


def _rms_hat(x):
    return x * lax.rsqrt(jnp.mean(x * x, axis=-1, keepdims=True) + RMS_EPS)


def _params(sem):
    return pltpu.CompilerParams(dimension_semantics=sem, vmem_limit_bytes=VMEM_LIMIT)


def _row_in_seq(tile_rows, tile_index, seq_rows):
    rows = tile_index * tile_rows + lax.broadcasted_iota(jnp.int32, (tile_rows, 1), 0)
    return rows % seq_rows


def _norm_proj_kernel(n_groups, x_ref, *refs):
    g_refs = refs[:n_groups]
    w_refs = refs[n_groups:2 * n_groups]
    o_refs = refs[2 * n_groups:]
    xhat = _rms_hat(x_ref[...])
    for g_ref, w_ref, o_ref in zip(g_refs, w_refs, o_refs):
        xn = (xhat * g_ref[...]).astype(BF16)
        o_ref[...] = _dot(xn, w_ref[...]).astype(o_ref.dtype)


def norm_proj(h, gains, weights, out_dtype, tm):
    m, d = h.shape
    n = len(gains)
    in_specs = [pl.BlockSpec((tm, d), lambda i: (i, 0))]
    in_specs += [pl.BlockSpec((1, d), lambda i: (0, 0)) for _ in gains]
    in_specs += [pl.BlockSpec(w.shape, lambda i: (0, 0)) for w in weights]
    out_specs = [pl.BlockSpec((tm, w.shape[1]), lambda i: (i, 0)) for w in weights]
    out_shape = [jax.ShapeDtypeStruct((m, w.shape[1]), out_dtype) for w in weights]
    return pl.pallas_call(
        functools.partial(_norm_proj_kernel, n),
        grid=(m // tm,),
        in_specs=in_specs, out_specs=out_specs, out_shape=out_shape,
        compiler_params=_params(("parallel",)),
        name="norm_proj",
    )(h, *[g.reshape(1, d) for g in gains], *weights)


def _rwkv_proj_kernel(has_vres, seq_rows, tm, *refs):
    if has_vres:
        (h_ref, halo_ref, gain_ref, mix_ref, vec_ref, wr_ref, wk_ref, wv_ref, w1_ref, w2_ref,
         a1_ref, a2_ref, g1_ref, g2_ref, v1_ref, v2_ref, vfirst_ref,
         r_ref, k_ref, v_ref, lw_ref, a_ref, g_ref) = refs
    else:
        (h_ref, halo_ref, gain_ref, mix_ref, vec_ref, wr_ref, wk_ref, wv_ref, w1_ref, w2_ref,
         a1_ref, a2_ref, g1_ref, g2_ref,
         r_ref, k_ref, v_ref, lw_ref, a_ref, g_ref) = refs
    i = pl.program_id(0)
    gain = gain_ref[...]
    hn = _rms_hat(h_ref[...]) * gain
    prev_last = (_rms_hat(halo_ref[...]) * gain)[SHIFT_HALO - 1:SHIFT_HALO, :]
    row = lax.broadcasted_iota(jnp.int32, (tm, 1), 0)
    shifted = jnp.where(row == 0, prev_last, pltpu.roll(hn, 1, 0))
    shifted = jnp.where(_row_in_seq(tm, i, seq_rows) == 0, 0.0, shifted)
    xx = shifted - hn

    def mixed(j):
        return (hn + xx * mix_ref[j:j + 1, :]).astype(BF16)

    w0, a0, v0 = vec_ref[0:1, :], vec_ref[1:2, :], vec_ref[2:3, :]
    r_ref[...] = _dot(mixed(0), wr_ref[...])
    t = jnp.tanh(_dot(mixed(1), w1_ref[...])).astype(BF16)
    w_in = w0 + _dot(t, w2_ref[...])
    lw_ref[...] = -math.exp(-0.5) * jax.nn.sigmoid(w_in)
    k_ref[...] = _dot(mixed(2), wk_ref[...])
    xv = mixed(3)
    v = _dot(xv, wv_ref[...])
    if has_vres:
        gate = jax.nn.sigmoid(v0 + _dot(_dot(xv, v1_ref[...]).astype(BF16), v2_ref[...]))
        v = v + (vfirst_ref[...] - v) * gate
    v_ref[...] = v
    a_ref[...] = jax.nn.sigmoid(a0 + _dot(_dot(mixed(4), a1_ref[...]).astype(BF16), a2_ref[...]))
    g_ref[...] = _dot(jax.nn.sigmoid(_dot(mixed(5), g1_ref[...])).astype(BF16), g2_ref[...])


def _pad_lora(w_in, w_out):
    hid = w_in.shape[1]
    hid_p = -(-hid // LANES) * LANES
    return (jnp.pad(w_in, ((0, 0), (0, hid_p - hid))).astype(BF16),
            jnp.pad(w_out, ((0, hid_p - hid), (0, 0))).astype(BF16))


def rwkv_proj(h, gain, mix, w0, a0, v0, wr, wk, wv, w1, w2, a1, a2, g1, g2, v1, v2, v_first,
              seq_rows, tm):
    m, d = h.shape
    has_vres = v_first is not None
    mix_p = jnp.pad(mix, ((0, 8 - mix.shape[0]), (0, 0)))
    vecs = jnp.stack([w0, a0, v0 if has_vres else jnp.zeros_like(w0)])
    vecs = jnp.pad(vecs, ((0, 8 - vecs.shape[0]), (0, 0)))
    w1p, w2p = _pad_lora(w1, w2)
    a1p, a2p = _pad_lora(a1, a2)
    g1p, g2p = _pad_lora(g1, g2)
    full = lambda a: pl.BlockSpec(a.shape, lambda i: (0, 0), pipeline_mode=pl.Buffered(1))
    tile = pl.BlockSpec((tm, d), lambda i: (i, 0))
    halo = pl.BlockSpec((SHIFT_HALO, d), lambda i: (jnp.maximum(i * (tm // SHIFT_HALO) - 1, 0), 0))
    args = [h, h, gain.reshape(1, d), mix_p, vecs, wr.astype(BF16), wk.astype(BF16), wv.astype(BF16),
            w1p, w2p, a1p, a2p, g1p, g2p]
    in_specs = [tile, halo] + [full(a) for a in args[2:]]
    if has_vres:
        v1p, v2p = _pad_lora(v1, v2)
        args += [v1p, v2p, v_first]
        in_specs += [full(v1p), full(v2p), tile]
    return pl.pallas_call(
        functools.partial(_rwkv_proj_kernel, has_vres, seq_rows, tm),
        grid=(m // tm,),
        in_specs=in_specs,
        out_specs=[tile] * 6,
        out_shape=[jax.ShapeDtypeStruct((m, d), F32)] * 6,
        compiler_params=_params(("parallel",)),
        name="rwkv_proj",
    )(*args)


def _rwkv_rec_kernel(r_ref, k_ref, v_ref, lw_ref, a_ref, kk_ref, ka_ref, rk_ref, lg_ref, lb_ref,
                     y_ref, s_ref):
    c = CHUNK
    n_seq = r_ref.shape[0]
    n_pairs = r_ref.shape[2] // LANES

    @pl.when(pl.program_id(1) == 0)
    def _():
        s_ref[...] = jnp.zeros_like(s_ref)

    assert c == HEAD_DIM, "a head's chunk-by-chunk matrices fill exactly its lanes"
    lane = lax.broadcasted_iota(jnp.int32, (c, LANES), 1)
    row = lax.broadcasted_iota(jnp.int32, (c, LANES), 0)
    head0 = lane < HEAD_DIM
    strict_s = row > lane % c
    incl_s = row >= lane % c
    eye_s = (row == lane % c).astype(F32)
    row_l = lax.broadcasted_iota(jnp.int32, (LANES, LANES), 0)
    col_l = lax.broadcasted_iota(jnp.int32, (LANES, LANES), 1)
    same_head = (row_l // HEAD_DIM) == (col_l // HEAD_DIM)
    ones_bd2 = jnp.concatenate([same_head, same_head], axis=0).astype(BF16)
    row_t = lax.broadcasted_iota(jnp.int32, (c, 2 * c), 0)
    col_t = lax.broadcasted_iota(jnp.int32, (c, 2 * c), 1)
    tri2 = (row_t >= col_t % c).astype(BF16)

    def head_sums(xs):
        x = jnp.concatenate(xs, axis=0)
        out = _dot(jnp.concatenate(_split_bf16(x), axis=1), ones_bd2)
        return [out[i * c:(i + 1) * c] for i in range(len(xs))]

    def stack(x):
        x = x.astype(BF16)
        zero = jnp.zeros_like(x)
        return jnp.concatenate([jnp.where(head0, x, zero), jnp.where(head0, zero, x)], axis=0)

    cum_all = [_dot(tri2, jnp.concatenate(_split_bf16(lw_ref[q]), axis=0)) for q in range(n_seq)]

    pairs = range(n_seq * n_pairs)
    seqs = [p // n_pairs for p in pairs]
    sls = [slice((p % n_pairs) * LANES, (p % n_pairs + 1) * LANES) for p in pairs]
    ins_ = [(r_ref[q, :, sl], k_ref[q, :, sl], v_ref[q, :, sl], lw_ref[q, :, sl], a_ref[q, :, sl])
            for q, sl in zip(seqs, sls)]
    kku_ = [k_raw * kk_ref[:, sl] for (_, k_raw, _, _, _), sl in zip(ins_, sls)]
    kk_sq_ = head_sums([kk * kk for kk in kku_])
    r_, v_, k_, atrt_, bhkh_, pe_, aa_ = [], [], [], [], [], [], []
    for (r, k_raw, v, lw, a_sig), q, sl, kk, kk_sq in zip(ins_, seqs, sls, kku_, kk_sq_):
        cum = cum_all[q][:, sl]
        kk = kk * lax.rsqrt(jnp.maximum(kk_sq, 1e-24))
        k = k_raw * (1.0 + (a_sig - 1.0) * ka_ref[:, sl])
        b_vec = kk * a_sig
        cum_last = cum[c - 1:c, :]
        inv_p = jnp.exp(-cum)
        to_end = jnp.exp(cum_last - cum)
        atrt = jnp.concatenate([-kk * jnp.exp(cum - lw), r * jnp.exp(cum)], axis=0).astype(BF16)
        aa_.append(_dot_nt(atrt, jnp.concatenate([stack(b_vec * inv_p), stack(k * inv_p)], axis=0)))
        r_.append(r); v_.append(v); k_.append(k); atrt_.append(atrt)
        bhkh_.append(jnp.concatenate([b_vec * to_end, k * to_end], axis=0).astype(BF16))
        pe_.append(jnp.exp(cum_last))

    a_ab_ = [jnp.where(strict_s, aa[:c, :LANES], 0.0) for aa in aa_]
    a_ak_ = [jnp.where(strict_s, aa[:c, LANES:], 0.0) for aa in aa_]
    a_r_ = [jnp.concatenate([jnp.where(incl_s, aa[c:, :LANES], 0.0), jnp.where(incl_s, aa[c:, LANES:], 0.0)], axis=1)
            for aa in aa_]

    t_inv_ = [eye_s + a for a in a_ab_]
    pw_ = [_dot(a.astype(BF16), stack(a)) for a in a_ab_]
    n_steps = c.bit_length() - 2
    for step in range(n_steps):
        if step < n_steps - 1:
            tp_ = [_dot(jnp.concatenate([t, pw], axis=0).astype(BF16), stack(pw)) for t, pw in zip(t_inv_, pw_)]
            t_inv_ = [t + tp[:c] for t, tp in zip(t_inv_, tp_)]
            pw_ = [tp[c:] for tp in tp_]
        else:
            t_inv_ = [t + _dot(t.astype(BF16), stack(pw)) for t, pw in zip(t_inv_, pw_)]

    vst_ = [stack(v) for v in v_]
    akv_ = [_dot(a_ak.astype(BF16), vst) for a_ak, vst in zip(a_ak_, vst_)]
    s0_ = [s_ref[p] for p in pairs]
    ars_ = [_dot_nt(atrt, s0.astype(BF16)) for atrt, s0 in zip(atrt_, s0_)]
    u_ = [_dot(t.astype(BF16), stack(ars[:c] + akv)) for t, ars, akv in zip(t_inv_, ars_, akv_)]
    y_ = [ars[c:] + _dot(a_r.astype(BF16), jnp.concatenate([stack(u), vst], axis=0))
          for ars, a_r, u, vst in zip(ars_, a_r_, u_, vst_)]
    upd_ = [_dot_tn(jnp.concatenate([u, v], axis=0).astype(BF16), bhkh) for u, v, bhkh in zip(u_, v_, bhkh_)]
    for p in pairs:
        s_ref[p] = s0_[p] * pe_[p] + jnp.where(same_head, upd_[p], 0.0)

    inv_n = 1.0 / HEAD_DIM
    dev_ = [y - mu * inv_n for y, mu in zip(y_, head_sums(y_))]
    var_ = [ss * inv_n for ss in head_sums([dev * dev for dev in dev_])]
    rk_ = head_sums([r * k * rk_ref[:, sl] for r, k, sl in zip(r_, k_, sls)])
    for p, sl in enumerate(sls):
        y_ref[seqs[p], :, sl] = (dev_[p] * lax.rsqrt(var_[p] + GN_EPS) * lg_ref[:, sl] + lb_ref[:, sl]
                                 + rk_[p] * v_[p])


def rwkv_recurrence(r, k, v, lw, a, k_k, k_a, r_k, lnx_g, lnx_b, batch):
    m, d = r.shape
    seq_rows = m // batch
    n_seq = math.gcd(batch, REC_SEQS)
    tile = pl.BlockSpec((n_seq, CHUNK, d), lambda b, c: (b, c, 0))
    vec = pl.BlockSpec((1, d), lambda b, c: (0, 0))
    vecs = [t.reshape(1, d) for t in (k_k, k_a, r_k, lnx_g, lnx_b)]
    seqs = [t.reshape(batch, seq_rows, d) for t in (r, k, v, lw, a)]
    return pl.pallas_call(
        _rwkv_rec_kernel,
        grid=(batch // n_seq, seq_rows // CHUNK),
        in_specs=[tile] * 5 + [vec] * 5,
        out_specs=tile,
        out_shape=jax.ShapeDtypeStruct((batch, seq_rows, d), F32),
        scratch_shapes=[pltpu.VMEM((n_seq * (d // LANES), LANES, LANES), F32)],
        compiler_params=_params(("parallel", "arbitrary")),
        name="rwkv_recurrence",
    )(*seqs, *vecs).reshape(m, d)


def _mix_ffn_kernel(has_gate, has_final, seq_rows, tm, tf, *refs):
    refs = list(refs)
    y_ref = refs.pop(0)
    g_ref = refs.pop(0) if has_gate else None
    h_ref, wo_ref, gain_ref, wup_ref, cw_ref, cb_ref, wd_ref = refs[:7]
    fgain_ref = refs[7] if has_final else None
    o_ref, hmid_ref, xn_ref, act_ref = refs[-4:]
    i = pl.program_id(0)
    d_ff = wd_ref.shape[0]
    keep = _row_in_seq(tm, i, seq_rows) >= PAD

    @pl.when(i == 0)
    def _():
        xn_ref[0:CONV_HALO, :] = jnp.zeros((CONV_HALO, xn_ref.shape[1]), BF16)

    @pl.when(i > 0)
    def _():
        xn_ref[0:CONV_HALO, :] = xn_ref[tm:tm + CONV_HALO, :]

    y = y_ref[...] if g_ref is None else y_ref[...] * g_ref[...]
    hmid_ref[...] = jnp.where(keep, h_ref[...] + _dot(y.astype(BF16), wo_ref[...]), 0.0)
    xn_ref[CONV_HALO:, :] = (_rms_hat(hmid_ref[...]) * gain_ref[...]).astype(BF16)
    xn = xn_ref[...]

    def conv(col):
        u = _dot(xn, wup_ref[:, col:col + tf])
        out = cb_ref[:, col:col + tf] + u[CONV_HALO:, :] * cw_ref[CONV_WIDTH - 1:CONV_WIDTH, col:col + tf]
        for tap in range(CONV_WIDTH - 1):
            back = CONV_WIDTH - 1 - tap
            out = out + pltpu.roll(u, back, 0)[CONV_HALO:, :] * cw_ref[tap:tap + 1, col:col + tf]
        return out

    for col in range(0, d_ff, tf):
        act_ref[:, col:col + tf] = (jax.nn.silu(conv(col)) * conv(d_ff + col)).astype(BF16)

    h_out = jnp.where(keep, hmid_ref[...] + _dot(act_ref[...], wd_ref[...]), 0.0)
    o_ref[...] = h_out if fgain_ref is None else _rms_hat(h_out) * fgain_ref[...]


def mix_ffn(y, gate, h, w_o, gain, w_up, conv_w, conv_b, w_down, final_gain, seq_rows, tm, tf):
    m, d = h.shape
    d_ff = w_down.shape[0]
    cw = jnp.pad(conv_w, ((0, 8 - CONV_WIDTH), (0, 0)))
    cb = conv_b.reshape(1, 2 * d_ff)
    tile = pl.BlockSpec((tm, d), lambda i: (i, 0))
    const = lambda shape: pl.BlockSpec(shape, lambda i: (0, 0), pipeline_mode=pl.Buffered(1))
    seqs = [y] + ([gate] if gate is not None else []) + [h]
    finals = [] if final_gain is None else [final_gain.reshape(1, d)]
    return pl.pallas_call(
        functools.partial(_mix_ffn_kernel, gate is not None, final_gain is not None, seq_rows, tm, tf),
        grid=(m // tm,),
        in_specs=[tile] * len(seqs) + [
            const((d, d)), const((1, d)), const((d, 2 * d_ff)), const((8, 2 * d_ff)), const((1, 2 * d_ff)),
            const((d_ff, d))] + [const((1, d))] * len(finals),
        out_specs=tile,
        out_shape=jax.ShapeDtypeStruct((m, d), F32),
        scratch_shapes=[pltpu.VMEM((tm, d), F32), pltpu.VMEM((tm + CONV_HALO, d), BF16),
                        pltpu.VMEM((tm, d_ff), BF16)],
        compiler_params=_params(("arbitrary",)),
        name="mix_ffn",
    )(*seqs, w_o.astype(BF16), gain.reshape(1, d), w_up.astype(BF16), cw, cb, w_down.astype(BF16), *finals)


def _sb_attn_kernel(q_ref, k_ref, v_ref, o_ref):
    qi = pl.program_id(2)
    qb = q_ref.shape[0]
    n_groups = q_ref.shape[1] // LANES
    scale = HEAD_DIM ** -0.5
    assert math.frexp(scale)[0] == 0.5, "the score scale is folded into bf16 q, which needs a power of two"
    qs = [q_ref[:, g * LANES:(g + 1) * LANES] * jnp.asarray(scale, BF16) for g in range(n_groups)]
    head0_k = lax.broadcasted_iota(jnp.int32, (KEY_STEP, LANES), 1) < HEAD_DIM
    q_pos = qi * qb + lax.broadcasted_iota(jnp.int32, (qb, LANES), 0)
    row_b = lax.broadcasted_iota(jnp.int32, (2 * LANES, 2 * LANES), 0) % LANES
    col_b = lax.broadcasted_iota(jnp.int32, (2 * LANES, 2 * LANES), 1)
    same_head = (row_b // KEY_STEP) == ((col_b % LANES) // KEY_STEP)
    sum_mat = jnp.where(same_head & ((row_b > col_b) | (col_b >= LANES)), 1.0, 0.0).astype(BF16)

    def stack(x):
        zero = jnp.zeros_like(x)
        return jnp.concatenate([jnp.where(head0_k, x, zero), jnp.where(head0_k, zero, x)], axis=0)

    def key_rows(ref, j, g):
        start = pl.multiple_of(jnp.maximum(j, 0) * KEY_STEP, KEY_STEP)
        return stack(ref[pl.ds(start, KEY_STEP), g * LANES:(g + 1) * LANES])

    def valid_keys(j, pos):
        k_pos = j * KEY_STEP + lax.broadcasted_iota(jnp.int32, pos.shape, 1) % KEY_STEP
        return (k_pos < pos) & (k_pos >= PAD)

    def scores(zs, valids):
        rows = zs[0].shape[0]
        log_betas = [jnp.minimum(z, 0.0) - jnp.log(1.0 + jnp.exp(-jnp.abs(z))) for z in zs]
        splits = [jnp.concatenate(_split_bf16(jnp.where(ok, lb - z, 0.0)), axis=1)
                  for lb, z, ok in zip(log_betas, zs, valids)]
        sums = _dot(jnp.concatenate(splits, axis=0), sum_mat)
        return log_betas, [sums[i * rows:(i + 1) * rows] for i in range(len(zs))]

    n_sub = qb // KEY_STEP
    sub_rows = [slice(sb * KEY_STEP, (sb + 1) * KEY_STEP) for sb in range(n_sub)]
    sub_steps = [[qi * n_sub + sb - t for t in range(ATTN_UNROLLED_STEPS)] for sb in range(n_sub)]
    sub_pos = [qi * qb + sb * KEY_STEP + lax.broadcasted_iota(jnp.int32, (KEY_STEP, LANES), 0) for sb in range(n_sub)]
    valids = [[valid_keys(j, pos) for j in steps] for pos, steps in zip(sub_pos, sub_steps)]
    window = [qi * n_sub + n_sub - 1 - t for t in range(ATTN_UNROLLED_STEPS + n_sub - 1)]
    zs, tile_valid = [], []
    for g in range(n_groups):
        k_tiles = [key_rows(k_ref, j, g) for j in window]
        for sb in range(n_sub):
            lo = n_sub - 1 - sb
            keys = jnp.concatenate(k_tiles[lo:lo + ATTN_UNROLLED_STEPS], axis=0)
            z_all = _dot_nt(qs[g][sub_rows[sb]], keys)
            zs += [z_all[:, t * LANES:(t + 1) * LANES] for t in range(ATTN_UNROLLED_STEPS)]
            tile_valid += valids[sb]
    log_betas, sums = scores(zs, tile_valid)
    carries, accs = [], []
    for g in range(n_groups):
        v_tiles = [key_rows(v_ref, j, g) for j in window]
        carry_g, acc_g = [], []
        for sb in range(n_sub):
            carry = jnp.zeros((KEY_STEP, LANES), F32)
            ws = []
            for t in range(ATTN_UNROLLED_STEPS):
                i = (g * n_sub + sb) * ATTN_UNROLLED_STEPS + t
                ws.append(jnp.where(valids[sb][t], jnp.exp(log_betas[i] + sums[i][:, :LANES] + carry), 0.0)
                          .astype(BF16))
                carry = carry + sums[i][:, LANES:]
            lo = n_sub - 1 - sb
            values = jnp.concatenate(v_tiles[lo:lo + ATTN_UNROLLED_STEPS], axis=0)
            acc_g.append(_dot(jnp.concatenate(ws, axis=1), values))
            carry_g.append(carry)
        accs.append(jnp.concatenate(acc_g, axis=0))
        carries.append(jnp.concatenate(carry_g, axis=0))

    def alive(carries):
        top = carries[0]
        for c in carries[1:]:
            top = jnp.maximum(top, c)
        return (jnp.max(top) > LOG_W_FLOOR).astype(jnp.int32)

    def cond(state):
        j, live, _, _ = state
        return (j >= PAD // KEY_STEP) & (live > 0)

    def body(state):
        j, _, carries, accs = state
        ok = valid_keys(j, q_pos) & (j <= q_pos // KEY_STEP - ATTN_UNROLLED_STEPS)
        zs = [_dot_nt(qs[g], key_rows(k_ref, j, g)) for g in range(n_groups)]
        log_betas, sums = scores(zs, [ok] * n_groups)
        ws = [jnp.where(ok, jnp.exp(lb + sm[:, :LANES] + c), 0.0) for lb, sm, c in zip(log_betas, sums, carries)]
        accs = tuple(acc + _dot(w.astype(BF16), key_rows(v_ref, j, g)) for g, (acc, w) in enumerate(zip(accs, ws)))
        carries = tuple(c + sm[:, LANES:] for c, sm in zip(carries, sums))
        return j - 1, alive(carries), carries, accs

    rest = qi * n_sub + n_sub - 1 - ATTN_UNROLLED_STEPS
    state = lax.while_loop(cond, body, (rest, alive(carries), tuple(carries), tuple(accs)))
    for g, acc in enumerate(state[-1]):
        o_ref[:, g * LANES:(g + 1) * LANES] = acc.astype(o_ref.dtype)


def sb_attention(q, k, v, batch, groups):
    m, d = q.shape
    seq_rows = m // batch
    nq = seq_rows // ATTN_Q
    width = groups * LANES
    whole_seq = pl.BlockSpec((seq_rows, width), lambda b, p, i: (b, p), pipeline_mode=pl.Buffered(ATTN_KV_BUFFERS))
    return pl.pallas_call(
        _sb_attn_kernel,
        grid=(batch, d // width, nq),
        in_specs=[pl.BlockSpec((ATTN_Q, width), lambda b, p, i: (b * nq + i, p)), whole_seq, whole_seq],
        out_specs=pl.BlockSpec((ATTN_Q, width), lambda b, p, i: (b * nq + i, p)),
        out_shape=jax.ShapeDtypeStruct((m, d), BF16),
        compiler_params=_params(("parallel", "parallel", "arbitrary")),
        name="sb_attention",
    )(q, k, v)


def _row_tile(m, want):
    tm = want
    while m % tm:
        tm //= 2
    return tm


def kernel(x, meta_tokens, norm_mix_g, norm_ffn_g, ffn_up, ffn_conv_w, ffn_conv_b, ffn_down, rw_mix, rw_wr, rw_wk, rw_wv, rw_wo, rw_w0, rw_w1, rw_w2, rw_a0, rw_a1, rw_a2, rw_g1, rw_g2, rw_kk, rw_ka, rw_rk, rw_lnx_g, rw_lnx_b, rw_v0, rw_v1, rw_v2, kv_norm_g, sb_wk, sb_wv, sb_wq, sb_wo, final_norm_g):
    bsz, seq, d = x.shape
    depth = norm_mix_g.shape[0]
    n_a = rw_wr.shape[0]
    d_ff = ffn_down.shape[1]
    seq_rows = PAD + N_META + seq
    assert seq % BLOCK == 0 and d % LANES == 0 and seq_rows % CHUNK == 0
    m = bsz * seq_rows
    tm_big = _row_tile(m, 512)
    tm_proj = _row_tile(m, 512)
    tf = 256 if d_ff % 256 == 0 else LANES

    meta = jnp.broadcast_to(meta_tokens.astype(x.dtype)[None], (bsz, N_META, d))
    h = jnp.concatenate([jnp.zeros((bsz, PAD, d), x.dtype), meta, x], axis=1).reshape(m, d)

    v_first = None
    k_sh = v_sh = None
    for layer in range(depth):
        if layer < n_a:
            i = layer
            vres = (None, None, None) if i == 0 else (rw_v0[i - 1], rw_v1[i - 1], rw_v2[i - 1])
            r, k, v, lw, a, g = rwkv_proj(
                h, norm_mix_g[layer], rw_mix[i], rw_w0[i], rw_a0[i], vres[0], rw_wr[i], rw_wk[i], rw_wv[i],
                rw_w1[i], rw_w2[i], rw_a1[i], rw_a2[i], rw_g1[i], rw_g2[i], vres[1], vres[2],
                None if i == 0 else v_first, seq_rows, tm_proj)
            if i == 0:
                v_first = v
            y = rwkv_recurrence(r, k, v, lw, a, rw_kk[i], rw_ka[i], rw_rk[i].reshape(-1),
                                rw_lnx_g[i], rw_lnx_b[i], bsz)
            mixer = (y, g, rw_wo[i])
        else:
            j = layer - n_a
            if layer == n_a:
                q, k_sh, v_sh = norm_proj(
                    h, [norm_mix_g[layer], kv_norm_g, kv_norm_g],
                    [sb_wq[j].astype(BF16), sb_wk.astype(BF16), sb_wv.astype(BF16)], BF16, tm_big)
            else:
                (q,) = norm_proj(h, [norm_mix_g[layer]], [sb_wq[j].astype(BF16)], BF16, tm_big)
            o = sb_attention(q, k_sh, v_sh, bsz, ATTN_GROUPS if (d // LANES) % ATTN_GROUPS == 0 else 1)
            mixer = (o, None, sb_wo[j])
        h = mix_ffn(mixer[0], mixer[1], h, mixer[2], norm_ffn_g[layer], ffn_up[layer], ffn_conv_w[layer],
                    ffn_conv_b[layer], ffn_down[layer], final_norm_g if layer == depth - 1 else None,
                    seq_rows, tm_big, tf)
    return h.reshape(bsz, seq_rows, d)[:, PAD + N_META:, :]
```

```python
import functools
import math

import jax
import jax.numpy as jnp
from jax import lax
from jax.experimental import pallas as pl
from jax.experimental.pallas import tpu as pltpu

HEAD_DIM = 64
N_META = 16
BLOCK = 128
PAD = BLOCK - N_META
CONV_WIDTH = 3
GN_EPS = 64e-5
RMS_EPS = 1e-6

LANES = 128
CHUNK = 64
CONV_HALO = 8
SHIFT_HALO = 8
REC_SEQS = 4
ATTN_GROUPS = 8
ATTN_Q = BLOCK
KEY_STEP = HEAD_DIM
ATTN_UNROLLED_STEPS = 4
ATTN_KV_BUFFERS = 1
LOG_W_FLOOR = -100.0
VMEM_LIMIT = 56 * 1024 * 1024

F32 = jnp.float32
BF16 = jnp.bfloat16


def _dot(a, b):
    return jnp.dot(a, b, preferred_element_type=F32)


def _dot_nt(a, b):
    return lax.dot_general(a, b, (((1,), (1,)), ((), ())), preferred_element_type=F32)


def _dot_tn(a, b):
    return lax.dot_general(a, b, (((0,), (0,)), ((), ())), preferred_element_type=F32)


def _split_bf16(x):
    hi = x.astype(BF16)
    return hi, (x - hi.astype(F32)).astype(BF16)


def _rms_hat(x):
    return x * lax.rsqrt(jnp.mean(x * x, axis=-1, keepdims=True) + RMS_EPS)


def _params(sem):
    return pltpu.CompilerParams(dimension_semantics=sem, vmem_limit_bytes=VMEM_LIMIT)


def _row_in_seq(tile_rows, tile_index, seq_rows):
    rows = tile_index * tile_rows + lax.broadcasted_iota(jnp.int32, (tile_rows, 1), 0)
    return rows % seq_rows


def _norm_proj_kernel(n_groups, x_ref, *refs):
    g_refs = refs[:n_groups]
    w_refs = refs[n_groups:2 * n_groups]
    o_refs = refs[2 * n_groups:]
    xhat = _rms_hat(x_ref[...])
    for g_ref, w_ref, o_ref in zip(g_refs, w_refs, o_refs):
        xn = (xhat * g_ref[...]).astype(BF16)
        o_ref[...] = _dot(xn, w_ref[...]).astype(o_ref.dtype)


def norm_proj(h, gains, weights, out_dtype, tm):
    m, d = h.shape
    n = len(gains)
    in_specs = [pl.BlockSpec((tm, d), lambda i: (i, 0))]
    in_specs += [pl.BlockSpec((1, d), lambda i: (0, 0)) for _ in gains]
    in_specs += [pl.BlockSpec(w.shape, lambda i: (0, 0)) for w in weights]
    out_specs = [pl.BlockSpec((tm, w.shape[1]), lambda i: (i, 0)) for w in weights]
    out_shape = [jax.ShapeDtypeStruct((m, w.shape[1]), out_dtype) for w in weights]
    return pl.pallas_call(
        functools.partial(_norm_proj_kernel, n),
        grid=(m // tm,),
        in_specs=in_specs, out_specs=out_specs, out_shape=out_shape,
        compiler_params=_params(("parallel",)),
        name="norm_proj",
    )(h, *[g.reshape(1, d) for g in gains], *weights)


def _rwkv_proj_kernel(has_vres, seq_rows, tm, *refs):
    if has_vres:
        (h_ref, halo_ref, gain_ref, mix_ref, vec_ref, wr_ref, wk_ref, wv_ref, w1_ref, w2_ref,
         a1_ref, a2_ref, g1_ref, g2_ref, v1_ref, v2_ref, vfirst_ref,
         r_ref, k_ref, v_ref, lw_ref, a_ref, g_ref) = refs
    else:
        (h_ref, halo_ref, gain_ref, mix_ref, vec_ref, wr_ref, wk_ref, wv_ref, w1_ref, w2_ref,
         a1_ref, a2_ref, g1_ref, g2_ref,
         r_ref, k_ref, v_ref, lw_ref, a_ref, g_ref) = refs
    i = pl.program_id(0)
    gain = gain_ref[...]
    hn = _rms_hat(h_ref[...]) * gain
    prev_last = (_rms_hat(halo_ref[...]) * gain)[SHIFT_HALO - 1:SHIFT_HALO, :]
    row = lax.broadcasted_iota(jnp.int32, (tm, 1), 0)
    shifted = jnp.where(row == 0, prev_last, pltpu.roll(hn, 1, 0))
    shifted = jnp.where(_row_in_seq(tm, i, seq_rows) == 0, 0.0, shifted)
    xx = shifted - hn

    def mixed(j):
        return (hn + xx * mix_ref[j:j + 1, :]).astype(BF16)

    w0, a0, v0 = vec_ref[0:1, :], vec_ref[1:2, :], vec_ref[2:3, :]
    r_ref[...] = _dot(mixed(0), wr_ref[...])
    t = jnp.tanh(_dot(mixed(1), w1_ref[...])).astype(BF16)
    w_in = w0 + _dot(t, w2_ref[...])
    lw_ref[...] = -math.exp(-0.5) * jax.nn.sigmoid(w_in)
    k_ref[...] = _dot(mixed(2), wk_ref[...])
    xv = mixed(3)
    v = _dot(xv, wv_ref[...])
    if has_vres:
        gate = jax.nn.sigmoid(v0 + _dot(_dot(xv, v1_ref[...]).astype(BF16), v2_ref[...]))
        v = v + (vfirst_ref[...] - v) * gate
    v_ref[...] = v
    a_ref[...] = jax.nn.sigmoid(a0 + _dot(_dot(mixed(4), a1_ref[...]).astype(BF16), a2_ref[...]))
    g_ref[...] = _dot(jax.nn.sigmoid(_dot(mixed(5), g1_ref[...])).astype(BF16), g2_ref[...])


def _pad_lora(w_in, w_out):
    hid = w_in.shape[1]
    hid_p = -(-hid // LANES) * LANES
    return (jnp.pad(w_in, ((0, 0), (0, hid_p - hid))).astype(BF16),
            jnp.pad(w_out, ((0, hid_p - hid), (0, 0))).astype(BF16))


def rwkv_proj(h, gain, mix, w0, a0, v0, wr, wk, wv, w1, w2, a1, a2, g1, g2, v1, v2, v_first,
              seq_rows, tm):
    m, d = h.shape
    has_vres = v_first is not None
    mix_p = jnp.pad(mix, ((0, 8 - mix.shape[0]), (0, 0)))
    vecs = jnp.stack([w0, a0, v0 if has_vres else jnp.zeros_like(w0)])
    vecs = jnp.pad(vecs, ((0, 8 - vecs.shape[0]), (0, 0)))
    w1p, w2p = _pad_lora(w1, w2)
    a1p, a2p = _pad_lora(a1, a2)
    g1p, g2p = _pad_lora(g1, g2)
    full = lambda a: pl.BlockSpec(a.shape, lambda i: (0, 0), pipeline_mode=pl.Buffered(1))
    tile = pl.BlockSpec((tm, d), lambda i: (i, 0))
    halo = pl.BlockSpec((SHIFT_HALO, d), lambda i: (jnp.maximum(i * (tm // SHIFT_HALO) - 1, 0), 0))
    args = [h, h, gain.reshape(1, d), mix_p, vecs, wr.astype(BF16), wk.astype(BF16), wv.astype(BF16),
            w1p, w2p, a1p, a2p, g1p, g2p]
    in_specs = [tile, halo] + [full(a) for a in args[2:]]
    if has_vres:
        v1p, v2p = _pad_lora(v1, v2)
        args += [v1p, v2p, v_first]
        in_specs += [full(v1p), full(v2p), tile]
    return pl.pallas_call(
        functools.partial(_rwkv_proj_kernel, has_vres, seq_rows, tm),
        grid=(m // tm,),
        in_specs=in_specs,
        out_specs=[tile] * 6,
        out_shape=[jax.ShapeDtypeStruct((m, d), F32)] * 6,
        compiler_params=_params(("parallel",)),
        name="rwkv_proj",
    )(*args)


def _rwkv_rec_kernel(r_ref, k_ref, v_ref, lw_ref, a_ref, kk_ref, ka_ref, rk_ref, lg_ref, lb_ref,
                     y_ref, s_ref):
    c = CHUNK
    n_seq = r_ref.shape[0]
    n_pairs = r_ref.shape[2] // LANES

    @pl.when(pl.program_id(1) == 0)
    def _():
        s_ref[...] = jnp.zeros_like(s_ref)

    assert c == HEAD_DIM, "a head's chunk-by-chunk matrices fill exactly its lanes"
    lane = lax.broadcasted_iota(jnp.int32, (c, LANES), 1)
    row = lax.broadcasted_iota(jnp.int32, (c, LANES), 0)
    head0 = lane < HEAD_DIM
    strict_s = row > lane % c
    incl_s = row >= lane % c
    eye_s = (row == lane % c).astype(F32)
    row_l = lax.broadcasted_iota(jnp.int32, (LANES, LANES), 0)
    col_l = lax.broadcasted_iota(jnp.int32, (LANES, LANES), 1)
    same_head = (row_l // HEAD_DIM) == (col_l // HEAD_DIM)
    ones_bd2 = jnp.concatenate([same_head, same_head], axis=0).astype(BF16)
    row_t = lax.broadcasted_iota(jnp.int32, (c, 2 * c), 0)
    col_t = lax.broadcasted_iota(jnp.int32, (c, 2 * c), 1)
    tri2 = (row_t >= col_t % c).astype(BF16)

    def head_sums(xs):
        x = jnp.concatenate(xs, axis=0)
        out = _dot(jnp.concatenate(_split_bf16(x), axis=1), ones_bd2)
        return [out[i * c:(i + 1) * c] for i in range(len(xs))]

    def stack(x):
        x = x.astype(BF16)
        zero = jnp.zeros_like(x)
        return jnp.concatenate([jnp.where(head0, x, zero), jnp.where(head0, zero, x)], axis=0)

    cum_all = [_dot(tri2, jnp.concatenate(_split_bf16(lw_ref[q]), axis=0)) for q in range(n_seq)]

    pairs = range(n_seq * n_pairs)
    seqs = [p // n_pairs for p in pairs]
    sls = [slice((p % n_pairs) * LANES, (p % n_pairs + 1) * LANES) for p in pairs]
    ins_ = [(r_ref[q, :, sl], k_ref[q, :, sl], v_ref[q, :, sl], lw_ref[q, :, sl], a_ref[q, :, sl])
            for q, sl in zip(seqs, sls)]
    kku_ = [k_raw * kk_ref[:, sl] for (_, k_raw, _, _, _), sl in zip(ins_, sls)]
    kk_sq_ = head_sums([kk * kk for kk in kku_])
    r_, v_, k_, atrt_, bhkh_, pe_, aa_ = [], [], [], [], [], [], []
    for (r, k_raw, v, lw, a_sig), q, sl, kk, kk_sq in zip(ins_, seqs, sls, kku_, kk_sq_):
        cum = cum_all[q][:, sl]
        kk = kk * lax.rsqrt(jnp.maximum(kk_sq, 1e-24))
        k = k_raw * (1.0 + (a_sig - 1.0) * ka_ref[:, sl])
        b_vec = kk * a_sig
        cum_last = cum[c - 1:c, :]
        inv_p = jnp.exp(-cum)
        to_end = jnp.exp(cum_last - cum)
        atrt = jnp.concatenate([-kk * jnp.exp(cum - lw), r * jnp.exp(cum)], axis=0).astype(BF16)
        aa_.append(_dot_nt(atrt, jnp.concatenate([stack(b_vec * inv_p), stack(k * inv_p)], axis=0)))
        r_.append(r); v_.append(v); k_.append(k); atrt_.append(atrt)
        bhkh_.append(jnp.concatenate([b_vec * to_end, k * to_end], axis=0).astype(BF16))
        pe_.append(jnp.exp(cum_last))

    a_ab_ = [jnp.where(strict_s, aa[:c, :LANES], 0.0) for aa in aa_]
    a_ak_ = [jnp.where(strict_s, aa[:c, LANES:], 0.0) for aa in aa_]
    a_r_ = [jnp.concatenate([jnp.where(incl_s, aa[c:, :LANES], 0.0), jnp.where(incl_s, aa[c:, LANES:], 0.0)], axis=1)
            for aa in aa_]

    t_inv_ = [eye_s + a for a in a_ab_]
    pw_ = [_dot(a.astype(BF16), stack(a)) for a in a_ab_]
    n_steps = c.bit_length() - 2
    for step in range(n_steps):
        if step < n_steps - 1:
            tp_ = [_dot(jnp.concatenate([t, pw], axis=0).astype(BF16), stack(pw)) for t, pw in zip(t_inv_, pw_)]
            t_inv_ = [t + tp[:c] for t, tp in zip(t_inv_, tp_)]
            pw_ = [tp[c:] for tp in tp_]
        else:
            t_inv_ = [t + _dot(t.astype(BF16), stack(pw)) for t, pw in zip(t_inv_, pw_)]

    vst_ = [stack(v) for v in v_]
    akv_ = [_dot(a_ak.astype(BF16), vst) for a_ak, vst in zip(a_ak_, vst_)]
    s0_ = [s_ref[p] for p in pairs]
    ars_ = [_dot_nt(atrt, s0.astype(BF16)) for atrt, s0 in zip(atrt_, s0_)]
    u_ = [_dot(t.astype(BF16), stack(ars[:c] + akv)) for t, ars, akv in zip(t_inv_, ars_, akv_)]
    y_ = [ars[c:] + _dot(a_r.astype(BF16), jnp.concatenate([stack(u), vst], axis=0))
          for ars, a_r, u, vst in zip(ars_, a_r_, u_, vst_)]
    upd_ = [_dot_tn(jnp.concatenate([u, v], axis=0).astype(BF16), bhkh) for u, v, bhkh in zip(u_, v_, bhkh_)]
    for p in pairs:
        s_ref[p] = s0_[p] * pe_[p] + jnp.where(same_head, upd_[p], 0.0)

    inv_n = 1.0 / HEAD_DIM
    dev_ = [y - mu * inv_n for y, mu in zip(y_, head_sums(y_))]
    var_ = [ss * inv_n for ss in head_sums([dev * dev for dev in dev_])]
    rk_ = head_sums([r * k * rk_ref[:, sl] for r, k, sl in zip(r_, k_, sls)])
    for p, sl in enumerate(sls):
        y_ref[seqs[p], :, sl] = (dev_[p] * lax.rsqrt(var_[p] + GN_EPS) * lg_ref[:, sl] + lb_ref[:, sl]
                                 + rk_[p] * v_[p])


def rwkv_recurrence(r, k, v, lw, a, k_k, k_a, r_k, lnx_g, lnx_b, batch):
    m, d = r.shape
    seq_rows = m // batch
    n_seq = math.gcd(batch, REC_SEQS)
    tile = pl.BlockSpec((n_seq, CHUNK, d), lambda b, c: (b, c, 0))
    vec = pl.BlockSpec((1, d), lambda b, c: (0, 0))
    vecs = [t.reshape(1, d) for t in (k_k, k_a, r_k, lnx_g, lnx_b)]
    seqs = [t.reshape(batch, seq_rows, d) for t in (r, k, v, lw, a)]
    return pl.pallas_call(
        _rwkv_rec_kernel,
        grid=(batch // n_seq, seq_rows // CHUNK),
        in_specs=[tile] * 5 + [vec] * 5,
        out_specs=tile,
        out_shape=jax.ShapeDtypeStruct((batch, seq_rows, d), F32),
        scratch_shapes=[pltpu.VMEM((n_seq * (d // LANES), LANES, LANES), F32)],
        compiler_params=_params(("parallel", "arbitrary")),
        name="rwkv_recurrence",
    )(*seqs, *vecs).reshape(m, d)


def _mix_ffn_kernel(has_gate, has_final, seq_rows, tm, tf, *refs):
    refs = list(refs)
    y_ref = refs.pop(0)
    g_ref = refs.pop(0) if has_gate else None
    h_ref, wo_ref, gain_ref, wup_ref, cw_ref, cb_ref, wd_ref = refs[:7]
    fgain_ref = refs[7] if has_final else None
    o_ref, hmid_ref, utail_ref, act_ref = refs[-4:]
    i = pl.program_id(0)
    d_ff = wd_ref.shape[0]
    keep = _row_in_seq(tm, i, seq_rows) >= PAD

    @pl.when(i == 0)
    def _():
        utail_ref[...] = jnp.zeros_like(utail_ref)

    y = y_ref[...] if g_ref is None else y_ref[...] * g_ref[...]
    hmid_ref[...] = jnp.where(keep, h_ref[...] + _dot(y.astype(BF16), wo_ref[...]), 0.0)
    xn = (_rms_hat(hmid_ref[...]) * gain_ref[...]).astype(BF16)

    def conv(col):
        u_tile = _dot(xn, wup_ref[:, col:col + tf])
        u = jnp.concatenate([utail_ref[:, col:col + tf], u_tile], axis=0)
        utail_ref[:, col:col + tf] = u_tile[tm - CONV_HALO:, :]
        out = cb_ref[:, col:col + tf] + u_tile * cw_ref[CONV_WIDTH - 1:CONV_WIDTH, col:col + tf]
        for tap in range(CONV_WIDTH - 1):
            back = CONV_WIDTH - 1 - tap
            out = out + pltpu.roll(u, back, 0)[CONV_HALO:, :] * cw_ref[tap:tap + 1, col:col + tf]
        return out

    for col in range(0, d_ff, tf):
        act_ref[:, col:col + tf] = (jax.nn.silu(conv(col)) * conv(d_ff + col)).astype(BF16)

    h_out = jnp.where(keep, hmid_ref[...] + _dot(act_ref[...], wd_ref[...]), 0.0)
    o_ref[...] = h_out if fgain_ref is None else _rms_hat(h_out) * fgain_ref[...]


def mix_ffn(y, gate, h, w_o, gain, w_up, conv_w, conv_b, w_down, final_gain, seq_rows, tm, tf):
    m, d = h.shape
    d_ff = w_down.shape[0]
    cw = jnp.pad(conv_w, ((0, 8 - CONV_WIDTH), (0, 0)))
    cb = conv_b.reshape(1, 2 * d_ff)
    tile = pl.BlockSpec((tm, d), lambda i: (i, 0))
    const = lambda shape: pl.BlockSpec(shape, lambda i: (0, 0), pipeline_mode=pl.Buffered(1))
    seqs = [y] + ([gate] if gate is not None else []) + [h]
    finals = [] if final_gain is None else [final_gain.reshape(1, d)]
    return pl.pallas_call(
        functools.partial(_mix_ffn_kernel, gate is not None, final_gain is not None, seq_rows, tm, tf),
        grid=(m // tm,),
        in_specs=[tile] * len(seqs) + [
            const((d, d)), const((1, d)), const((d, 2 * d_ff)), const((8, 2 * d_ff)), const((1, 2 * d_ff)),
            const((d_ff, d))] + [const((1, d))] * len(finals),
        out_specs=tile,
        out_shape=jax.ShapeDtypeStruct((m, d), F32),
        scratch_shapes=[pltpu.VMEM((tm, d), F32), pltpu.VMEM((CONV_HALO, 2 * d_ff), F32),
                        pltpu.VMEM((tm, d_ff), BF16)],
        compiler_params=_params(("arbitrary",)),
        name="mix_ffn",
    )(*seqs, w_o.astype(BF16), gain.reshape(1, d), w_up.astype(BF16), cw, cb, w_down.astype(BF16), *finals)


def _sb_attn_kernel(q_ref, k_ref, v_ref, o_ref):
    qi = pl.program_id(2)
    qb = q_ref.shape[0]
    n_groups = q_ref.shape[1] // LANES
    scale = HEAD_DIM ** -0.5
    assert math.frexp(scale)[0] == 0.5, "the score scale is folded into bf16 q, which needs a power of two"
    qs = [q_ref[:, g * LANES:(g + 1) * LANES] * jnp.asarray(scale, BF16) for g in range(n_groups)]
    head0_k = lax.broadcasted_iota(jnp.int32, (KEY_STEP, LANES), 1) < HEAD_DIM
    q_pos = qi * qb + lax.broadcasted_iota(jnp.int32, (qb, LANES), 0)
    row_b = lax.broadcasted_iota(jnp.int32, (2 * LANES, 2 * LANES), 0) % LANES
    col_b = lax.broadcasted_iota(jnp.int32, (2 * LANES, 2 * LANES), 1)
    same_head = (row_b // KEY_STEP) == ((col_b % LANES) // KEY_STEP)
    sum_mat = jnp.where(same_head & ((row_b > col_b) | (col_b >= LANES)), 1.0, 0.0).astype(BF16)

    def stack(x):
        zero = jnp.zeros_like(x)
        return jnp.concatenate([jnp.where(head0_k, x, zero), jnp.where(head0_k, zero, x)], axis=0)

    def key_rows(ref, j, g):
        start = pl.multiple_of(jnp.maximum(j, 0) * KEY_STEP, KEY_STEP)
        return stack(ref[pl.ds(start, KEY_STEP), g * LANES:(g + 1) * LANES])

    def valid_keys(j, pos):
        k_pos = j * KEY_STEP + lax.broadcasted_iota(jnp.int32, pos.shape, 1) % KEY_STEP
        return (k_pos < pos) & (k_pos >= PAD)

    def masked(ok, x):
        return x if ok is None else jnp.where(ok, x, 0.0)

    def scores(zs, valids):
        rows = zs[0].shape[0]
        log_betas = [jnp.minimum(z, 0.0) - jnp.log(1.0 + jnp.exp(-jnp.abs(z))) for z in zs]
        splits = [jnp.concatenate(_split_bf16(masked(ok, lb - z)), axis=1)
                  for lb, z, ok in zip(log_betas, zs, valids)]
        sums = _dot(jnp.concatenate(splits, axis=0), sum_mat)
        return log_betas, [sums[i * rows:(i + 1) * rows] for i in range(len(zs))]

    n_sub = qb // KEY_STEP

    def sweep(pad_free):
        sub_rows = [slice(sb * KEY_STEP, (sb + 1) * KEY_STEP) for sb in range(n_sub)]
        sub_steps = [[qi * n_sub + sb - t for t in range(ATTN_UNROLLED_STEPS)] for sb in range(n_sub)]
        sub_pos = [qi * qb + sb * KEY_STEP + lax.broadcasted_iota(jnp.int32, (KEY_STEP, LANES), 0) for sb in range(n_sub)]
        valids = [[valid_keys(j, pos) if (t == 0 or not pad_free) else None for t, j in enumerate(steps)]
                  for pos, steps in zip(sub_pos, sub_steps)]
        window = [qi * n_sub + n_sub - 1 - t for t in range(ATTN_UNROLLED_STEPS + n_sub - 1)]
        zs, tile_valid = [], []
        for g in range(n_groups):
            k_tiles = [key_rows(k_ref, j, g) for j in window]
            for sb in range(n_sub):
                lo = n_sub - 1 - sb
                keys = jnp.concatenate(k_tiles[lo:lo + ATTN_UNROLLED_STEPS], axis=0)
                z_all = _dot_nt(qs[g][sub_rows[sb]], keys)
                zs += [z_all[:, t * LANES:(t + 1) * LANES] for t in range(ATTN_UNROLLED_STEPS)]
                tile_valid += valids[sb]
        log_betas, sums = scores(zs, tile_valid)
        carries, accs = [], []
        for g in range(n_groups):
            v_tiles = [key_rows(v_ref, j, g) for j in window]
            carry_g, acc_g = [], []
            for sb in range(n_sub):
                carry = jnp.zeros((KEY_STEP, LANES), F32)
                ws = []
                for t in range(ATTN_UNROLLED_STEPS):
                    i = (g * n_sub + sb) * ATTN_UNROLLED_STEPS + t
                    ws.append(masked(valids[sb][t], jnp.exp(log_betas[i] + sums[i][:, :LANES] + carry)).astype(BF16))
                    carry = carry + sums[i][:, LANES:]
                lo = n_sub - 1 - sb
                values = jnp.concatenate(v_tiles[lo:lo + ATTN_UNROLLED_STEPS], axis=0)
                acc_g.append(_dot(jnp.concatenate(ws, axis=1), values))
                carry_g.append(carry)
            accs.append(jnp.concatenate(acc_g, axis=0))
            carries.append(jnp.concatenate(carry_g, axis=0))

        def alive(carries):
            top = carries[0]
            for c in carries[1:]:
                top = jnp.maximum(top, c)
            return (jnp.max(top) > LOG_W_FLOOR).astype(jnp.int32)

        def cond(state):
            j, live, _, _ = state
            return (j >= PAD // KEY_STEP) & (live > 0)

        def body(state):
            j, _, carries, accs = state
            ok = valid_keys(j, q_pos) & (j <= q_pos // KEY_STEP - ATTN_UNROLLED_STEPS)
            zs = [_dot_nt(qs[g], key_rows(k_ref, j, g)) for g in range(n_groups)]
            log_betas, sums = scores(zs, [ok] * n_groups)
            ws = [jnp.where(ok, jnp.exp(lb + sm[:, :LANES] + c), 0.0) for lb, sm, c in zip(log_betas, sums, carries)]
            accs = tuple(acc + _dot(w.astype(BF16), key_rows(v_ref, j, g)) for g, (acc, w) in enumerate(zip(accs, ws)))
            carries = tuple(c + sm[:, LANES:] for c, sm in zip(carries, sums))
            return j - 1, alive(carries), carries, accs

        rest = qi * n_sub + n_sub - 1 - ATTN_UNROLLED_STEPS
        state = lax.while_loop(cond, body, (rest, alive(carries), tuple(carries), tuple(accs)))
        for g, acc in enumerate(state[-1]):
            o_ref[:, g * LANES:(g + 1) * LANES] = acc.astype(o_ref.dtype)

    pad_free = (qi * n_sub - (ATTN_UNROLLED_STEPS - 1)) * KEY_STEP >= PAD
    pl.when(pad_free)(functools.partial(sweep, True))
    pl.when(jnp.logical_not(pad_free))(functools.partial(sweep, False))


def sb_attention(q, k, v, batch, groups):
    m, d = q.shape
    seq_rows = m // batch
    assert seq_rows % ATTN_Q == 0 and ATTN_Q % KEY_STEP == 0
    nq = seq_rows // ATTN_Q
    width = groups * LANES
    whole_seq = pl.BlockSpec((seq_rows, width), lambda b, p, i: (b, p), pipeline_mode=pl.Buffered(ATTN_KV_BUFFERS))
    return pl.pallas_call(
        _sb_attn_kernel,
        grid=(batch, d // width, nq),
        in_specs=[pl.BlockSpec((ATTN_Q, width), lambda b, p, i: (b * nq + i, p)), whole_seq, whole_seq],
        out_specs=pl.BlockSpec((ATTN_Q, width), lambda b, p, i: (b * nq + i, p)),
        out_shape=jax.ShapeDtypeStruct((m, d), BF16),
        compiler_params=_params(("parallel", "parallel", "arbitrary")),
        name="sb_attention",
    )(q, k, v)


def _row_tile(m, want):
    tm = want
    while m % tm:
        tm //= 2
    return tm


def kernel(x, meta_tokens, norm_mix_g, norm_ffn_g, ffn_up, ffn_conv_w, ffn_conv_b, ffn_down, rw_mix, rw_wr, rw_wk, rw_wv, rw_wo, rw_w0, rw_w1, rw_w2, rw_a0, rw_a1, rw_a2, rw_g1, rw_g2, rw_kk, rw_ka, rw_rk, rw_lnx_g, rw_lnx_b, rw_v0, rw_v1, rw_v2, kv_norm_g, sb_wk, sb_wv, sb_wq, sb_wo, final_norm_g):
    bsz, seq, d = x.shape
    depth = norm_mix_g.shape[0]
    n_a = rw_wr.shape[0]
    d_ff = ffn_down.shape[1]
    seq_rows = PAD + N_META + seq
    assert seq % BLOCK == 0 and d % LANES == 0 and seq_rows % CHUNK == 0
    m = bsz * seq_rows
    tm_big = _row_tile(m, 512)
    tm_proj = _row_tile(m, 512)
    tf = 256 if d_ff % 256 == 0 else LANES

    meta = jnp.broadcast_to(meta_tokens.astype(x.dtype)[None], (bsz, N_META, d))
    h = jnp.concatenate([jnp.zeros((bsz, PAD, d), x.dtype), meta, x], axis=1).reshape(m, d)

    v_first = None
    k_sh = v_sh = None
    for layer in range(depth):
        if layer < n_a:
            i = layer
            vres = (None, None, None) if i == 0 else (rw_v0[i - 1], rw_v1[i - 1], rw_v2[i - 1])
            r, k, v, lw, a, g = rwkv_proj(
                h, norm_mix_g[layer], rw_mix[i], rw_w0[i], rw_a0[i], vres[0], rw_wr[i], rw_wk[i], rw_wv[i],
                rw_w1[i], rw_w2[i], rw_a1[i], rw_a2[i], rw_g1[i], rw_g2[i], vres[1], vres[2],
                None if i == 0 else v_first, seq_rows, tm_proj)
            if i == 0:
                v_first = v
            y = rwkv_recurrence(r, k, v, lw, a, rw_kk[i], rw_ka[i], rw_rk[i].reshape(-1),
                                rw_lnx_g[i], rw_lnx_b[i], bsz)
            mixer = (y, g, rw_wo[i])
        else:
            j = layer - n_a
            if layer == n_a:
                q, k_sh, v_sh = norm_proj(
                    h, [norm_mix_g[layer], kv_norm_g, kv_norm_g],
                    [sb_wq[j].astype(BF16), sb_wk.astype(BF16), sb_wv.astype(BF16)], BF16, tm_big)
            else:
                (q,) = norm_proj(h, [norm_mix_g[layer]], [sb_wq[j].astype(BF16)], BF16, tm_big)
            o = sb_attention(q, k_sh, v_sh, bsz, ATTN_GROUPS if (d // LANES) % ATTN_GROUPS == 0 else 1)
            mixer = (o, None, sb_wo[j])
        h = mix_ffn(mixer[0], mixer[1], h, mixer[2], norm_ffn_g[layer], ffn_up[layer], ffn_conv_w[layer],
                    ffn_conv_b[layer], ffn_down[layer], final_norm_g if layer == depth - 1 else None,
                    seq_rows, tm_big, tf)
    return h.reshape(bsz, seq_rows, d)[:, PAD + N_META:, :]
```

```python
import functools
import math

import jax
import jax.numpy as jnp
from jax import lax
from jax.experimental import pallas as pl
from jax.experimental.pallas import tpu as pltpu

HEAD_DIM = 64
N_META = 16
BLOCK = 128
PAD = BLOCK - N_META
CONV_WIDTH = 3
GN_EPS = 64e-5
RMS_EPS = 1e-6

LANES = 128
CHUNK = 64
CONV_HALO = 8
SHIFT_HALO = 8
REC_SEQS = 4
ATTN_GROUPS = 8
ATTN_Q = BLOCK
KEY_STEP = HEAD_DIM
ATTN_UNROLLED_STEPS = 4
ATTN_KV_BUFFERS = 1
LOG_W_FLOOR = -100.0
VMEM_LIMIT = 56 * 1024 * 1024

F32 = jnp.float32
BF16 = jnp.bfloat16


def _dot(a, b):
    return jnp.dot(a, b, preferred_element_type=F32)


def _dot_nt(a, b):
    return lax.dot_general(a, b, (((1,), (1,)), ((), ())), preferred_element_type=F32)


def _dot_tn(a, b):
    return lax.dot_general(a, b, (((0,), (0,)), ((), ())), preferred_element_type=F32)


def _split_bf16(x):
    hi = x.astype(BF16)
    return hi, (x - hi.astype(F32)).astype(BF16)


def _rms_hat(x):
    return x * lax.rsqrt(jnp.mean(x * x, axis=-1, keepdims=True) + RMS_EPS)


def _params(sem):
    return pltpu.CompilerParams(dimension_semantics=sem, vmem_limit_bytes=VMEM_LIMIT)


def _row_in_seq(tile_rows, tile_index, seq_rows):
    rows = tile_index * tile_rows + lax.broadcasted_iota(jnp.int32, (tile_rows, 1), 0)
    return rows % seq_rows


def _norm_proj_kernel(n_groups, x_ref, *refs):
    g_refs = refs[:n_groups]
    w_refs = refs[n_groups:2 * n_groups]
    o_refs = refs[2 * n_groups:]
    xhat = _rms_hat(x_ref[...])
    for g_ref, w_ref, o_ref in zip(g_refs, w_refs, o_refs):
        xn = (xhat * g_ref[...]).astype(BF16)
        o_ref[...] = _dot(xn, w_ref[...]).astype(o_ref.dtype)


def norm_proj(h, gains, weights, out_dtype, tm):
    m, d = h.shape
    n = len(gains)
    in_specs = [pl.BlockSpec((tm, d), lambda i: (i, 0))]
    in_specs += [pl.BlockSpec((1, d), lambda i: (0, 0)) for _ in gains]
    in_specs += [pl.BlockSpec(w.shape, lambda i: (0, 0)) for w in weights]
    out_specs = [pl.BlockSpec((tm, w.shape[1]), lambda i: (i, 0)) for w in weights]
    out_shape = [jax.ShapeDtypeStruct((m, w.shape[1]), out_dtype) for w in weights]
    return pl.pallas_call(
        functools.partial(_norm_proj_kernel, n),
        grid=(m // tm,),
        in_specs=in_specs, out_specs=out_specs, out_shape=out_shape,
        compiler_params=_params(("parallel",)),
        name="norm_proj",
    )(h, *[g.reshape(1, d) for g in gains], *weights)


def _rwkv_proj_kernel(has_vres, seq_rows, tm, *refs):
    if has_vres:
        (h_ref, halo_ref, gain_ref, mix_ref, vec_ref, wr_ref, wk_ref, wv_ref, w1_ref, w2_ref,
         a1_ref, a2_ref, g1_ref, g2_ref, v1_ref, v2_ref, vfirst_ref,
         r_ref, k_ref, v_ref, lw_ref, a_ref, g_ref) = refs
    else:
        (h_ref, halo_ref, gain_ref, mix_ref, vec_ref, wr_ref, wk_ref, wv_ref, w1_ref, w2_ref,
         a1_ref, a2_ref, g1_ref, g2_ref,
         r_ref, k_ref, v_ref, lw_ref, a_ref, g_ref) = refs
    i = pl.program_id(0)
    gain = gain_ref[...]
    hn = _rms_hat(h_ref[...]) * gain
    prev_last = (_rms_hat(halo_ref[...]) * gain)[SHIFT_HALO - 1:SHIFT_HALO, :]
    row = lax.broadcasted_iota(jnp.int32, (tm, 1), 0)
    shifted = jnp.where(row == 0, prev_last, pltpu.roll(hn, 1, 0))
    shifted = jnp.where(_row_in_seq(tm, i, seq_rows) == 0, 0.0, shifted)
    xx = shifted - hn

    def mixed(j):
        return (hn + xx * mix_ref[j:j + 1, :]).astype(BF16)

    w0, a0, v0 = vec_ref[0:1, :], vec_ref[1:2, :], vec_ref[2:3, :]
    xv = mixed(3)
    r_ref[...] = _dot(mixed(0), wr_ref[...])
    t_w = _dot(mixed(1), w1_ref[...])
    k_ref[...] = _dot(mixed(2), wk_ref[...])
    t_a = _dot(mixed(4), a1_ref[...])
    v = _dot(xv, wv_ref[...])
    t_g = _dot(mixed(5), g1_ref[...])
    if has_vres:
        t_v = _dot(xv, v1_ref[...])
    w_in = w0 + _dot(jnp.tanh(t_w).astype(BF16), w2_ref[...])
    lw_ref[...] = -math.exp(-0.5) * jax.nn.sigmoid(w_in)
    a_ref[...] = jax.nn.sigmoid(a0 + _dot(t_a.astype(BF16), a2_ref[...]))
    g_ref[...] = _dot(jax.nn.sigmoid(t_g).astype(BF16), g2_ref[...])
    if has_vres:
        gate = jax.nn.sigmoid(v0 + _dot(t_v.astype(BF16), v2_ref[...]))
        v = v + (vfirst_ref[...] - v) * gate
    v_ref[...] = v


def _pad_lora(w_in, w_out):
    hid = w_in.shape[1]
    hid_p = -(-hid // LANES) * LANES
    return (jnp.pad(w_in, ((0, 0), (0, hid_p - hid))).astype(BF16),
            jnp.pad(w_out, ((0, hid_p - hid), (0, 0))).astype(BF16))


def rwkv_proj(h, gain, mix, w0, a0, v0, wr, wk, wv, w1, w2, a1, a2, g1, g2, v1, v2, v_first,
              seq_rows, tm):
    m, d = h.shape
    has_vres = v_first is not None
    mix_p = jnp.pad(mix, ((0, 8 - mix.shape[0]), (0, 0)))
    vecs = jnp.stack([w0, a0, v0 if has_vres else jnp.zeros_like(w0)])
    vecs = jnp.pad(vecs, ((0, 8 - vecs.shape[0]), (0, 0)))
    w1p, w2p = _pad_lora(w1, w2)
    a1p, a2p = _pad_lora(a1, a2)
    g1p, g2p = _pad_lora(g1, g2)
    full = lambda a: pl.BlockSpec(a.shape, lambda i: (0, 0), pipeline_mode=pl.Buffered(1))
    tile = pl.BlockSpec((tm, d), lambda i: (i, 0))
    halo = pl.BlockSpec((SHIFT_HALO, d), lambda i: (jnp.maximum(i * (tm // SHIFT_HALO) - 1, 0), 0))
    args = [h, h, gain.reshape(1, d), mix_p, vecs, wr.astype(BF16), wk.astype(BF16), wv.astype(BF16),
            w1p, w2p, a1p, a2p, g1p, g2p]
    in_specs = [tile, halo] + [full(a) for a in args[2:]]
    if has_vres:
        v1p, v2p = _pad_lora(v1, v2)
        args += [v1p, v2p, v_first]
        in_specs += [full(v1p), full(v2p), tile]
    return pl.pallas_call(
        functools.partial(_rwkv_proj_kernel, has_vres, seq_rows, tm),
        grid=(m // tm,),
        in_specs=in_specs,
        out_specs=[tile] * 6,
        out_shape=[jax.ShapeDtypeStruct((m, d), F32)] * 6,
        compiler_params=_params(("parallel",)),
        name="rwkv_proj",
    )(*args)


def _rwkv_rec_kernel(r_ref, k_ref, v_ref, lw_ref, a_ref, kk_ref, ka_ref, rk_ref, lg_ref, lb_ref,
                     y_ref, s_ref):
    c = CHUNK
    n_seq = r_ref.shape[0]
    n_pairs = r_ref.shape[2] // LANES

    @pl.when(pl.program_id(1) == 0)
    def _():
        s_ref[...] = jnp.zeros_like(s_ref)

    assert c == HEAD_DIM, "a head's chunk-by-chunk matrices fill exactly its lanes"
    lane = lax.broadcasted_iota(jnp.int32, (c, LANES), 1)
    row = lax.broadcasted_iota(jnp.int32, (c, LANES), 0)
    head0 = lane < HEAD_DIM
    strict_s = row > lane % c
    incl_s = row >= lane % c
    eye_s = (row == lane % c).astype(F32)
    row_l = lax.broadcasted_iota(jnp.int32, (LANES, LANES), 0)
    col_l = lax.broadcasted_iota(jnp.int32, (LANES, LANES), 1)
    same_head = (row_l // HEAD_DIM) == (col_l // HEAD_DIM)
    ones_bd2 = jnp.concatenate([same_head, same_head], axis=0).astype(BF16)
    row_t = lax.broadcasted_iota(jnp.int32, (c, 2 * c), 0)
    col_t = lax.broadcasted_iota(jnp.int32, (c, 2 * c), 1)
    tri2 = (row_t >= col_t % c).astype(BF16)

    def head_sums(xs):
        x = jnp.concatenate(xs, axis=0)
        out = _dot(jnp.concatenate(_split_bf16(x), axis=1), ones_bd2)
        return [out[i * c:(i + 1) * c] for i in range(len(xs))]

    def stack(x):
        x = x.astype(BF16)
        zero = jnp.zeros_like(x)
        return jnp.concatenate([jnp.where(head0, x, zero), jnp.where(head0, zero, x)], axis=0)

    cum_all = [_dot(tri2, jnp.concatenate(_split_bf16(lw_ref[q]), axis=0)) for q in range(n_seq)]

    pairs = range(n_seq * n_pairs)
    seqs = [p // n_pairs for p in pairs]
    sls = [slice((p % n_pairs) * LANES, (p % n_pairs + 1) * LANES) for p in pairs]
    ins_ = [(r_ref[q, :, sl], k_ref[q, :, sl], v_ref[q, :, sl], lw_ref[q, :, sl], a_ref[q, :, sl])
            for q, sl in zip(seqs, sls)]
    kku_ = [k_raw * kk_ref[:, sl] for (_, k_raw, _, _, _), sl in zip(ins_, sls)]
    kk_sq_ = head_sums([kk * kk for kk in kku_])
    r_, v_, k_, atrt_, bhkh_, pe_, aa_ = [], [], [], [], [], [], []
    for (r, k_raw, v, lw, a_sig), q, sl, kk, kk_sq in zip(ins_, seqs, sls, kku_, kk_sq_):
        cum = cum_all[q][:, sl]
        kk = kk * lax.rsqrt(jnp.maximum(kk_sq, 1e-24))
        k = k_raw * (1.0 + (a_sig - 1.0) * ka_ref[:, sl])
        b_vec = kk * a_sig
        cum_last = cum[c - 1:c, :]
        inv_p = jnp.exp(-cum)
        to_end = jnp.exp(cum_last - cum)
        atrt = jnp.concatenate([-kk * jnp.exp(cum - lw), r * jnp.exp(cum)], axis=0).astype(BF16)
        aa_.append(_dot_nt(atrt, jnp.concatenate([stack(b_vec * inv_p), stack(k * inv_p)], axis=0)))
        r_.append(r); v_.append(v); k_.append(k); atrt_.append(atrt)
        bhkh_.append(jnp.concatenate([b_vec * to_end, k * to_end], axis=0).astype(BF16))
        pe_.append(jnp.exp(cum_last))

    a_ab_ = [jnp.where(strict_s, aa[:c, :LANES], 0.0) for aa in aa_]
    a_ak_ = [jnp.where(strict_s, aa[:c, LANES:], 0.0) for aa in aa_]
    a_r_ = [jnp.concatenate([jnp.where(incl_s, aa[c:, :LANES], 0.0), jnp.where(incl_s, aa[c:, LANES:], 0.0)], axis=1)
            for aa in aa_]

    t_inv_ = [eye_s + a for a in a_ab_]
    pw_ = [_dot(a.astype(BF16), stack(a)) for a in a_ab_]
    n_steps = c.bit_length() - 2
    for step in range(n_steps):
        if step < n_steps - 1:
            tp_ = [_dot(jnp.concatenate([t, pw], axis=0).astype(BF16), stack(pw)) for t, pw in zip(t_inv_, pw_)]
            t_inv_ = [t + tp[:c] for t, tp in zip(t_inv_, tp_)]
            pw_ = [tp[c:] for tp in tp_]
        else:
            t_inv_ = [t + _dot(t.astype(BF16), stack(pw)) for t, pw in zip(t_inv_, pw_)]

    vst_ = [stack(v) for v in v_]
    akv_ = [_dot(a_ak.astype(BF16), vst) for a_ak, vst in zip(a_ak_, vst_)]
    s0_ = [s_ref[p] for p in pairs]
    ars_ = [_dot_nt(atrt, s0.astype(BF16)) for atrt, s0 in zip(atrt_, s0_)]
    u_ = [_dot(t.astype(BF16), stack(ars[:c] + akv)) for t, ars, akv in zip(t_inv_, ars_, akv_)]
    y_ = [ars[c:] + _dot(a_r.astype(BF16), jnp.concatenate([stack(u), vst], axis=0))
          for ars, a_r, u, vst in zip(ars_, a_r_, u_, vst_)]
    upd_ = [_dot_tn(jnp.concatenate([u, v], axis=0).astype(BF16), bhkh) for u, v, bhkh in zip(u_, v_, bhkh_)]
    for p in pairs:
        s_ref[p] = s0_[p] * pe_[p] + jnp.where(same_head, upd_[p], 0.0)

    inv_n = 1.0 / HEAD_DIM
    dev_ = [y - mu * inv_n for y, mu in zip(y_, head_sums(y_))]
    var_ = [ss * inv_n for ss in head_sums([dev * dev for dev in dev_])]
    rk_ = head_sums([r * k * rk_ref[:, sl] for r, k, sl in zip(r_, k_, sls)])
    for p, sl in enumerate(sls):
        y_ref[seqs[p], :, sl] = (dev_[p] * lax.rsqrt(var_[p] + GN_EPS) * lg_ref[:, sl] + lb_ref[:, sl]
                                 + rk_[p] * v_[p])


def rwkv_recurrence(r, k, v, lw, a, k_k, k_a, r_k, lnx_g, lnx_b, batch):
    m, d = r.shape
    seq_rows = m // batch
    n_seq = math.gcd(batch, REC_SEQS)
    tile = pl.BlockSpec((n_seq, CHUNK, d), lambda b, c: (b, c, 0))
    vec = pl.BlockSpec((1, d), lambda b, c: (0, 0))
    vecs = [t.reshape(1, d) for t in (k_k, k_a, r_k, lnx_g, lnx_b)]
    seqs = [t.reshape(batch, seq_rows, d) for t in (r, k, v, lw, a)]
    return pl.pallas_call(
        _rwkv_rec_kernel,
        grid=(batch // n_seq, seq_rows // CHUNK),
        in_specs=[tile] * 5 + [vec] * 5,
        out_specs=tile,
        out_shape=jax.ShapeDtypeStruct((batch, seq_rows, d), F32),
        scratch_shapes=[pltpu.VMEM((n_seq * (d // LANES), LANES, LANES), F32)],
        compiler_params=_params(("parallel", "arbitrary")),
        name="rwkv_recurrence",
    )(*seqs, *vecs).reshape(m, d)


def _mix_ffn_kernel(has_gate, has_final, seq_rows, tm, tf, *refs):
    refs = list(refs)
    y_ref = refs.pop(0)
    g_ref = refs.pop(0) if has_gate else None
    h_ref, wo_ref, gain_ref, wup_ref, cw_ref, cb_ref, wd_ref = refs[:7]
    fgain_ref = refs[7] if has_final else None
    o_ref, hmid_ref, utail_ref, act_ref = refs[-4:]
    i = pl.program_id(0)
    d_ff = wd_ref.shape[0]
    keep = _row_in_seq(tm, i, seq_rows) >= PAD

    @pl.when(i == 0)
    def _():
        utail_ref[...] = jnp.zeros_like(utail_ref)

    y = y_ref[...] if g_ref is None else y_ref[...] * g_ref[...]
    hmid_ref[...] = jnp.where(keep, h_ref[...] + _dot(y.astype(BF16), wo_ref[...]), 0.0)
    xn = (_rms_hat(hmid_ref[...]) * gain_ref[...]).astype(BF16)

    def conv(col):
        u_tile = _dot(xn, wup_ref[:, col:col + tf])
        u = jnp.concatenate([utail_ref[:, col:col + tf], u_tile], axis=0)
        utail_ref[:, col:col + tf] = u_tile[tm - CONV_HALO:, :]
        out = cb_ref[:, col:col + tf] + u_tile * cw_ref[CONV_WIDTH - 1:CONV_WIDTH, col:col + tf]
        for tap in range(CONV_WIDTH - 1):
            back = CONV_WIDTH - 1 - tap
            out = out + pltpu.roll(u, back, 0)[CONV_HALO:, :] * cw_ref[tap:tap + 1, col:col + tf]
        return out

    for col in range(0, d_ff, tf):
        act_ref[:, col:col + tf] = (jax.nn.silu(conv(col)) * conv(d_ff + col)).astype(BF16)

    h_out = jnp.where(keep, hmid_ref[...] + _dot(act_ref[...], wd_ref[...]), 0.0)
    o_ref[...] = h_out if fgain_ref is None else _rms_hat(h_out) * fgain_ref[...]


def mix_ffn(y, gate, h, w_o, gain, w_up, conv_w, conv_b, w_down, final_gain, seq_rows, tm, tf):
    m, d = h.shape
    d_ff = w_down.shape[0]
    cw = jnp.pad(conv_w, ((0, 8 - CONV_WIDTH), (0, 0)))
    cb = conv_b.reshape(1, 2 * d_ff)
    tile = pl.BlockSpec((tm, d), lambda i: (i, 0))
    const = lambda shape: pl.BlockSpec(shape, lambda i: (0, 0), pipeline_mode=pl.Buffered(1))
    seqs = [y] + ([gate] if gate is not None else []) + [h]
    finals = [] if final_gain is None else [final_gain.reshape(1, d)]
    return pl.pallas_call(
        functools.partial(_mix_ffn_kernel, gate is not None, final_gain is not None, seq_rows, tm, tf),
        grid=(m // tm,),
        in_specs=[tile] * len(seqs) + [
            const((d, d)), const((1, d)), const((d, 2 * d_ff)), const((8, 2 * d_ff)), const((1, 2 * d_ff)),
            const((d_ff, d))] + [const((1, d))] * len(finals),
        out_specs=tile,
        out_shape=jax.ShapeDtypeStruct((m, d), F32),
        scratch_shapes=[pltpu.VMEM((tm, d), F32), pltpu.VMEM((CONV_HALO, 2 * d_ff), F32),
                        pltpu.VMEM((tm, d_ff), BF16)],
        compiler_params=_params(("arbitrary",)),
        name="mix_ffn",
    )(*seqs, w_o.astype(BF16), gain.reshape(1, d), w_up.astype(BF16), cw, cb, w_down.astype(BF16), *finals)


def _sb_attn_kernel(q_ref, k_ref, v_ref, o_ref):
    qi = pl.program_id(2)
    qb = q_ref.shape[0]
    n_groups = q_ref.shape[1] // LANES
    scale = HEAD_DIM ** -0.5
    assert math.frexp(scale)[0] == 0.5, "the score scale is folded into bf16 q, which needs a power of two"
    qs = [q_ref[:, g * LANES:(g + 1) * LANES] * jnp.asarray(scale, BF16) for g in range(n_groups)]
    head0_k = lax.broadcasted_iota(jnp.int32, (KEY_STEP, LANES), 1) < HEAD_DIM
    q_pos = qi * qb + lax.broadcasted_iota(jnp.int32, (qb, LANES), 0)
    row_b = lax.broadcasted_iota(jnp.int32, (2 * LANES, 2 * LANES), 0) % LANES
    col_b = lax.broadcasted_iota(jnp.int32, (2 * LANES, 2 * LANES), 1)
    same_head = (row_b // KEY_STEP) == ((col_b % LANES) // KEY_STEP)
    sum_mat = jnp.where(same_head & ((row_b > col_b) | (col_b >= LANES)), 1.0, 0.0).astype(BF16)

    def stack(x):
        zero = jnp.zeros_like(x)
        return jnp.concatenate([jnp.where(head0_k, x, zero), jnp.where(head0_k, zero, x)], axis=0)

    def key_rows(ref, j, g):
        start = pl.multiple_of(jnp.maximum(j, 0) * KEY_STEP, KEY_STEP)
        return stack(ref[pl.ds(start, KEY_STEP), g * LANES:(g + 1) * LANES])

    def valid_keys(j, pos):
        k_pos = j * KEY_STEP + lax.broadcasted_iota(jnp.int32, pos.shape, 1) % KEY_STEP
        return (k_pos < pos) & (k_pos >= PAD)

    def masked(ok, x):
        return x if ok is None else jnp.where(ok, x, 0.0)

    def scores(zs, valids):
        rows = zs[0].shape[0]
        log_betas = [jnp.minimum(z, 0.0) - jnp.log(1.0 + jnp.exp(-jnp.abs(z))) for z in zs]
        splits = [jnp.concatenate(_split_bf16(masked(ok, lb - z)), axis=1)
                  for lb, z, ok in zip(log_betas, zs, valids)]
        sums = _dot(jnp.concatenate(splits, axis=0), sum_mat)
        return log_betas, [sums[i * rows:(i + 1) * rows] for i in range(len(zs))]

    n_sub = qb // KEY_STEP

    def sweep(pad_free):
        sub_rows = [slice(sb * KEY_STEP, (sb + 1) * KEY_STEP) for sb in range(n_sub)]
        sub_steps = [[qi * n_sub + sb - t for t in range(ATTN_UNROLLED_STEPS)] for sb in range(n_sub)]
        sub_pos = [qi * qb + sb * KEY_STEP + lax.broadcasted_iota(jnp.int32, (KEY_STEP, LANES), 0) for sb in range(n_sub)]
        valids = [[valid_keys(j, pos) if (t == 0 or not pad_free) else None for t, j in enumerate(steps)]
                  for pos, steps in zip(sub_pos, sub_steps)]
        window = [qi * n_sub + n_sub - 1 - t for t in range(ATTN_UNROLLED_STEPS + n_sub - 1)]
        zs, tile_valid = [], []
        for g in range(n_groups):
            k_tiles = [key_rows(k_ref, j, g) for j in window]
            for sb in range(n_sub):
                lo = n_sub - 1 - sb
                keys = jnp.concatenate(k_tiles[lo:lo + ATTN_UNROLLED_STEPS], axis=0)
                z_all = _dot_nt(qs[g][sub_rows[sb]], keys)
                zs += [z_all[:, t * LANES:(t + 1) * LANES] for t in range(ATTN_UNROLLED_STEPS)]
                tile_valid += valids[sb]
        log_betas, sums = scores(zs, tile_valid)
        carries, accs = [], []
        for g in range(n_groups):
            v_tiles = [key_rows(v_ref, j, g) for j in window]
            carry_g, acc_g = [], []
            for sb in range(n_sub):
                carry = jnp.zeros((KEY_STEP, LANES), F32)
                ws = []
                for t in range(ATTN_UNROLLED_STEPS):
                    i = (g * n_sub + sb) * ATTN_UNROLLED_STEPS + t
                    ws.append(masked(valids[sb][t], jnp.exp(log_betas[i] + sums[i][:, :LANES] + carry)).astype(BF16))
                    carry = carry + sums[i][:, LANES:]
                lo = n_sub - 1 - sb
                values = jnp.concatenate(v_tiles[lo:lo + ATTN_UNROLLED_STEPS], axis=0)
                acc_g.append(_dot(jnp.concatenate(ws, axis=1), values))
                carry_g.append(carry)
            accs.append(jnp.concatenate(acc_g, axis=0))
            carries.append(jnp.concatenate(carry_g, axis=0))

        def alive(carries):
            top = carries[0]
            for c in carries[1:]:
                top = jnp.maximum(top, c)
            return (jnp.max(top) > LOG_W_FLOOR).astype(jnp.int32)

        def cond(state):
            j, live, _, _ = state
            return (j >= PAD // KEY_STEP) & (live > 0)

        def body(state):
            j, _, carries, accs = state
            ok = valid_keys(j, q_pos) & (j <= q_pos // KEY_STEP - ATTN_UNROLLED_STEPS)
            zs = [_dot_nt(qs[g], key_rows(k_ref, j, g)) for g in range(n_groups)]
            log_betas, sums = scores(zs, [ok] * n_groups)
            ws = [jnp.where(ok, jnp.exp(lb + sm[:, :LANES] + c), 0.0) for lb, sm, c in zip(log_betas, sums, carries)]
            accs = tuple(acc + _dot(w.astype(BF16), key_rows(v_ref, j, g)) for g, (acc, w) in enumerate(zip(accs, ws)))
            carries = tuple(c + sm[:, LANES:] for c, sm in zip(carries, sums))
            return j - 1, alive(carries), carries, accs

        rest = qi * n_sub + n_sub - 1 - ATTN_UNROLLED_STEPS
        state = lax.while_loop(cond, body, (rest, alive(carries), tuple(carries), tuple(accs)))
        for g, acc in enumerate(state[-1]):
            o_ref[:, g * LANES:(g + 1) * LANES] = acc.astype(o_ref.dtype)

    pad_free = (qi * n_sub - (ATTN_UNROLLED_STEPS - 1)) * KEY_STEP >= PAD
    pl.when(pad_free)(functools.partial(sweep, True))
    pl.when(jnp.logical_not(pad_free))(functools.partial(sweep, False))


def sb_attention(q, k, v, batch, groups):
    m, d = q.shape
    seq_rows = m // batch
    assert seq_rows % ATTN_Q == 0 and ATTN_Q % KEY_STEP == 0
    nq = seq_rows // ATTN_Q
    width = groups * LANES
    whole_seq = pl.BlockSpec((seq_rows, width), lambda b, p, i: (b, p), pipeline_mode=pl.Buffered(ATTN_KV_BUFFERS))
    return pl.pallas_call(
        _sb_attn_kernel,
        grid=(batch, d // width, nq),
        in_specs=[pl.BlockSpec((ATTN_Q, width), lambda b, p, i: (b * nq + i, p)), whole_seq, whole_seq],
        out_specs=pl.BlockSpec((ATTN_Q, width), lambda b, p, i: (b * nq + i, p)),
        out_shape=jax.ShapeDtypeStruct((m, d), BF16),
        compiler_params=_params(("parallel", "parallel", "arbitrary")),
        name="sb_attention",
    )(q, k, v)


def _row_tile(m, want):
    tm = want
    while m % tm:
        tm //= 2
    return tm


def kernel(x, meta_tokens, norm_mix_g, norm_ffn_g, ffn_up, ffn_conv_w, ffn_conv_b, ffn_down, rw_mix, rw_wr, rw_wk, rw_wv, rw_wo, rw_w0, rw_w1, rw_w2, rw_a0, rw_a1, rw_a2, rw_g1, rw_g2, rw_kk, rw_ka, rw_rk, rw_lnx_g, rw_lnx_b, rw_v0, rw_v1, rw_v2, kv_norm_g, sb_wk, sb_wv, sb_wq, sb_wo, final_norm_g):
    bsz, seq, d = x.shape
    depth = norm_mix_g.shape[0]
    n_a = rw_wr.shape[0]
    d_ff = ffn_down.shape[1]
    seq_rows = PAD + N_META + seq
    assert seq % BLOCK == 0 and d % LANES == 0 and seq_rows % CHUNK == 0
    m = bsz * seq_rows
    tm_big = _row_tile(m, 512)
    tm_ffn = 640 if m % 640 == 0 else tm_big
    tm_proj = _row_tile(m, 512)
    tf = 256 if d_ff % 256 == 0 else LANES

    meta = jnp.broadcast_to(meta_tokens.astype(x.dtype)[None], (bsz, N_META, d))
    h = jnp.concatenate([jnp.zeros((bsz, PAD, d), x.dtype), meta, x], axis=1).reshape(m, d)

    v_first = None
    k_sh = v_sh = None
    for layer in range(depth):
        if layer < n_a:
            i = layer
            vres = (None, None, None) if i == 0 else (rw_v0[i - 1], rw_v1[i - 1], rw_v2[i - 1])
            r, k, v, lw, a, g = rwkv_proj(
                h, norm_mix_g[layer], rw_mix[i], rw_w0[i], rw_a0[i], vres[0], rw_wr[i], rw_wk[i], rw_wv[i],
                rw_w1[i], rw_w2[i], rw_a1[i], rw_a2[i], rw_g1[i], rw_g2[i], vres[1], vres[2],
                None if i == 0 else v_first, seq_rows, tm_proj)
            if i == 0:
                v_first = v
            y = rwkv_recurrence(r, k, v, lw, a, rw_kk[i], rw_ka[i], rw_rk[i].reshape(-1),
                                rw_lnx_g[i], rw_lnx_b[i], bsz)
            mixer = (y, g, rw_wo[i])
        else:
            j = layer - n_a
            if layer == n_a:
                q, k_sh, v_sh = norm_proj(
                    h, [norm_mix_g[layer], kv_norm_g, kv_norm_g],
                    [sb_wq[j].astype(BF16), sb_wk.astype(BF16), sb_wv.astype(BF16)], BF16, tm_big)
            else:
                (q,) = norm_proj(h, [norm_mix_g[layer]], [sb_wq[j].astype(BF16)], BF16, tm_big)
            o = sb_attention(q, k_sh, v_sh, bsz, ATTN_GROUPS if (d // LANES) % ATTN_GROUPS == 0 else 1)
            mixer = (o, None, sb_wo[j])
        h = mix_ffn(mixer[0], mixer[1], h, mixer[2], norm_ffn_g[layer], ffn_up[layer], ffn_conv_w[layer],
                    ffn_conv_b[layer], ffn_down[layer], final_norm_g if layer == depth - 1 else None,
                    seq_rows, tm_ffn, tf)
    return h.reshape(bsz, seq_rows, d)[:, PAD + N_META:, :]
```

```python
import functools
import math

import jax
import jax.numpy as jnp
from jax import lax
from jax.experimental import pallas as pl
from jax.experimental.pallas import tpu as pltpu

HEAD_DIM = 64
N_META = 16
BLOCK = 128
PAD = BLOCK - N_META
CONV_WIDTH = 3
GN_EPS = 64e-5
RMS_EPS = 1e-6

LANES = 128
CHUNK = 64
CONV_HALO = 8
SHIFT_HALO = 8
REC_SEQS = 4
ATTN_GROUPS = 8
ATTN_Q = BLOCK
KEY_STEP = HEAD_DIM
ATTN_UNROLLED_STEPS = 4
ATTN_KV_BUFFERS = 1
LOG_W_FLOOR = -100.0
VMEM_LIMIT = 56 * 1024 * 1024

F32 = jnp.float32
BF16 = jnp.bfloat16


def _dot(a, b):
    return jnp.dot(a, b, preferred_element_type=F32)


def _dot_nt(a, b):
    return lax.dot_general(a, b, (((1,), (1,)), ((), ())), preferred_element_type=F32)


def _dot_tn(a, b):
    return lax.dot_general(a, b, (((0,), (0,)), ((), ())), preferred_element_type=F32)


def _split_bf16(x):
    hi = x.astype(BF16)
    return hi, (x - hi.astype(F32)).astype(BF16)


def _rms_hat(x):
    return x * lax.rsqrt(jnp.mean(x * x, axis=-1, keepdims=True) + RMS_EPS)


def _params(sem):
    return pltpu.CompilerParams(dimension_semantics=sem, vmem_limit_bytes=VMEM_LIMIT)


def _row_in_seq(tile_rows, tile_index, seq_rows):
    rows = tile_index * tile_rows + lax.broadcasted_iota(jnp.int32, (tile_rows, 1), 0)
    return rows % seq_rows


def _norm_proj_kernel(n_groups, x_ref, *refs):
    g_refs = refs[:n_groups]
    w_refs = refs[n_groups:2 * n_groups]
    o_refs = refs[2 * n_groups:]
    xhat = _rms_hat(x_ref[...])
    for g_ref, w_ref, o_ref in zip(g_refs, w_refs, o_refs):
        xn = (xhat * g_ref[...]).astype(BF16)
        o_ref[...] = _dot(xn, w_ref[...]).astype(o_ref.dtype)


def norm_proj(h, gains, weights, out_dtype, tm):
    m, d = h.shape
    n = len(gains)
    in_specs = [pl.BlockSpec((tm, d), lambda i: (i, 0))]
    in_specs += [pl.BlockSpec((1, d), lambda i: (0, 0)) for _ in gains]
    in_specs += [pl.BlockSpec(w.shape, lambda i: (0, 0)) for w in weights]
    out_specs = [pl.BlockSpec((tm, w.shape[1]), lambda i: (i, 0)) for w in weights]
    out_shape = [jax.ShapeDtypeStruct((m, w.shape[1]), out_dtype) for w in weights]
    return pl.pallas_call(
        functools.partial(_norm_proj_kernel, n),
        grid=(m // tm,),
        in_specs=in_specs, out_specs=out_specs, out_shape=out_shape,
        compiler_params=_params(("parallel",)),
        name="norm_proj",
    )(h, *[g.reshape(1, d) for g in gains], *weights)


def _rwkv_proj_kernel(has_vres, seq_rows, tm, *refs):
    if has_vres:
        (h_ref, halo_ref, gain_ref, mix_ref, vec_ref, wr_ref, wk_ref, wv_ref, w1_ref, w2_ref,
         a1_ref, a2_ref, g1_ref, g2_ref, v1_ref, v2_ref, vfirst_ref,
         r_ref, k_ref, v_ref, lw_ref, a_ref, g_ref) = refs
    else:
        (h_ref, halo_ref, gain_ref, mix_ref, vec_ref, wr_ref, wk_ref, wv_ref, w1_ref, w2_ref,
         a1_ref, a2_ref, g1_ref, g2_ref,
         r_ref, k_ref, v_ref, lw_ref, a_ref, g_ref) = refs
    i = pl.program_id(0)
    gain = gain_ref[...]
    hn = _rms_hat(h_ref[...]) * gain
    prev_last = (_rms_hat(halo_ref[...]) * gain)[SHIFT_HALO - 1:SHIFT_HALO, :]
    row = lax.broadcasted_iota(jnp.int32, (tm, 1), 0)
    shifted = jnp.where(row == 0, prev_last, pltpu.roll(hn, 1, 0))
    shifted = jnp.where(_row_in_seq(tm, i, seq_rows) == 0, 0.0, shifted)
    xx = shifted - hn

    def mixed(j):
        return (hn + xx * mix_ref[j:j + 1, :]).astype(BF16)

    w0, a0, v0 = vec_ref[0:1, :], vec_ref[1:2, :], vec_ref[2:3, :]
    xv = mixed(3)
    r_ref[...] = _dot(mixed(0), wr_ref[...])
    t_w = _dot(mixed(1), w1_ref[...])
    k_ref[...] = _dot(mixed(2), wk_ref[...])
    t_a = _dot(mixed(4), a1_ref[...])
    v = _dot(xv, wv_ref[...])
    t_g = _dot(mixed(5), g1_ref[...])
    if has_vres:
        t_v = _dot(xv, v1_ref[...])
    w_in = w0 + _dot(jnp.tanh(t_w).astype(BF16), w2_ref[...])
    lw_ref[...] = -math.exp(-0.5) * jax.nn.sigmoid(w_in)
    a_ref[...] = jax.nn.sigmoid(a0 + _dot(t_a.astype(BF16), a2_ref[...]))
    g_ref[...] = _dot(jax.nn.sigmoid(t_g).astype(BF16), g2_ref[...])
    if has_vres:
        gate = jax.nn.sigmoid(v0 + _dot(t_v.astype(BF16), v2_ref[...]))
        v = v + (vfirst_ref[...] - v) * gate
    v_ref[...] = v


def _pad_lora(w_in, w_out):
    hid = w_in.shape[1]
    hid_p = -(-hid // LANES) * LANES
    return (jnp.pad(w_in, ((0, 0), (0, hid_p - hid))).astype(BF16),
            jnp.pad(w_out, ((0, hid_p - hid), (0, 0))).astype(BF16))


def rwkv_proj(h, gain, mix, w0, a0, v0, wr, wk, wv, w1, w2, a1, a2, g1, g2, v1, v2, v_first,
              seq_rows, tm):
    m, d = h.shape
    has_vres = v_first is not None
    mix_p = jnp.pad(mix, ((0, 8 - mix.shape[0]), (0, 0)))
    vecs = jnp.stack([w0, a0, v0 if has_vres else jnp.zeros_like(w0)])
    vecs = jnp.pad(vecs, ((0, 8 - vecs.shape[0]), (0, 0)))
    w1p, w2p = _pad_lora(w1, w2)
    a1p, a2p = _pad_lora(a1, a2)
    g1p, g2p = _pad_lora(g1, g2)
    full = lambda a: pl.BlockSpec(a.shape, lambda i: (0, 0), pipeline_mode=pl.Buffered(1))
    tile = pl.BlockSpec((tm, d), lambda i: (i, 0))
    halo = pl.BlockSpec((SHIFT_HALO, d), lambda i: (jnp.maximum(i * (tm // SHIFT_HALO) - 1, 0), 0))
    args = [h, h, gain.reshape(1, d), mix_p, vecs, wr.astype(BF16), wk.astype(BF16), wv.astype(BF16),
            w1p, w2p, a1p, a2p, g1p, g2p]
    in_specs = [tile, halo] + [full(a) for a in args[2:]]
    if has_vres:
        v1p, v2p = _pad_lora(v1, v2)
        args += [v1p, v2p, v_first]
        in_specs += [full(v1p), full(v2p), tile]
    return pl.pallas_call(
        functools.partial(_rwkv_proj_kernel, has_vres, seq_rows, tm),
        grid=(m // tm,),
        in_specs=in_specs,
        out_specs=[tile] * 6,
        out_shape=[jax.ShapeDtypeStruct((m, d), F32)] * 6,
        compiler_params=_params(("parallel",)),
        name="rwkv_proj",
    )(*args)


def _rwkv_rec_kernel(r_ref, k_ref, v_ref, lw_ref, a_ref, kk_ref, ka_ref, rk_ref, lg_ref, lb_ref,
                     y_ref, s_ref):
    c = CHUNK
    n_seq = r_ref.shape[0]
    n_pairs = r_ref.shape[2] // LANES

    @pl.when(pl.program_id(1) == 0)
    def _():
        s_ref[...] = jnp.zeros_like(s_ref)

    assert c == HEAD_DIM, "a head's chunk-by-chunk matrices fill exactly its lanes"
    lane = lax.broadcasted_iota(jnp.int32, (c, LANES), 1)
    row = lax.broadcasted_iota(jnp.int32, (c, LANES), 0)
    head0 = lane < HEAD_DIM
    strict_s = row > lane % c
    incl_s = row >= lane % c
    eye_s = (row == lane % c).astype(F32)
    row_l = lax.broadcasted_iota(jnp.int32, (LANES, LANES), 0)
    col_l = lax.broadcasted_iota(jnp.int32, (LANES, LANES), 1)
    same_head = (row_l // HEAD_DIM) == (col_l // HEAD_DIM)
    ones_bd2 = jnp.concatenate([same_head, same_head], axis=0).astype(BF16)
    row_t = lax.broadcasted_iota(jnp.int32, (c, 2 * c), 0)
    col_t = lax.broadcasted_iota(jnp.int32, (c, 2 * c), 1)
    tri2 = (row_t >= col_t % c).astype(BF16)

    def head_sums(xs):
        x = jnp.concatenate(xs, axis=0)
        out = _dot(jnp.concatenate(_split_bf16(x), axis=1), ones_bd2)
        return [out[i * c:(i + 1) * c] for i in range(len(xs))]

    def stack(x):
        x = x.astype(BF16)
        zero = jnp.zeros_like(x)
        return jnp.concatenate([jnp.where(head0, x, zero), jnp.where(head0, zero, x)], axis=0)

    cum_all = [_dot(tri2, jnp.concatenate(_split_bf16(lw_ref[q]), axis=0)) for q in range(n_seq)]

    pairs = range(n_seq * n_pairs)
    seqs = [p // n_pairs for p in pairs]
    sls = [slice((p % n_pairs) * LANES, (p % n_pairs + 1) * LANES) for p in pairs]
    ins_ = [(r_ref[q, :, sl], k_ref[q, :, sl], v_ref[q, :, sl], lw_ref[q, :, sl], a_ref[q, :, sl])
            for q, sl in zip(seqs, sls)]
    kku_ = [k_raw * kk_ref[:, sl] for (_, k_raw, _, _, _), sl in zip(ins_, sls)]
    kk_sq_ = head_sums([kk * kk for kk in kku_])
    r_, v_, k_, atrt_, bhkh_, pe_, aa_ = [], [], [], [], [], [], []
    for (r, k_raw, v, lw, a_sig), q, sl, kk, kk_sq in zip(ins_, seqs, sls, kku_, kk_sq_):
        cum = cum_all[q][:, sl]
        kk = kk * lax.rsqrt(jnp.maximum(kk_sq, 1e-24))
        k = k_raw * (1.0 + (a_sig - 1.0) * ka_ref[:, sl])
        b_vec = kk * a_sig
        cum_last = cum[c - 1:c, :]
        inv_p = jnp.exp(-cum)
        to_end = jnp.exp(cum_last - cum)
        atrt = jnp.concatenate([-kk * jnp.exp(cum - lw), r * jnp.exp(cum)], axis=0).astype(BF16)
        aa_.append(_dot_nt(atrt, jnp.concatenate([stack(b_vec * inv_p), stack(k * inv_p)], axis=0)))
        r_.append(r); v_.append(v); k_.append(k); atrt_.append(atrt)
        bhkh_.append(jnp.concatenate([b_vec * to_end, k * to_end], axis=0).astype(BF16))
        pe_.append(jnp.exp(cum_last))

    a_ab_ = [jnp.where(strict_s, aa[:c, :LANES], 0.0) for aa in aa_]
    a_ak_ = [jnp.where(strict_s, aa[:c, LANES:], 0.0) for aa in aa_]
    a_r_ = [jnp.concatenate([jnp.where(incl_s, aa[c:, :LANES], 0.0), jnp.where(incl_s, aa[c:, LANES:], 0.0)], axis=1)
            for aa in aa_]

    t_inv_ = [eye_s + a for a in a_ab_]
    pw_ = [_dot(a.astype(BF16), stack(a)) for a in a_ab_]
    n_steps = c.bit_length() - 2
    for step in range(n_steps):
        if step < n_steps - 1:
            tp_ = [_dot(jnp.concatenate([t, pw], axis=0).astype(BF16), stack(pw)) for t, pw in zip(t_inv_, pw_)]
            t_inv_ = [t + tp[:c] for t, tp in zip(t_inv_, tp_)]
            pw_ = [tp[c:] for tp in tp_]
        else:
            t_inv_ = [t + _dot(t.astype(BF16), stack(pw)) for t, pw in zip(t_inv_, pw_)]

    vst_ = [stack(v) for v in v_]
    akv_ = [_dot(a_ak.astype(BF16), vst) for a_ak, vst in zip(a_ak_, vst_)]
    s0_ = [s_ref[p] for p in pairs]
    ars_ = [_dot_nt(atrt, s0.astype(BF16)) for atrt, s0 in zip(atrt_, s0_)]
    u_ = [_dot(t.astype(BF16), stack(ars[:c] + akv)) for t, ars, akv in zip(t_inv_, ars_, akv_)]
    y_ = [ars[c:] + _dot(a_r.astype(BF16), jnp.concatenate([stack(u), vst], axis=0))
          for ars, a_r, u, vst in zip(ars_, a_r_, u_, vst_)]
    upd_ = [_dot_tn(jnp.concatenate([u, v], axis=0).astype(BF16), bhkh) for u, v, bhkh in zip(u_, v_, bhkh_)]
    for p in pairs:
        s_ref[p] = s0_[p] * pe_[p] + jnp.where(same_head, upd_[p], 0.0)

    inv_n = 1.0 / HEAD_DIM
    dev_ = [y - mu * inv_n for y, mu in zip(y_, head_sums(y_))]
    var_ = [ss * inv_n for ss in head_sums([dev * dev for dev in dev_])]
    rk_ = head_sums([r * k * rk_ref[:, sl] for r, k, sl in zip(r_, k_, sls)])
    for p, sl in enumerate(sls):
        y_ref[seqs[p], :, sl] = (dev_[p] * lax.rsqrt(var_[p] + GN_EPS) * lg_ref[:, sl] + lb_ref[:, sl]
                                 + rk_[p] * v_[p])


def rwkv_recurrence(r, k, v, lw, a, k_k, k_a, r_k, lnx_g, lnx_b, batch):
    m, d = r.shape
    seq_rows = m // batch
    n_seq = math.gcd(batch, REC_SEQS)
    tile = pl.BlockSpec((n_seq, CHUNK, d), lambda b, c: (b, c, 0))
    vec = pl.BlockSpec((1, d), lambda b, c: (0, 0))
    vecs = [t.reshape(1, d) for t in (k_k, k_a, r_k, lnx_g, lnx_b)]
    seqs = [t.reshape(batch, seq_rows, d) for t in (r, k, v, lw, a)]
    return pl.pallas_call(
        _rwkv_rec_kernel,
        grid=(batch // n_seq, seq_rows // CHUNK),
        in_specs=[tile] * 5 + [vec] * 5,
        out_specs=tile,
        out_shape=jax.ShapeDtypeStruct((batch, seq_rows, d), F32),
        scratch_shapes=[pltpu.VMEM((n_seq * (d // LANES), LANES, LANES), F32)],
        compiler_params=_params(("parallel", "arbitrary")),
        name="rwkv_recurrence",
    )(*seqs, *vecs).reshape(m, d)


def _mix_ffn_kernel(has_gate, has_final, seq_rows, tm, tf, *refs):
    refs = list(refs)
    y_ref = refs.pop(0)
    g_ref = refs.pop(0) if has_gate else None
    h_ref, wo_ref, gain_ref, wup_ref, cw_ref, cb_ref, wd_ref = refs[:7]
    fgain_ref = refs[7] if has_final else None
    o_ref, hmid_ref, utail_ref, act_ref = refs[-4:]
    i = pl.program_id(0)
    d_ff = wd_ref.shape[0]
    keep = _row_in_seq(tm, i, seq_rows) >= PAD

    @pl.when(i == 0)
    def _():
        utail_ref[...] = jnp.zeros_like(utail_ref)

    y = y_ref[...] if g_ref is None else y_ref[...] * g_ref[...]
    hmid_ref[...] = jnp.where(keep, h_ref[...] + _dot(y.astype(BF16), wo_ref[...]), 0.0)
    xn = (_rms_hat(hmid_ref[...]) * gain_ref[...]).astype(BF16)

    def conv(col):
        u_tile = _dot(xn, wup_ref[:, col:col + tf])
        u = jnp.concatenate([utail_ref[:, col:col + tf], u_tile], axis=0)
        utail_ref[:, col:col + tf] = u_tile[tm - CONV_HALO:, :]
        out = cb_ref[:, col:col + tf] + u_tile * cw_ref[CONV_WIDTH - 1:CONV_WIDTH, col:col + tf]
        for tap in range(CONV_WIDTH - 1):
            back = CONV_WIDTH - 1 - tap
            out = out + pltpu.roll(u, back, 0)[CONV_HALO:, :] * cw_ref[tap:tap + 1, col:col + tf]
        return out

    for col in range(0, d_ff, tf):
        act_ref[:, col:col + tf] = (jax.nn.silu(conv(col)) * conv(d_ff + col)).astype(BF16)

    h_out = jnp.where(keep, hmid_ref[...] + _dot(act_ref[...], wd_ref[...]), 0.0)
    o_ref[...] = h_out if fgain_ref is None else _rms_hat(h_out) * fgain_ref[...]


def mix_ffn(y, gate, h, w_o, gain, w_up, conv_w, conv_b, w_down, final_gain, seq_rows, tm, tf):
    m, d = h.shape
    d_ff = w_down[0].shape[1]
    cw = jnp.pad(conv_w, ((0, 8 - CONV_WIDTH), (0, 0)))
    cb = conv_b.reshape(1, 2 * d_ff)
    tile = pl.BlockSpec((tm, d), lambda i: (i, 0))
    const = lambda shape: pl.BlockSpec(shape, lambda i: (0, 0), pipeline_mode=pl.Buffered(1))
    layer_of = lambda w: pl.BlockSpec((None,) + w[0].shape[1:], lambda i: (w[1], 0, 0), pipeline_mode=pl.Buffered(1))
    seqs = [y] + ([gate] if gate is not None else []) + [h]
    finals = [] if final_gain is None else [final_gain.reshape(1, d)]
    return pl.pallas_call(
        functools.partial(_mix_ffn_kernel, gate is not None, final_gain is not None, seq_rows, tm, tf),
        grid=(m // tm,),
        in_specs=[tile] * len(seqs) + [
            layer_of(w_o), const((1, d)), layer_of(w_up), const((8, 2 * d_ff)), const((1, 2 * d_ff)),
            layer_of(w_down)] + [const((1, d))] * len(finals),
        out_specs=tile,
        out_shape=jax.ShapeDtypeStruct((m, d), F32),
        scratch_shapes=[pltpu.VMEM((tm, d), F32), pltpu.VMEM((CONV_HALO, 2 * d_ff), F32),
                        pltpu.VMEM((tm, d_ff), BF16)],
        compiler_params=_params(("arbitrary",)),
        name="mix_ffn",
    )(*seqs, w_o[0], gain.reshape(1, d), w_up[0], cw, cb, w_down[0], *finals)


def _sb_attn_kernel(q_ref, k_ref, v_ref, o_ref):
    qi = pl.program_id(2)
    qb = q_ref.shape[0]
    n_groups = q_ref.shape[1] // LANES
    scale = HEAD_DIM ** -0.5
    assert math.frexp(scale)[0] == 0.5, "the score scale is folded into bf16 q, which needs a power of two"
    qs = [q_ref[:, g * LANES:(g + 1) * LANES] * jnp.asarray(scale, BF16) for g in range(n_groups)]
    head0_k = lax.broadcasted_iota(jnp.int32, (KEY_STEP, LANES), 1) < HEAD_DIM
    q_pos = qi * qb + lax.broadcasted_iota(jnp.int32, (qb, LANES), 0)
    row_b = lax.broadcasted_iota(jnp.int32, (2 * LANES, 2 * LANES), 0) % LANES
    col_b = lax.broadcasted_iota(jnp.int32, (2 * LANES, 2 * LANES), 1)
    same_head = (row_b // KEY_STEP) == ((col_b % LANES) // KEY_STEP)
    sum_mat = jnp.where(same_head & ((row_b > col_b) | (col_b >= LANES)), 1.0, 0.0).astype(BF16)

    def stack(x):
        zero = jnp.zeros_like(x)
        return jnp.concatenate([jnp.where(head0_k, x, zero), jnp.where(head0_k, zero, x)], axis=0)

    def key_rows(ref, j, g):
        start = pl.multiple_of(jnp.maximum(j, 0) * KEY_STEP, KEY_STEP)
        return stack(ref[pl.ds(start, KEY_STEP), g * LANES:(g + 1) * LANES])

    def valid_keys(j, pos):
        k_pos = j * KEY_STEP + lax.broadcasted_iota(jnp.int32, pos.shape, 1) % KEY_STEP
        return (k_pos < pos) & (k_pos >= PAD)

    def masked(ok, x):
        return x if ok is None else jnp.where(ok, x, 0.0)

    def scores(zs, valids):
        rows = zs[0].shape[0]
        log_betas = [jnp.minimum(z, 0.0) - jnp.log(1.0 + jnp.exp(-jnp.abs(z))) for z in zs]
        splits = [jnp.concatenate(_split_bf16(masked(ok, lb - z)), axis=1)
                  for lb, z, ok in zip(log_betas, zs, valids)]
        sums = _dot(jnp.concatenate(splits, axis=0), sum_mat)
        return log_betas, [sums[i * rows:(i + 1) * rows] for i in range(len(zs))]

    n_sub = qb // KEY_STEP

    def sweep(pad_free):
        sub_rows = [slice(sb * KEY_STEP, (sb + 1) * KEY_STEP) for sb in range(n_sub)]
        sub_steps = [[qi * n_sub + sb - t for t in range(ATTN_UNROLLED_STEPS)] for sb in range(n_sub)]
        sub_pos = [qi * qb + sb * KEY_STEP + lax.broadcasted_iota(jnp.int32, (KEY_STEP, LANES), 0) for sb in range(n_sub)]
        valids = [[valid_keys(j, pos) if (t == 0 or not pad_free) else None for t, j in enumerate(steps)]
                  for pos, steps in zip(sub_pos, sub_steps)]
        window = [qi * n_sub + n_sub - 1 - t for t in range(ATTN_UNROLLED_STEPS + n_sub - 1)]
        zs, tile_valid = [], []
        for g in range(n_groups):
            k_tiles = [key_rows(k_ref, j, g) for j in window]
            for sb in range(n_sub):
                lo = n_sub - 1 - sb
                keys = jnp.concatenate(k_tiles[lo:lo + ATTN_UNROLLED_STEPS], axis=0)
                z_all = _dot_nt(qs[g][sub_rows[sb]], keys)
                zs += [z_all[:, t * LANES:(t + 1) * LANES] for t in range(ATTN_UNROLLED_STEPS)]
                tile_valid += valids[sb]
        log_betas, sums = scores(zs, tile_valid)
        carries, accs = [], []
        for g in range(n_groups):
            v_tiles = [key_rows(v_ref, j, g) for j in window]
            carry_g, acc_g = [], []
            for sb in range(n_sub):
                carry = jnp.zeros((KEY_STEP, LANES), F32)
                ws = []
                for t in range(ATTN_UNROLLED_STEPS):
                    i = (g * n_sub + sb) * ATTN_UNROLLED_STEPS + t
                    ws.append(masked(valids[sb][t], jnp.exp(log_betas[i] + sums[i][:, :LANES] + carry)).astype(BF16))
                    carry = carry + sums[i][:, LANES:]
                lo = n_sub - 1 - sb
                values = jnp.concatenate(v_tiles[lo:lo + ATTN_UNROLLED_STEPS], axis=0)
                acc_g.append(_dot(jnp.concatenate(ws, axis=1), values))
                carry_g.append(carry)
            accs.append(jnp.concatenate(acc_g, axis=0))
            carries.append(jnp.concatenate(carry_g, axis=0))

        def alive(carries):
            top = carries[0]
            for c in carries[1:]:
                top = jnp.maximum(top, c)
            return (jnp.max(top) > LOG_W_FLOOR).astype(jnp.int32)

        def cond(state):
            j, live, _, _ = state
            return (j >= PAD // KEY_STEP) & (live > 0)

        def body(state):
            j, _, carries, accs = state
            ok = valid_keys(j, q_pos) & (j <= q_pos // KEY_STEP - ATTN_UNROLLED_STEPS)
            zs = [_dot_nt(qs[g], key_rows(k_ref, j, g)) for g in range(n_groups)]
            log_betas, sums = scores(zs, [ok] * n_groups)
            ws = [jnp.where(ok, jnp.exp(lb + sm[:, :LANES] + c), 0.0) for lb, sm, c in zip(log_betas, sums, carries)]
            accs = tuple(acc + _dot(w.astype(BF16), key_rows(v_ref, j, g)) for g, (acc, w) in enumerate(zip(accs, ws)))
            carries = tuple(c + sm[:, LANES:] for c, sm in zip(carries, sums))
            return j - 1, alive(carries), carries, accs

        rest = qi * n_sub + n_sub - 1 - ATTN_UNROLLED_STEPS
        state = lax.while_loop(cond, body, (rest, alive(carries), tuple(carries), tuple(accs)))
        for g, acc in enumerate(state[-1]):
            o_ref[:, g * LANES:(g + 1) * LANES] = acc.astype(o_ref.dtype)

    pad_free = (qi * n_sub - (ATTN_UNROLLED_STEPS - 1)) * KEY_STEP >= PAD
    pl.when(pad_free)(functools.partial(sweep, True))
    pl.when(jnp.logical_not(pad_free))(functools.partial(sweep, False))


def sb_attention(q, k, v, batch, groups):
    m, d = q.shape
    seq_rows = m // batch
    assert seq_rows % ATTN_Q == 0 and ATTN_Q % KEY_STEP == 0
    nq = seq_rows // ATTN_Q
    width = groups * LANES
    whole_seq = pl.BlockSpec((seq_rows, width), lambda b, p, i: (b, p), pipeline_mode=pl.Buffered(ATTN_KV_BUFFERS))
    return pl.pallas_call(
        _sb_attn_kernel,
        grid=(batch, d // width, nq),
        in_specs=[pl.BlockSpec((ATTN_Q, width), lambda b, p, i: (b * nq + i, p)), whole_seq, whole_seq],
        out_specs=pl.BlockSpec((ATTN_Q, width), lambda b, p, i: (b * nq + i, p)),
        out_shape=jax.ShapeDtypeStruct((m, d), BF16),
        compiler_params=_params(("parallel", "parallel", "arbitrary")),
        name="sb_attention",
    )(q, k, v)


def _row_tile(m, want):
    tm = want
    while m % tm:
        tm //= 2
    return tm


def kernel(x, meta_tokens, norm_mix_g, norm_ffn_g, ffn_up, ffn_conv_w, ffn_conv_b, ffn_down, rw_mix, rw_wr, rw_wk, rw_wv, rw_wo, rw_w0, rw_w1, rw_w2, rw_a0, rw_a1, rw_a2, rw_g1, rw_g2, rw_kk, rw_ka, rw_rk, rw_lnx_g, rw_lnx_b, rw_v0, rw_v1, rw_v2, kv_norm_g, sb_wk, sb_wv, sb_wq, sb_wo, final_norm_g):
    bsz, seq, d = x.shape
    depth = norm_mix_g.shape[0]
    n_a = rw_wr.shape[0]
    d_ff = ffn_down.shape[1]
    seq_rows = PAD + N_META + seq
    assert seq % BLOCK == 0 and d % LANES == 0 and seq_rows % CHUNK == 0
    m = bsz * seq_rows
    tm_big = _row_tile(m, 512)
    tm_ffn = 640 if m % 640 == 0 else tm_big
    tm_proj = _row_tile(m, 512)
    tf = 256 if d_ff % 256 == 0 else LANES

    meta = jnp.broadcast_to(meta_tokens.astype(x.dtype)[None], (bsz, N_META, d))
    h = jnp.concatenate([jnp.zeros((bsz, PAD, d), x.dtype), meta, x], axis=1).reshape(m, d)

    ffn_up_b, ffn_down_b, rw_wo_b, sb_wo_b = (w.astype(BF16) for w in (ffn_up, ffn_down, rw_wo, sb_wo))

    v_first = None
    k_sh = v_sh = None
    for layer in range(depth):
        if layer < n_a:
            i = layer
            vres = (None, None, None) if i == 0 else (rw_v0[i - 1], rw_v1[i - 1], rw_v2[i - 1])
            r, k, v, lw, a, g = rwkv_proj(
                h, norm_mix_g[layer], rw_mix[i], rw_w0[i], rw_a0[i], vres[0], rw_wr[i], rw_wk[i], rw_wv[i],
                rw_w1[i], rw_w2[i], rw_a1[i], rw_a2[i], rw_g1[i], rw_g2[i], vres[1], vres[2],
                None if i == 0 else v_first, seq_rows, tm_proj)
            if i == 0:
                v_first = v
            y = rwkv_recurrence(r, k, v, lw, a, rw_kk[i], rw_ka[i], rw_rk[i].reshape(-1),
                                rw_lnx_g[i], rw_lnx_b[i], bsz)
            mixer = (y, g, (rw_wo_b, i))
        else:
            j = layer - n_a
            if layer == n_a:
                q, k_sh, v_sh = norm_proj(
                    h, [norm_mix_g[layer], kv_norm_g, kv_norm_g],
                    [sb_wq[j].astype(BF16), sb_wk.astype(BF16), sb_wv.astype(BF16)], BF16, tm_big)
            else:
                (q,) = norm_proj(h, [norm_mix_g[layer]], [sb_wq[j].astype(BF16)], BF16, tm_big)
            o = sb_attention(q, k_sh, v_sh, bsz, ATTN_GROUPS if (d // LANES) % ATTN_GROUPS == 0 else 1)
            mixer = (o, None, (sb_wo_b, j))
        h = mix_ffn(mixer[0], mixer[1], h, mixer[2], norm_ffn_g[layer], (ffn_up_b, layer), ffn_conv_w[layer],
                    ffn_conv_b[layer], (ffn_down_b, layer), final_norm_g if layer == depth - 1 else None,
                    seq_rows, tm_ffn, tf)
    return h.reshape(bsz, seq_rows, d)[:, PAD + N_META:, :]
```

```python
import functools
import math

import jax
import jax.numpy as jnp
from jax import lax
from jax.experimental import pallas as pl
from jax.experimental.pallas import tpu as pltpu

HEAD_DIM = 64
N_META = 16
BLOCK = 128
PAD = BLOCK - N_META
CONV_WIDTH = 3
GN_EPS = 64e-5
RMS_EPS = 1e-6

LANES = 128
CHUNK = 64
CONV_HALO = 8
SHIFT_HALO = 8
REC_SEQS = 4
ATTN_GROUPS = 8
ATTN_Q = BLOCK
KEY_STEP = HEAD_DIM
ATTN_UNROLLED_STEPS = 4
ATTN_KV_BUFFERS = 1
LOG_W_FLOOR = -100.0
VMEM_LIMIT = 56 * 1024 * 1024

F32 = jnp.float32
BF16 = jnp.bfloat16


def _dot(a, b):
    return jnp.dot(a, b, preferred_element_type=F32)


def _dot_nt(a, b):
    return lax.dot_general(a, b, (((1,), (1,)), ((), ())), preferred_element_type=F32)


def _dot_tn(a, b):
    return lax.dot_general(a, b, (((0,), (0,)), ((), ())), preferred_element_type=F32)


def _split_bf16(x):
    hi = x.astype(BF16)
    return hi, (x - hi.astype(F32)).astype(BF16)


def _rms_hat(x):
    return x * lax.rsqrt(jnp.mean(x * x, axis=-1, keepdims=True) + RMS_EPS)


def _params(sem):
    return pltpu.CompilerParams(dimension_semantics=sem, vmem_limit_bytes=VMEM_LIMIT)


def _row_in_seq(tile_rows, tile_index, seq_rows):
    rows = tile_index * tile_rows + lax.broadcasted_iota(jnp.int32, (tile_rows, 1), 0)
    return rows % seq_rows


def _norm_proj_kernel(n_groups, x_ref, *refs):
    g_refs = refs[:n_groups]
    w_refs = refs[n_groups:2 * n_groups]
    o_refs = refs[2 * n_groups:]
    xhat = _rms_hat(x_ref[...])
    for g_ref, w_ref, o_ref in zip(g_refs, w_refs, o_refs):
        xn = (xhat * g_ref[...]).astype(BF16)
        o_ref[...] = _dot(xn, w_ref[...]).astype(o_ref.dtype)


def norm_proj(h, gains, weights, out_dtype, tm):
    m, d = h.shape
    n = len(gains)
    in_specs = [pl.BlockSpec((tm, d), lambda i: (i, 0))]
    in_specs += [pl.BlockSpec((1, d), lambda i: (0, 0)) for _ in gains]
    in_specs += [pl.BlockSpec(w.shape, lambda i: (0, 0)) for w in weights]
    out_specs = [pl.BlockSpec((tm, w.shape[1]), lambda i: (i, 0)) for w in weights]
    out_shape = [jax.ShapeDtypeStruct((m, w.shape[1]), out_dtype) for w in weights]
    return pl.pallas_call(
        functools.partial(_norm_proj_kernel, n),
        grid=(m // tm,),
        in_specs=in_specs, out_specs=out_specs, out_shape=out_shape,
        compiler_params=_params(("parallel",)),
        name="norm_proj",
    )(h, *[g.reshape(1, d) for g in gains], *weights)


def _rwkv_proj_kernel(has_vres, seq_rows, tm, *refs):
    if has_vres:
        (h_ref, halo_ref, gain_ref, mix_ref, vec_ref, wr_ref, wk_ref, wv_ref, w1_ref, w2_ref,
         a1_ref, a2_ref, g1_ref, g2_ref, v1_ref, v2_ref, vfirst_ref,
         r_ref, k_ref, v_ref, lw_ref, a_ref, g_ref) = refs
    else:
        (h_ref, halo_ref, gain_ref, mix_ref, vec_ref, wr_ref, wk_ref, wv_ref, w1_ref, w2_ref,
         a1_ref, a2_ref, g1_ref, g2_ref,
         r_ref, k_ref, v_ref, lw_ref, a_ref, g_ref) = refs
    i = pl.program_id(0)
    gain = gain_ref[...]
    hn = _rms_hat(h_ref[...]) * gain
    prev_last = (_rms_hat(halo_ref[...]) * gain)[SHIFT_HALO - 1:SHIFT_HALO, :]
    row = lax.broadcasted_iota(jnp.int32, (tm, 1), 0)
    shifted = jnp.where(row == 0, prev_last, pltpu.roll(hn, 1, 0))
    shifted = jnp.where(_row_in_seq(tm, i, seq_rows) == 0, 0.0, shifted)
    xx = shifted - hn

    def mixed(j):
        return (hn + xx * mix_ref[j:j + 1, :]).astype(BF16)

    w0, a0, v0 = vec_ref[0:1, :], vec_ref[1:2, :], vec_ref[2:3, :]
    xv = mixed(3)
    r_ref[...] = _dot(mixed(0), wr_ref[...])
    t_w = _dot(mixed(1), w1_ref[...])
    k_ref[...] = _dot(mixed(2), wk_ref[...])
    t_a = _dot(mixed(4), a1_ref[...])
    v = _dot(xv, wv_ref[...])
    t_g = _dot(mixed(5), g1_ref[...])
    if has_vres:
        t_v = _dot(xv, v1_ref[...])
    w_in = w0 + _dot(jnp.tanh(t_w).astype(BF16), w2_ref[...])
    lw_ref[...] = -math.exp(-0.5) * jax.nn.sigmoid(w_in)
    a_ref[...] = jax.nn.sigmoid(a0 + _dot(t_a.astype(BF16), a2_ref[...]))
    g_ref[...] = _dot(jax.nn.sigmoid(t_g).astype(BF16), g2_ref[...])
    if has_vres:
        gate = jax.nn.sigmoid(v0 + _dot(t_v.astype(BF16), v2_ref[...]))
        v = v + (vfirst_ref[...] - v) * gate
    v_ref[...] = v


def _pad_lora(w_in, w_out):
    hid = w_in.shape[1]
    hid_p = -(-hid // LANES) * LANES
    return (jnp.pad(w_in, ((0, 0), (0, hid_p - hid))).astype(BF16),
            jnp.pad(w_out, ((0, hid_p - hid), (0, 0))).astype(BF16))


def rwkv_proj(h, gain, mix, w0, a0, v0, wr, wk, wv, w1, w2, a1, a2, g1, g2, v1, v2, v_first,
              seq_rows, tm):
    m, d = h.shape
    has_vres = v_first is not None
    mix_p = jnp.pad(mix, ((0, 8 - mix.shape[0]), (0, 0)))
    vecs = jnp.stack([w0, a0, v0 if has_vres else jnp.zeros_like(w0)])
    vecs = jnp.pad(vecs, ((0, 8 - vecs.shape[0]), (0, 0)))
    w1p, w2p = _pad_lora(w1, w2)
    a1p, a2p = _pad_lora(a1, a2)
    g1p, g2p = _pad_lora(g1, g2)
    full = lambda a: pl.BlockSpec(a.shape, lambda i: (0, 0), pipeline_mode=pl.Buffered(1))
    layer_of = lambda w: pl.BlockSpec((None,) + w[0].shape[1:], lambda i: (w[1], 0, 0), pipeline_mode=pl.Buffered(1))
    tile = pl.BlockSpec((tm, d), lambda i: (i, 0))
    halo = pl.BlockSpec((SHIFT_HALO, d), lambda i: (jnp.maximum(i * (tm // SHIFT_HALO) - 1, 0), 0))
    args = [h, h, gain.reshape(1, d), mix_p, vecs, wr[0], wk[0], wv[0], w1p, w2p, a1p, a2p, g1p, g2p]
    in_specs = ([tile, halo] + [full(a) for a in args[2:5]] + [layer_of(w) for w in (wr, wk, wv)]
                + [full(a) for a in args[8:]])
    if has_vres:
        v1p, v2p = _pad_lora(v1, v2)
        args += [v1p, v2p, v_first]
        in_specs += [full(v1p), full(v2p), tile]
    return pl.pallas_call(
        functools.partial(_rwkv_proj_kernel, has_vres, seq_rows, tm),
        grid=(m // tm,),
        in_specs=in_specs,
        out_specs=[tile] * 6,
        out_shape=[jax.ShapeDtypeStruct((m, d), F32)] * 6,
        compiler_params=_params(("parallel",)),
        name="rwkv_proj",
    )(*args)


def _rwkv_rec_kernel(r_ref, k_ref, v_ref, lw_ref, a_ref, kk_ref, ka_ref, rk_ref, lg_ref, lb_ref,
                     y_ref, s_ref):
    c = CHUNK
    n_seq = r_ref.shape[0]
    n_pairs = r_ref.shape[2] // LANES

    @pl.when(pl.program_id(1) == 0)
    def _():
        s_ref[...] = jnp.zeros_like(s_ref)

    assert c == HEAD_DIM, "a head's chunk-by-chunk matrices fill exactly its lanes"
    lane = lax.broadcasted_iota(jnp.int32, (c, LANES), 1)
    row = lax.broadcasted_iota(jnp.int32, (c, LANES), 0)
    head0 = lane < HEAD_DIM
    strict_s = row > lane % c
    incl_s = row >= lane % c
    eye_s = (row == lane % c).astype(F32)
    row_l = lax.broadcasted_iota(jnp.int32, (LANES, LANES), 0)
    col_l = lax.broadcasted_iota(jnp.int32, (LANES, LANES), 1)
    same_head = (row_l // HEAD_DIM) == (col_l // HEAD_DIM)
    ones_bd2 = jnp.concatenate([same_head, same_head], axis=0).astype(BF16)
    row_t = lax.broadcasted_iota(jnp.int32, (c, 2 * c), 0)
    col_t = lax.broadcasted_iota(jnp.int32, (c, 2 * c), 1)
    tri2 = (row_t >= col_t % c).astype(BF16)

    def head_sums(xs):
        x = jnp.concatenate(xs, axis=0)
        out = _dot(jnp.concatenate(_split_bf16(x), axis=1), ones_bd2)
        return [out[i * c:(i + 1) * c] for i in range(len(xs))]

    def stack(x):
        x = x.astype(BF16)
        zero = jnp.zeros_like(x)
        return jnp.concatenate([jnp.where(head0, x, zero), jnp.where(head0, zero, x)], axis=0)

    cum_all = [_dot(tri2, jnp.concatenate(_split_bf16(lw_ref[q]), axis=0)) for q in range(n_seq)]

    pairs = range(n_seq * n_pairs)
    seqs = [p // n_pairs for p in pairs]
    sls = [slice((p % n_pairs) * LANES, (p % n_pairs + 1) * LANES) for p in pairs]
    ins_ = [(r_ref[q, :, sl], k_ref[q, :, sl], v_ref[q, :, sl], lw_ref[q, :, sl], a_ref[q, :, sl])
            for q, sl in zip(seqs, sls)]
    kku_ = [k_raw * kk_ref[:, sl] for (_, k_raw, _, _, _), sl in zip(ins_, sls)]
    kk_sq_ = head_sums([kk * kk for kk in kku_])
    r_, v_, k_, atrt_, bhkh_, pe_, aa_ = [], [], [], [], [], [], []
    for (r, k_raw, v, lw, a_sig), q, sl, kk, kk_sq in zip(ins_, seqs, sls, kku_, kk_sq_):
        cum = cum_all[q][:, sl]
        kk = kk * lax.rsqrt(jnp.maximum(kk_sq, 1e-24))
        k = k_raw * (1.0 + (a_sig - 1.0) * ka_ref[:, sl])
        b_vec = kk * a_sig
        cum_last = cum[c - 1:c, :]
        inv_p = jnp.exp(-cum)
        to_end = jnp.exp(cum_last - cum)
        atrt = jnp.concatenate([-kk * jnp.exp(cum - lw), r * jnp.exp(cum)], axis=0).astype(BF16)
        aa_.append(_dot_nt(atrt, jnp.concatenate([stack(b_vec * inv_p), stack(k * inv_p)], axis=0)))
        r_.append(r); v_.append(v); k_.append(k); atrt_.append(atrt)
        bhkh_.append(jnp.concatenate([b_vec * to_end, k * to_end], axis=0).astype(BF16))
        pe_.append(jnp.exp(cum_last))

    a_ab_ = [jnp.where(strict_s, aa[:c, :LANES], 0.0) for aa in aa_]
    a_ak_ = [jnp.where(strict_s, aa[:c, LANES:], 0.0) for aa in aa_]
    a_r_ = [jnp.concatenate([jnp.where(incl_s, aa[c:, :LANES], 0.0), jnp.where(incl_s, aa[c:, LANES:], 0.0)], axis=1)
            for aa in aa_]

    t_inv_ = [eye_s + a for a in a_ab_]
    pw_ = [_dot(a.astype(BF16), stack(a)) for a in a_ab_]
    n_steps = c.bit_length() - 2
    for step in range(n_steps):
        if step < n_steps - 1:
            tp_ = [_dot(jnp.concatenate([t, pw], axis=0).astype(BF16), stack(pw)) for t, pw in zip(t_inv_, pw_)]
            t_inv_ = [t + tp[:c] for t, tp in zip(t_inv_, tp_)]
            pw_ = [tp[c:] for tp in tp_]
        else:
            t_inv_ = [t + _dot(t.astype(BF16), stack(pw)) for t, pw in zip(t_inv_, pw_)]

    vst_ = [stack(v) for v in v_]
    akv_ = [_dot(a_ak.astype(BF16), vst) for a_ak, vst in zip(a_ak_, vst_)]
    s0_ = [s_ref[p] for p in pairs]
    ars_ = [_dot_nt(atrt, s0.astype(BF16)) for atrt, s0 in zip(atrt_, s0_)]
    u_ = [_dot(t.astype(BF16), stack(ars[:c] + akv)) for t, ars, akv in zip(t_inv_, ars_, akv_)]
    y_ = [ars[c:] + _dot(a_r.astype(BF16), jnp.concatenate([stack(u), vst], axis=0))
          for ars, a_r, u, vst in zip(ars_, a_r_, u_, vst_)]
    upd_ = [_dot_tn(jnp.concatenate([u, v], axis=0).astype(BF16), bhkh) for u, v, bhkh in zip(u_, v_, bhkh_)]
    for p in pairs:
        s_ref[p] = s0_[p] * pe_[p] + jnp.where(same_head, upd_[p], 0.0)

    inv_n = 1.0 / HEAD_DIM
    dev_ = [y - mu * inv_n for y, mu in zip(y_, head_sums(y_))]
    var_ = [ss * inv_n for ss in head_sums([dev * dev for dev in dev_])]
    rk_ = head_sums([r * k * rk_ref[:, sl] for r, k, sl in zip(r_, k_, sls)])
    for p, sl in enumerate(sls):
        y_ref[seqs[p], :, sl] = (dev_[p] * lax.rsqrt(var_[p] + GN_EPS) * lg_ref[:, sl] + lb_ref[:, sl]
                                 + rk_[p] * v_[p])


def rwkv_recurrence(r, k, v, lw, a, k_k, k_a, r_k, lnx_g, lnx_b, batch):
    m, d = r.shape
    seq_rows = m // batch
    n_seq = math.gcd(batch, REC_SEQS)
    tile = pl.BlockSpec((n_seq, CHUNK, d), lambda b, c: (b, c, 0))
    vec = pl.BlockSpec((1, d), lambda b, c: (0, 0))
    vecs = [t.reshape(1, d) for t in (k_k, k_a, r_k, lnx_g, lnx_b)]
    seqs = [t.reshape(batch, seq_rows, d) for t in (r, k, v, lw, a)]
    return pl.pallas_call(
        _rwkv_rec_kernel,
        grid=(batch // n_seq, seq_rows // CHUNK),
        in_specs=[tile] * 5 + [vec] * 5,
        out_specs=tile,
        out_shape=jax.ShapeDtypeStruct((batch, seq_rows, d), F32),
        scratch_shapes=[pltpu.VMEM((n_seq * (d // LANES), LANES, LANES), F32)],
        compiler_params=_params(("parallel", "arbitrary")),
        name="rwkv_recurrence",
    )(*seqs, *vecs).reshape(m, d)


def _mix_ffn_kernel(has_gate, has_final, seq_rows, tm, tf, *refs):
    refs = list(refs)
    y_ref = refs.pop(0)
    g_ref = refs.pop(0) if has_gate else None
    h_ref, wo_ref, gain_ref, wup_ref, cw_ref, cb_ref, wd_ref = refs[:7]
    fgain_ref = refs[7] if has_final else None
    o_ref, hmid_ref, utail_ref, act_ref = refs[-4:]
    i = pl.program_id(0)
    d_ff = wd_ref.shape[0]
    keep = _row_in_seq(tm, i, seq_rows) >= PAD

    @pl.when(i == 0)
    def _():
        utail_ref[...] = jnp.zeros_like(utail_ref)

    y = y_ref[...] if g_ref is None else y_ref[...] * g_ref[...]
    hmid_ref[...] = jnp.where(keep, h_ref[...] + _dot(y.astype(BF16), wo_ref[...]), 0.0)
    xn = (_rms_hat(hmid_ref[...]) * gain_ref[...]).astype(BF16)

    def conv(col):
        u_tile = _dot(xn, wup_ref[:, col:col + tf])
        u = jnp.concatenate([utail_ref[:, col:col + tf], u_tile], axis=0)
        utail_ref[:, col:col + tf] = u_tile[tm - CONV_HALO:, :]
        out = cb_ref[:, col:col + tf] + u_tile * cw_ref[CONV_WIDTH - 1:CONV_WIDTH, col:col + tf]
        for tap in range(CONV_WIDTH - 1):
            back = CONV_WIDTH - 1 - tap
            out = out + pltpu.roll(u, back, 0)[CONV_HALO:, :] * cw_ref[tap:tap + 1, col:col + tf]
        return out

    for col in range(0, d_ff, tf):
        act_ref[:, col:col + tf] = (jax.nn.silu(conv(col)) * conv(d_ff + col)).astype(BF16)

    h_out = jnp.where(keep, hmid_ref[...] + _dot(act_ref[...], wd_ref[...]), 0.0)
    o_ref[...] = h_out if fgain_ref is None else _rms_hat(h_out) * fgain_ref[...]


def mix_ffn(y, gate, h, w_o, gain, w_up, conv_w, conv_b, w_down, final_gain, seq_rows, tm, tf):
    m, d = h.shape
    d_ff = w_down[0].shape[1]
    cw = jnp.pad(conv_w, ((0, 8 - CONV_WIDTH), (0, 0)))
    cb = conv_b.reshape(1, 2 * d_ff)
    tile = pl.BlockSpec((tm, d), lambda i: (i, 0))
    const = lambda shape: pl.BlockSpec(shape, lambda i: (0, 0), pipeline_mode=pl.Buffered(1))
    layer_of = lambda w: pl.BlockSpec((None,) + w[0].shape[1:], lambda i: (w[1], 0, 0), pipeline_mode=pl.Buffered(1))
    seqs = [y] + ([gate] if gate is not None else []) + [h]
    finals = [] if final_gain is None else [final_gain.reshape(1, d)]
    return pl.pallas_call(
        functools.partial(_mix_ffn_kernel, gate is not None, final_gain is not None, seq_rows, tm, tf),
        grid=(m // tm,),
        in_specs=[tile] * len(seqs) + [
            layer_of(w_o), const((1, d)), layer_of(w_up), const((8, 2 * d_ff)), const((1, 2 * d_ff)),
            layer_of(w_down)] + [const((1, d))] * len(finals),
        out_specs=tile,
        out_shape=jax.ShapeDtypeStruct((m, d), F32),
        scratch_shapes=[pltpu.VMEM((tm, d), F32), pltpu.VMEM((CONV_HALO, 2 * d_ff), F32),
                        pltpu.VMEM((tm, d_ff), BF16)],
        compiler_params=_params(("arbitrary",)),
        name="mix_ffn",
    )(*seqs, w_o[0], gain.reshape(1, d), w_up[0], cw, cb, w_down[0], *finals)


def _sb_attn_kernel(q_ref, k_ref, v_ref, o_ref):
    qi = pl.program_id(2)
    qb = q_ref.shape[0]
    n_groups = q_ref.shape[1] // LANES
    scale = HEAD_DIM ** -0.5
    assert math.frexp(scale)[0] == 0.5, "the score scale is folded into bf16 q, which needs a power of two"
    qs = [q_ref[:, g * LANES:(g + 1) * LANES] * jnp.asarray(scale, BF16) for g in range(n_groups)]
    head0_k = lax.broadcasted_iota(jnp.int32, (KEY_STEP, LANES), 1) < HEAD_DIM
    q_pos = qi * qb + lax.broadcasted_iota(jnp.int32, (qb, LANES), 0)
    row_b = lax.broadcasted_iota(jnp.int32, (2 * LANES, 2 * LANES), 0) % LANES
    col_b = lax.broadcasted_iota(jnp.int32, (2 * LANES, 2 * LANES), 1)
    same_head = (row_b // KEY_STEP) == ((col_b % LANES) // KEY_STEP)
    sum_mat = jnp.where(same_head & ((row_b > col_b) | (col_b >= LANES)), 1.0, 0.0).astype(BF16)

    def stack(x):
        zero = jnp.zeros_like(x)
        return jnp.concatenate([jnp.where(head0_k, x, zero), jnp.where(head0_k, zero, x)], axis=0)

    def key_rows(ref, j, g):
        start = pl.multiple_of(jnp.maximum(j, 0) * KEY_STEP, KEY_STEP)
        return stack(ref[pl.ds(start, KEY_STEP), g * LANES:(g + 1) * LANES])

    def valid_keys(j, pos):
        k_pos = j * KEY_STEP + lax.broadcasted_iota(jnp.int32, pos.shape, 1) % KEY_STEP
        return (k_pos < pos) & (k_pos >= PAD)

    def masked(ok, x):
        return x if ok is None else jnp.where(ok, x, 0.0)

    def scores(zs, valids):
        rows = zs[0].shape[0]
        log_betas = [jnp.minimum(z, 0.0) - jnp.log(1.0 + jnp.exp(-jnp.abs(z))) for z in zs]
        splits = [jnp.concatenate(_split_bf16(masked(ok, lb - z)), axis=1)
                  for lb, z, ok in zip(log_betas, zs, valids)]
        sums = _dot(jnp.concatenate(splits, axis=0), sum_mat)
        return log_betas, [sums[i * rows:(i + 1) * rows] for i in range(len(zs))]

    n_sub = qb // KEY_STEP

    def sweep(pad_free):
        sub_rows = [slice(sb * KEY_STEP, (sb + 1) * KEY_STEP) for sb in range(n_sub)]
        sub_steps = [[qi * n_sub + sb - t for t in range(ATTN_UNROLLED_STEPS)] for sb in range(n_sub)]
        sub_pos = [qi * qb + sb * KEY_STEP + lax.broadcasted_iota(jnp.int32, (KEY_STEP, LANES), 0) for sb in range(n_sub)]
        valids = [[valid_keys(j, pos) if (t == 0 or not pad_free) else None for t, j in enumerate(steps)]
                  for pos, steps in zip(sub_pos, sub_steps)]
        window = [qi * n_sub + n_sub - 1 - t for t in range(ATTN_UNROLLED_STEPS + n_sub - 1)]
        zs, tile_valid = [], []
        for g in range(n_groups):
            k_tiles = [key_rows(k_ref, j, g) for j in window]
            for sb in range(n_sub):
                lo = n_sub - 1 - sb
                keys = jnp.concatenate(k_tiles[lo:lo + ATTN_UNROLLED_STEPS], axis=0)
                z_all = _dot_nt(qs[g][sub_rows[sb]], keys)
                zs += [z_all[:, t * LANES:(t + 1) * LANES] for t in range(ATTN_UNROLLED_STEPS)]
                tile_valid += valids[sb]
        log_betas, sums = scores(zs, tile_valid)
        carries, accs = [], []
        for g in range(n_groups):
            v_tiles = [key_rows(v_ref, j, g) for j in window]
            carry_g, acc_g = [], []
            for sb in range(n_sub):
                carry = jnp.zeros((KEY_STEP, LANES), F32)
                ws = []
                for t in range(ATTN_UNROLLED_STEPS):
                    i = (g * n_sub + sb) * ATTN_UNROLLED_STEPS + t
                    ws.append(masked(valids[sb][t], jnp.exp(log_betas[i] + sums[i][:, :LANES] + carry)).astype(BF16))
                    carry = carry + sums[i][:, LANES:]
                lo = n_sub - 1 - sb
                values = jnp.concatenate(v_tiles[lo:lo + ATTN_UNROLLED_STEPS], axis=0)
                acc_g.append(_dot(jnp.concatenate(ws, axis=1), values))
                carry_g.append(carry)
            accs.append(jnp.concatenate(acc_g, axis=0))
            carries.append(jnp.concatenate(carry_g, axis=0))

        def alive(carries):
            top = carries[0]
            for c in carries[1:]:
                top = jnp.maximum(top, c)
            return (jnp.max(top) > LOG_W_FLOOR).astype(jnp.int32)

        def cond(state):
            j, live, _, _ = state
            return (j >= PAD // KEY_STEP) & (live > 0)

        def body(state):
            j, _, carries, accs = state
            ok = valid_keys(j, q_pos) & (j <= q_pos // KEY_STEP - ATTN_UNROLLED_STEPS)
            zs = [_dot_nt(qs[g], key_rows(k_ref, j, g)) for g in range(n_groups)]
            log_betas, sums = scores(zs, [ok] * n_groups)
            ws = [jnp.where(ok, jnp.exp(lb + sm[:, :LANES] + c), 0.0) for lb, sm, c in zip(log_betas, sums, carries)]
            accs = tuple(acc + _dot(w.astype(BF16), key_rows(v_ref, j, g)) for g, (acc, w) in enumerate(zip(accs, ws)))
            carries = tuple(c + sm[:, LANES:] for c, sm in zip(carries, sums))
            return j - 1, alive(carries), carries, accs

        rest = qi * n_sub + n_sub - 1 - ATTN_UNROLLED_STEPS
        state = lax.while_loop(cond, body, (rest, alive(carries), tuple(carries), tuple(accs)))
        for g, acc in enumerate(state[-1]):
            o_ref[:, g * LANES:(g + 1) * LANES] = acc.astype(o_ref.dtype)

    pad_free = (qi * n_sub - (ATTN_UNROLLED_STEPS - 1)) * KEY_STEP >= PAD
    pl.when(pad_free)(functools.partial(sweep, True))
    pl.when(jnp.logical_not(pad_free))(functools.partial(sweep, False))


def sb_attention(q, k, v, batch, groups):
    m, d = q.shape
    seq_rows = m // batch
    assert seq_rows % ATTN_Q == 0 and ATTN_Q % KEY_STEP == 0
    nq = seq_rows // ATTN_Q
    width = groups * LANES
    whole_seq = pl.BlockSpec((seq_rows, width), lambda b, p, i: (b, p), pipeline_mode=pl.Buffered(ATTN_KV_BUFFERS))
    return pl.pallas_call(
        _sb_attn_kernel,
        grid=(batch, d // width, nq),
        in_specs=[pl.BlockSpec((ATTN_Q, width), lambda b, p, i: (b * nq + i, p)), whole_seq, whole_seq],
        out_specs=pl.BlockSpec((ATTN_Q, width), lambda b, p, i: (b * nq + i, p)),
        out_shape=jax.ShapeDtypeStruct((m, d), BF16),
        compiler_params=_params(("parallel", "parallel", "arbitrary")),
        name="sb_attention",
    )(q, k, v)


def _row_tile(m, want):
    tm = want
    while m % tm:
        tm //= 2
    return tm


def kernel(x, meta_tokens, norm_mix_g, norm_ffn_g, ffn_up, ffn_conv_w, ffn_conv_b, ffn_down, rw_mix, rw_wr, rw_wk, rw_wv, rw_wo, rw_w0, rw_w1, rw_w2, rw_a0, rw_a1, rw_a2, rw_g1, rw_g2, rw_kk, rw_ka, rw_rk, rw_lnx_g, rw_lnx_b, rw_v0, rw_v1, rw_v2, kv_norm_g, sb_wk, sb_wv, sb_wq, sb_wo, final_norm_g):
    bsz, seq, d = x.shape
    depth = norm_mix_g.shape[0]
    n_a = rw_wr.shape[0]
    d_ff = ffn_down.shape[1]
    seq_rows = PAD + N_META + seq
    assert seq % BLOCK == 0 and d % LANES == 0 and seq_rows % CHUNK == 0
    m = bsz * seq_rows
    tm_big = _row_tile(m, 512)
    tm_ffn = 640 if m % 640 == 0 else tm_big
    tm_proj = _row_tile(m, 512)
    tf = 256 if d_ff % 256 == 0 else LANES

    meta = jnp.broadcast_to(meta_tokens.astype(x.dtype)[None], (bsz, N_META, d))
    h = jnp.concatenate([jnp.zeros((bsz, PAD, d), x.dtype), meta, x], axis=1).reshape(m, d)

    ffn_up_b, ffn_down_b, rw_wo_b, sb_wo_b = (w.astype(BF16) for w in (ffn_up, ffn_down, rw_wo, sb_wo))
    rw_wr_b, rw_wk_b, rw_wv_b = (w.astype(BF16) for w in (rw_wr, rw_wk, rw_wv))

    v_first = None
    k_sh = v_sh = None
    for layer in range(depth):
        if layer < n_a:
            i = layer
            vres = (None, None, None) if i == 0 else (rw_v0[i - 1], rw_v1[i - 1], rw_v2[i - 1])
            r, k, v, lw, a, g = rwkv_proj(
                h, norm_mix_g[layer], rw_mix[i], rw_w0[i], rw_a0[i], vres[0], (rw_wr_b, i), (rw_wk_b, i), (rw_wv_b, i),
                rw_w1[i], rw_w2[i], rw_a1[i], rw_a2[i], rw_g1[i], rw_g2[i], vres[1], vres[2],
                None if i == 0 else v_first, seq_rows, tm_proj)
            if i == 0:
                v_first = v
            y = rwkv_recurrence(r, k, v, lw, a, rw_kk[i], rw_ka[i], rw_rk[i].reshape(-1),
                                rw_lnx_g[i], rw_lnx_b[i], bsz)
            mixer = (y, g, (rw_wo_b, i))
        else:
            j = layer - n_a
            if layer == n_a:
                q, k_sh, v_sh = norm_proj(
                    h, [norm_mix_g[layer], kv_norm_g, kv_norm_g],
                    [sb_wq[j].astype(BF16), sb_wk.astype(BF16), sb_wv.astype(BF16)], BF16, tm_big)
            else:
                (q,) = norm_proj(h, [norm_mix_g[layer]], [sb_wq[j].astype(BF16)], BF16, tm_big)
            o = sb_attention(q, k_sh, v_sh, bsz, ATTN_GROUPS if (d // LANES) % ATTN_GROUPS == 0 else 1)
            mixer = (o, None, (sb_wo_b, j))
        h = mix_ffn(mixer[0], mixer[1], h, mixer[2], norm_ffn_g[layer], (ffn_up_b, layer), ffn_conv_w[layer],
                    ffn_conv_b[layer], (ffn_down_b, layer), final_norm_g if layer == depth - 1 else None,
                    seq_rows, tm_ffn, tf)
    return h.reshape(bsz, seq_rows, d)[:, PAD + N_META:, :]
```
